```python
import jax
import jax.numpy as jnp
from jax import lax
import numpy as np

D_MODEL = 1024
BATCH = 4
SEQ = 8192
DEPTH = 2

GRID_W = 64
CTX_LEN = 256
D_CONV = 512
CONV_K = 31
N_NA_HEADS = 8
NA_HEAD_DIM = 64
D_NA = N_NA_HEADS * NA_HEAD_DIM
NA_ROWS = 8
NA_COLS = 16
D_RNN = 512
RNN_BLOCKS = 8
RNN_BLOCK_W = D_RNN // RNN_BLOCKS
RNN_CONV_K = 4
LRU_C = 8.0
N_EXPERTS = 16
N_GROUPS = 4
EXPERTS_PER_GROUP = N_EXPERTS // N_GROUPS
TOP_K = 2
D_EXPERT = 512
N_MOD = 6
IN_SIZES = (D_CONV, D_CONV, D_NA, D_NA, D_NA, D_RNN, D_RNN, D_MODEL, D_MODEL, D_MODEL)
D_IN = sum(IN_SIZES)
EPS = 1e-6
NEG_INF = -1e30

kernel_name = "hybrid_dit_conv_natten_rglru_groupmoe"


def _rmsnorm(x, g):
    xf = x.astype(jnp.float32)
    y = xf * lax.rsqrt(jnp.mean(xf * xf, axis=-1, keepdims=True) + EPS)
    return (y * g.astype(jnp.float32)).astype(x.dtype)


def _layernorm(x, g, b):
    xf = x.astype(jnp.float32)
    mu = jnp.mean(xf, axis=-1, keepdims=True)
    var = jnp.mean(jnp.square(xf - mu), axis=-1, keepdims=True)
    y = (xf - mu) * lax.rsqrt(var + EPS)
    return (y * g.astype(jnp.float32) + b.astype(jnp.float32)).astype(x.dtype)


def _modulate(h, shift, scale):
    return h * (1.0 + scale) + shift


def _split_in(z):
    offs = np.cumsum(IN_SIZES)[:-1].tolist()
    return jnp.split(z, offs, axis=-1)


def _dwconv(x, w, b, pad):
    y = lax.conv_general_dilated(
        x, w[:, None, :].astype(x.dtype), window_strides=(1,), padding=[pad],
        dimension_numbers=("NWC", "WIO", "NWC"), feature_group_count=x.shape[-1])
    return y + b.astype(x.dtype)


def _conformer_conv(a, gate, dw_w, dw_b, ln_g, ln_b, pw_w):
    u = a * jax.nn.sigmoid(gate)
    u = _dwconv(u, dw_w, dw_b, (CONV_K // 2, CONV_K // 2))
    u = jax.nn.silu(_layernorm(u, ln_g, ln_b))
    return u @ pw_w


def _heads(t):
    return t.reshape(t.shape[0], t.shape[1], N_NA_HEADS, NA_HEAD_DIM)


def _context_attention(qc, kc, vc):
    scale = NA_HEAD_DIM ** -0.5
    s = jnp.einsum("bqhd,bkhd->bhqk", qc, kc).astype(jnp.float32) * scale
    p = jax.nn.softmax(s, axis=-1)
    o = jnp.einsum("bhqk,bkhd->bqhd", p, vc.astype(jnp.float32))
    return o.astype(qc.dtype).reshape(qc.shape[0], qc.shape[1], D_NA)


def _neighbourhood_attention(q, k, v, kc, vc, rpb):
    b, s, h, d = q.shape
    rows = s // GRID_W
    kr = min(NA_ROWS, rows)
    scale = d ** -0.5
    qg = q.reshape(b, rows, GRID_W, h, d)
    kg = k.reshape(b, rows, GRID_W, h, d)
    vg = v.reshape(b, rows, GRID_W, h, d)
    col = jnp.arange(GRID_W)
    col_start = jnp.clip(col - NA_COLS // 2, 0, GRID_W - NA_COLS)
    col_mask = (col[None, :] >= col_start[:, None]) & (col[None, :] < col_start[:, None] + NA_COLS)
    dc_idx = jnp.clip(col[None, :] - col[:, None] + NA_COLS - 1, 0, 2 * NA_COLS - 2)
    rpb_f = rpb.astype(jnp.float32)
    vc_f = vc.astype(jnp.float32)
    n_loc = kr * GRID_W

    def one_row(r):
        rs = jnp.clip(r - NA_ROWS // 2, 0, rows - kr)
        qr = lax.dynamic_index_in_dim(qg, r, axis=1, keepdims=False)
        kb = lax.dynamic_slice_in_dim(kg, rs, kr, axis=1)
        vb = lax.dynamic_slice_in_dim(vg, rs, kr, axis=1)
        s_loc = jnp.einsum("bqhd,brkhd->bhqrk", qr, kb).astype(jnp.float32) * scale
        dr = rs + jnp.arange(kr) - r
        bias = rpb_f[:, dr + NA_ROWS - 1][:, :, dc_idx]
        s_loc = s_loc + jnp.transpose(bias, (0, 2, 1, 3))[None]
        s_loc = jnp.where(col_mask[None, None, :, None, :], s_loc, NEG_INF)
        s_ctx = jnp.einsum("bqhd,bchd->bhqc", qr, kc).astype(jnp.float32) * scale
        p = jax.nn.softmax(jnp.concatenate([s_loc.reshape(b, h, GRID_W, n_loc), s_ctx], axis=-1), axis=-1)
        p_loc = p[..., :n_loc].reshape(b, h, GRID_W, kr, GRID_W)
        o = (jnp.einsum("bhqrk,brkhd->bqhd", p_loc, vb.astype(jnp.float32))
             + jnp.einsum("bhqc,bchd->bqhd", p[..., n_loc:], vc_f))
        return o.astype(q.dtype)

    out = lax.map(one_row, jnp.arange(rows))
    return jnp.transpose(out, (1, 0, 2, 3, 4)).reshape(b, s, h * d)


def _block_diag(x, w, bias):
    bsz, length, _ = x.shape
    xb = x.reshape(bsz, length, RNN_BLOCKS, RNN_BLOCK_W)
    return jnp.einsum("blnk,nkj->blnj", xb, w).reshape(bsz, length, D_RNN) + bias


def _rglru_coeffs(u, wa, ba, wx, bx, lam):
    f32 = jnp.float32
    uf = u.astype(f32)
    r = jax.nn.sigmoid(_block_diag(uf, wa.astype(f32), ba.astype(f32)))
    i = jax.nn.sigmoid(_block_diag(uf, wx.astype(f32), bx.astype(f32)))
    log_a = -LRU_C * r * jax.nn.softplus(-lam.astype(f32))
    a = jnp.exp(log_a)
    mult = jnp.sqrt(-jnp.expm1(2.0 * log_a))
    return a, mult * (i * uf)


def _linear_scan(a, bx, h0):
    bx = bx.at[:, 0].add(a[:, 0] * h0)

    def combine(left, right):
        a_l, b_l = left
        a_r, b_r = right
        return a_l * a_r, a_r * b_l + b_r

    _, h = lax.associative_scan(combine, (a, bx), axis=1)
    return h


def _rglru_direction(xc, xl, conv_w, conv_b, wa, ba, wx, bx, lam, with_ctx_out):
    pad = (RNN_CONV_K - 1, 0)
    ac, bc = _rglru_coeffs(_dwconv(xc, conv_w, conv_b, pad), wa, ba, wx, bx, lam)
    al, bl = _rglru_coeffs(_dwconv(xl, conv_w, conv_b, pad), wa, ba, wx, bx, lam)
    hc = _linear_scan(ac, bc, jnp.zeros_like(ac[:, 0]))
    hl = _linear_scan(al, bl, hc[:, -1])
    return hl, (hc if with_ctx_out else None)


def _griffin_merge(gate_branch, h_sum, out_w):
    return (jax.nn.gelu(gate_branch) * h_sum.astype(gate_branch.dtype)) @ out_w


def _moe(h, router_w, router_bias, w1, w3, w2):
    f32 = jnp.float32
    scores = jax.nn.sigmoid(h.astype(f32) @ router_w.astype(f32))
    sel = scores + router_bias.astype(f32)
    grp = sel.reshape(sel.shape[:-1] + (N_GROUPS, EXPERTS_PER_GROUP))
    grp_score = jnp.sum(lax.top_k(grp, TOP_K)[0], axis=-1)
    g_idx = jnp.argmax(grp_score, axis=-1)
    in_grp = jnp.take_along_axis(grp, g_idx[..., None, None], axis=-2)[..., 0, :]
    _, local = lax.top_k(in_grp, TOP_K)
    e_idx = g_idx[..., None] * EXPERTS_PER_GROUP + local
    w_sel = jnp.take_along_axis(scores, e_idx, axis=-1)
    w_sel = w_sel / jnp.sum(w_sel, axis=-1, keepdims=True)
    gate = jnp.einsum("blk,blke->ble", w_sel,
                      jax.nn.one_hot(e_idx, N_EXPERTS, dtype=f32)).astype(h.dtype)
    out = jnp.zeros_like(h)
    for e in range(N_EXPERTS):
        he = jax.nn.silu(h @ w1[e]) * (h @ w3[e])
        out = out + gate[..., e:e + 1] * (he @ w2[e])
    return out


def setup_inputs(seed: int = 0) -> dict:
    key = jax.random.key(seed)
    keys = iter(jax.random.split(key, 48))
    f32 = jnp.float32

    def normal(shape, std=1.0):
        return std * jax.random.normal(next(keys), shape, f32)

    def dense(shape, fan_in, mult=1.0):
        return normal(shape, mult * fan_in ** -0.5)

    def gain(shape):
        return 1.0 + normal(shape, 0.05)

    L = DEPTH
    u = jax.random.uniform(next(keys), (L, 2, D_RNN), f32, 0.9, 0.999)
    a0 = u ** (1.0 / LRU_C)
    return {
        "x": normal((BATCH, SEQ, D_MODEL)),
        "c": normal((BATCH, D_MODEL)),
        "ctx": normal((BATCH, CTX_LEN, D_MODEL)),
        "c_ctx": normal((D_MODEL,)),
        "router_w": dense((D_MODEL, N_EXPERTS), D_MODEL),
        "router_bias": normal((N_EXPERTS,), 0.01),
        "mod_w": dense((L, D_MODEL, N_MOD * D_MODEL), D_MODEL, 0.5),
        "mod_b": normal((L, N_MOD * D_MODEL), 0.02),
        "norm1_g": gain((L, D_MODEL)),
        "norm2_g": gain((L, D_MODEL)),
        "in_w": dense((L, D_MODEL, D_IN), D_MODEL),
        "in_b": normal((L, D_IN), 0.02),
        "conv_dw_w": dense((L, CONV_K, D_CONV), CONV_K),
        "conv_dw_b": normal((L, D_CONV), 0.02),
        "conv_ln_g": gain((L, D_CONV)),
        "conv_ln_b": normal((L, D_CONV), 0.02),
        "conv_pw_w": dense((L, D_CONV, D_MODEL), D_CONV),
        "na_q_g": gain((L, NA_HEAD_DIM)),
        "na_k_g": gain((L, NA_HEAD_DIM)),
        "na_rpb": normal((L, N_NA_HEADS, 2 * NA_ROWS - 1, 2 * NA_COLS - 1), 0.1),
        "na_out_w": dense((L, D_NA, D_MODEL), D_NA),
        "rnn_conv_w": dense((L, 2, RNN_CONV_K, D_RNN), RNN_CONV_K),
        "rnn_conv_b": normal((L, 2, D_RNN), 0.02),
        "rnn_wa": dense((L, 2, RNN_BLOCKS, RNN_BLOCK_W, RNN_BLOCK_W), RNN_BLOCK_W),
        "rnn_ba": normal((L, 2, D_RNN), 0.02),
        "rnn_wx": dense((L, 2, RNN_BLOCKS, RNN_BLOCK_W, RNN_BLOCK_W), RNN_BLOCK_W),
        "rnn_bx": normal((L, 2, D_RNN), 0.02),
        "rnn_lam": jnp.log(a0) - jnp.log1p(-a0),
        "rnn_out_w": dense((L, D_RNN, D_MODEL), D_RNN),
        "out_w": dense((L, D_MODEL, D_MODEL), D_MODEL),
        "exp_w1": dense((L, N_EXPERTS, D_MODEL, D_EXPERT), D_MODEL),
        "exp_w3": dense((L, N_EXPERTS, D_MODEL, D_EXPERT), D_MODEL),
        "exp_w2": dense((L, N_EXPERTS, D_EXPERT, D_MODEL), D_EXPERT),
    }


def reference(x, c, ctx, c_ctx, router_w, router_bias, mod_w, mod_b, norm1_g, norm2_g,
              in_w, in_b, conv_dw_w, conv_dw_b, conv_ln_g, conv_ln_b, conv_pw_w,
              na_q_g, na_k_g, na_rpb, na_out_w, rnn_conv_w, rnn_conv_b, rnn_wa, rnn_ba,
              rnn_wx, rnn_bx, rnn_lam, rnn_out_w, out_w, exp_w1, exp_w3, exp_w2):
    c_act = jax.nn.silu(c)
    cctx_act = jax.nn.silu(c_ctx)
    xl, xc = x, ctx
    for l in range(DEPTH):
        ctx_out = l < DEPTH - 1
        mod_l = (c_act @ mod_w[l] + mod_b[l])[:, None, :]
        mod_c = cctx_act @ mod_w[l] + mod_b[l]
        sh1_l, sc1_l, g1_l, sh2_l, sc2_l, g2_l = jnp.split(mod_l, N_MOD, axis=-1)
        sh1_c, sc1_c, g1_c, sh2_c, sc2_c, g2_c = jnp.split(mod_c, N_MOD, axis=-1)

        hl = _modulate(_rmsnorm(xl, norm1_g[l]), sh1_l, sc1_l)
        hc = _modulate(_rmsnorm(xc, norm1_g[l]), sh1_c, sc1_c)
        a_l, ga_l, q_l, k_l, v_l, rx_l, rg_l, gcv_l, gna_l, grn_l = _split_in(hl @ in_w[l] + in_b[l])
        a_c, ga_c, q_c, k_c, v_c, rx_c, rg_c, gcv_c, gna_c, grn_c = _split_in(hc @ in_w[l] + in_b[l])

        conv_l = _conformer_conv(a_l, ga_l, conv_dw_w[l], conv_dw_b[l], conv_ln_g[l],
                                 conv_ln_b[l], conv_pw_w[l])

        kn_c = _rmsnorm(_heads(k_c), na_k_g[l])
        vh_c = _heads(v_c)
        na_l = _neighbourhood_attention(_rmsnorm(_heads(q_l), na_q_g[l]),
                                        _rmsnorm(_heads(k_l), na_k_g[l]), _heads(v_l),
                                        kn_c, vh_c, na_rpb[l]) @ na_out_w[l]

        fwd = (rnn_conv_w[l, 0], rnn_conv_b[l, 0], rnn_wa[l, 0], rnn_ba[l, 0],
               rnn_wx[l, 0], rnn_bx[l, 0], rnn_lam[l, 0])
        bwd = (rnn_conv_w[l, 1], rnn_conv_b[l, 1], rnn_wa[l, 1], rnn_ba[l, 1],
               rnn_wx[l, 1], rnn_bx[l, 1], rnn_lam[l, 1])
        hf_l, hf_c = _rglru_direction(rx_c, rx_l, *fwd, ctx_out)
        hb_l, hb_c = _rglru_direction(rx_c[:, ::-1], rx_l[:, ::-1], *bwd, ctx_out)
        rnn_l = _griffin_merge(rg_l, hf_l + hb_l[:, ::-1], rnn_out_w[l])

        y_l = (jax.nn.sigmoid(gcv_l) * conv_l + jax.nn.sigmoid(gna_l) * na_l
               + jax.nn.sigmoid(grn_l) * rnn_l) @ out_w[l]
        xl = xl + g1_l * y_l
        if ctx_out:
            conv_c = _conformer_conv(a_c, ga_c, conv_dw_w[l], conv_dw_b[l], conv_ln_g[l],
                                     conv_ln_b[l], conv_pw_w[l])
            na_c = _context_attention(_rmsnorm(_heads(q_c), na_q_g[l]), kn_c, vh_c) @ na_out_w[l]
            rnn_c = _griffin_merge(rg_c, hf_c + hb_c[:, ::-1], rnn_out_w[l])
            y_c = (jax.nn.sigmoid(gcv_c) * conv_c + jax.nn.sigmoid(gna_c) * na_c
                   + jax.nn.sigmoid(grn_c) * rnn_c) @ out_w[l]
            xc = xc + g1_c * y_c

        hl2 = _modulate(_rmsnorm(xl, norm2_g[l]), sh2_l, sc2_l)
        xl = xl + g2_l * _moe(hl2, router_w, router_bias, exp_w1[l], exp_w3[l], exp_w2[l])
        if ctx_out:
            hc2 = _modulate(_rmsnorm(xc, norm2_g[l]), sh2_c, sc2_c)
            xc = xc + g2_c * _moe(hc2, router_w, router_bias, exp_w1[l], exp_w3[l], exp_w2[l])
    return xl
```

```python
import functools
import math

import jax
import jax.numpy as jnp
from jax import lax
from jax.experimental import pallas as pl
from jax.experimental.pallas import tpu as pltpu

F32 = jnp.float32
BF16 = jnp.bfloat16

GRID_W = 64
CONV_K = 31
N_HEADS = 8
HEAD_DIM = 64
NA_ROWS = 8
NA_COLS = 16
RNN_BLOCKS = 8
RNN_CONV_K = 4
LRU_C = 8.0
N_EXPERTS = 16
N_GROUPS = 4
EXPERTS_PER_GROUP = N_EXPERTS // N_GROUPS
N_MOD = 6
EPS = 1e-6
NEG_INF = -1e30

TOKEN_TILE = 256
HALO = 16
SUBLANES = 8
LANES = 128
VMEM_LIMIT = 56 * 1024 * 1024


def _params(sem, vmem=VMEM_LIMIT):
    return pltpu.CompilerParams(dimension_semantics=sem, vmem_limit_bytes=vmem)


def _sigmoid(x):
    return 1.0 / (1.0 + jnp.exp(-x))


def _silu(x):
    return x * _sigmoid(x)


def _gelu_tanh(x):
    return x * (0.5 * (1.0 + jnp.tanh(math.sqrt(2.0 / math.pi) * (x + 0.044715 * (x * x * x)))))


def _dot(a, b):
    return jnp.dot(a, b, preferred_element_type=F32)


def _dot_nt(a, b):
    return lax.dot_general(a, b, (((1,), (1,)), ((), ())), preferred_element_type=F32)


def _mod_kernel(c_ref, w_ref, b_ref, o_ref):
    a = _silu(c_ref[...])
    o_ref[0] = jnp.dot(a, w_ref[0], preferred_element_type=F32,
                       precision=lax.Precision.HIGHEST) + b_ref[0]


def _mod_vectors(c_all, mod_w, mod_b):
    depth, d, n = mod_w.shape
    tn = 1536
    rows = c_all.shape[0]
    return pl.pallas_call(
        _mod_kernel,
        grid=(depth, n // tn),
        in_specs=[pl.BlockSpec((rows, d), lambda l, j: (0, 0)),
                  pl.BlockSpec((1, d, tn), lambda l, j: (l, 0, j)),
                  pl.BlockSpec((1, 1, tn), lambda l, j: (l, 0, j))],
        out_specs=pl.BlockSpec((1, rows, tn), lambda l, j: (l, 0, j)),
        out_shape=jax.ShapeDtypeStruct((depth, rows, n), F32),
        compiler_params=_params(("arbitrary", "arbitrary")),
        name="mod_vectors",
    )(c_all, mod_w, mod_b.reshape(depth, 1, n))


def _mod_spec(d, which):
    return pl.BlockSpec((1, 1, d),
                        lambda b, i: ((2 * b + jnp.minimum(i, 1)) * N_MOD + which, 0, 0))


def _inproj_kernel(sizes, x_ref, sh_ref, sc_ref, g_ref, w_ref, b_ref, qg_ref, kg_ref, bd_ref,
                   u_ref, q_ref, k_ref, v_ref, rx_ref, rg_ref, gt_ref):
    dc, dn, dr, d = sizes
    x = x_ref[0]
    h = x * lax.rsqrt(jnp.mean(x * x, axis=-1, keepdims=True) + EPS) * g_ref[...]
    h = h * (1.0 + sc_ref[0]) + sh_ref[0]
    hb = h.astype(BF16)

    def proj(off, n):
        return _dot(hb, w_ref[:, off:off + n]) + b_ref[:, off:off + n]

    def head_norm(t, gain_ref):
        ms = _dot((t * t).astype(BF16), bd_ref[...])
        return t * lax.rsqrt(ms + EPS) * gain_ref[...]

    off = 0
    a = proj(off, dc); off += dc
    ga = proj(off, dc); off += dc
    u_ref[0] = (a * _sigmoid(ga)).astype(BF16)
    q_ref[0] = head_norm(proj(off, dn), qg_ref).astype(BF16); off += dn
    k_ref[0] = head_norm(proj(off, dn), kg_ref).astype(BF16); off += dn
    v_ref[0] = proj(off, dn).astype(BF16); off += dn
    rx_ref[0] = proj(off, dr).astype(BF16); off += dr
    rg_ref[0] = _gelu_tanh(proj(off, dr)).astype(BF16); off += dr
    for j in range(3):
        gt_ref[0, :, j * d:(j + 1) * d] = _sigmoid(proj(off, d)).astype(BF16); off += d


def _inproj(xa, modtab, g, w, b, qg, kg, bd, sizes):
    bsz, l, d = xa.shape
    dc, dn, dr, _ = sizes
    d_in = w.shape[1]
    tm = TOKEN_TILE
    const = lambda shape: pl.BlockSpec(shape, lambda b, i: (0,) * len(shape))
    tile = lambda n: pl.BlockSpec((1, tm, n), lambda b, i: (b, i, 0))
    out = lambda n: jax.ShapeDtypeStruct((bsz, l, n), BF16)
    return pl.pallas_call(
        functools.partial(_inproj_kernel, sizes),
        grid=(bsz, l // tm),
        in_specs=[tile(d), _mod_spec(d, 0), _mod_spec(d, 1), const((1, d)),
                  pl.BlockSpec((d, d_in), lambda b, i: (0, 0), pipeline_mode=pl.Buffered(1)),
                  const((1, d_in)), const((1, dn)), const((1, dn)), const((dn, dn))],
        out_specs=[tile(dc), tile(dn), tile(dn), tile(dn), tile(dr), tile(dr), tile(3 * d)],
        out_shape=[out(dc), out(dn), out(dn), out(dn), out(dr), out(dr), out(3 * d)],
        compiler_params=_params(("arbitrary", "arbitrary")),
        name="inproj",
    )(xa, modtab, modtab, g, w, b, qg, kg, bd)


def _conv_kernel(u_ref, prev_ref, next_ref, w_ref, b_ref, lg_ref, lb_ref, o_ref, win_ref, y_ref):
    i = pl.program_id(1)
    n = pl.num_programs(1)
    tm = u_ref.shape[1]
    dc = u_ref.shape[2]
    prev_ok = i >= 2
    next_ok = jnp.logical_and(i >= 1, i < n - 1)
    win_ref[0:HALO, :] = jnp.where(prev_ok, prev_ref[0].astype(F32), 0.0)
    win_ref[HALO:HALO + tm, :] = u_ref[0].astype(F32)
    win_ref[HALO + tm:HALO + tm + HALO, :] = jnp.where(next_ok, next_ref[0].astype(F32), 0.0)
    rc = 64
    base = HALO - CONV_K // 2
    for cb in range(dc // LANES):
        cs = slice(cb * LANES, (cb + 1) * LANES)
        for r0 in range(0, tm, rc):
            acc = jnp.zeros((rc, LANES), F32)
            for k in range(CONV_K):
                acc = acc + win_ref[r0 + base + k:r0 + base + k + rc, cs] * w_ref[k:k + 1, cs]
            y_ref[r0:r0 + rc, cs] = acc
    y = y_ref[...] + b_ref[...]
    mu = jnp.mean(y, axis=-1, keepdims=True)
    yc = y - mu
    var = jnp.mean(yc * yc, axis=-1, keepdims=True)
    z = yc * lax.rsqrt(var + EPS) * lg_ref[...] + lb_ref[...]
    o_ref[0] = _silu(z).astype(BF16)


def _conv_branch(u, w, b, lg, lb):
    bsz, l, dc = u.shape
    tm = TOKEN_TILE
    hb = tm // HALO
    nh = l // HALO
    const = lambda shape: pl.BlockSpec(shape, lambda b, i: (0,) * len(shape))
    return pl.pallas_call(
        _conv_kernel,
        grid=(bsz, l // tm),
        in_specs=[pl.BlockSpec((1, tm, dc), lambda b, i: (b, i, 0)),
                  pl.BlockSpec((1, HALO, dc), lambda b, i: (b, jnp.maximum(i * hb - 1, 0), 0)),
                  pl.BlockSpec((1, HALO, dc), lambda b, i: (b, jnp.minimum((i + 1) * hb, nh - 1), 0)),
                  const(w.shape), const((1, dc)), const((1, dc)), const((1, dc))],
        out_specs=pl.BlockSpec((1, tm, dc), lambda b, i: (b, i, 0)),
        out_shape=jax.ShapeDtypeStruct((bsz, l, dc), BF16),
        scratch_shapes=[pltpu.VMEM((tm + 2 * HALO, dc), F32), pltpu.VMEM((tm, dc), F32)],
        compiler_params=_params(("arbitrary", "arbitrary")),
        name="conv_branch",
    )(u, u, u, w, b, lg, lb)


def _softmax_pv(parts):
    m = None
    for s, _ in parts:
        mi = jnp.max(s, axis=-1, keepdims=True)
        m = mi if m is None else jnp.maximum(m, mi)
    den = None
    acc = None
    for s, v in parts:
        p = jnp.exp(s - m)
        di = jnp.sum(p, axis=-1, keepdims=True)
        den = di if den is None else den + di
        pv = _dot(p.astype(BF16), v)
        acc = pv if acc is None else acc + pv
    return acc / den


def _attn_kernel(n_ctx, q_ref, k_ref, v_ref, strip_ref, o_ref):
    j = pl.program_id(1)
    tq = q_ref.shape[1]
    rows_per_step = tq // GRID_W
    n_rows = (k_ref.shape[1] - n_ctx) // GRID_W
    n_loc = NA_ROWS * GRID_W
    pairs = N_HEADS // 2

    def split_heads(q2):
        lane = lax.broadcasted_iota(jnp.int32, q2.shape, 1)
        zero = jnp.zeros_like(q2)
        return jnp.concatenate([jnp.where(lane < HEAD_DIM, q2, zero),
                                jnp.where(lane >= HEAD_DIM, q2, zero)], axis=0)

    def merge_heads(o, m):
        lane = lax.broadcasted_iota(jnp.int32, (m, LANES), 1)
        return jnp.where(lane < HEAD_DIM, o[0:m], o[m:2 * m])

    @pl.when(j == 0)
    def _():
        for hp in range(pairs):
            cs = slice(hp * LANES, (hp + 1) * LANES)
            qs = split_heads(q_ref[0, :, cs])
            s = _dot_nt(qs, k_ref[0, 0:n_ctx, cs])
            o = _softmax_pv([(s, v_ref[0, 0:n_ctx, cs])])
            o_ref[0, :, cs] = merge_heads(o, tq).astype(BF16)

    @pl.when(j > 0)
    def _():
        for rr in range(rows_per_step):
            r = (j - 1) * rows_per_step + rr
            rs = jnp.clip(r - NA_ROWS // 2, 0, n_rows - NA_ROWS)
            case = rs - r + (NA_ROWS - 1)
            kstart = pl.multiple_of(n_ctx + rs * GRID_W, GRID_W)
            rsl = slice(rr * GRID_W, (rr + 1) * GRID_W)
            for hp in range(pairs):
                cs = slice(hp * LANES, (hp + 1) * LANES)
                qs = split_heads(q_ref[0, rsl, cs])
                s_loc = _dot_nt(qs, k_ref[0, pl.ds(kstart, n_loc), cs])
                bias = jnp.concatenate([strip_ref[2 * hp, case], strip_ref[2 * hp + 1, case]], axis=0)
                s_ctx = _dot_nt(qs, k_ref[0, 0:n_ctx, cs])
                o = _softmax_pv([(s_loc + bias, v_ref[0, pl.ds(kstart, n_loc), cs]),
                                 (s_ctx, v_ref[0, 0:n_ctx, cs])])
                o_ref[0, rsl, cs] = merge_heads(o, GRID_W).astype(BF16)


def _attention(q, k, v, strips, n_ctx):
    bsz, l, dn = q.shape
    tq = TOKEN_TILE
    whole = lambda: pl.BlockSpec((1, l, dn), lambda b, j: (b, 0, 0), pipeline_mode=pl.Buffered(1))
    return pl.pallas_call(
        functools.partial(_attn_kernel, n_ctx),
        grid=(bsz, l // tq),
        in_specs=[pl.BlockSpec((1, tq, dn), lambda b, j: (b, j, 0)), whole(), whole(),
                  pl.BlockSpec(strips.shape, lambda b, j: (0, 0, 0, 0), pipeline_mode=pl.Buffered(1))],
        out_specs=pl.BlockSpec((1, tq, dn), lambda b, j: (b, j, 0)),
        out_shape=jax.ShapeDtypeStruct((bsz, l, dn), BF16),
        compiler_params=_params(("arbitrary", "arbitrary")),
        name="attention",
    )(q, k, v, strips)


def _bias_strips(rpb):
    col = jnp.arange(GRID_W)
    start = jnp.clip(col - NA_COLS // 2, 0, GRID_W - NA_COLS)
    mask = (col[None, :] >= start[:, None]) & (col[None, :] < start[:, None] + NA_COLS)
    dc_idx = jnp.clip(col[None, :] - col[:, None] + NA_COLS - 1, 0, 2 * NA_COLS - 2)
    bias = jnp.where(mask[None, None], rpb.astype(F32)[:, :, dc_idx], NEG_INF)
    cases = [jnp.concatenate([bias[:, c + j] for j in range(NA_ROWS)], axis=-1) for c in range(NA_ROWS)]
    return jnp.stack(cases, axis=1)


def _rglru_kernel(reverse, has_addend, *refs):
    if has_addend:
        x_ref, add_ref, cw_ref, cb_ref, wg_ref, bg_ref, nsp_ref, o_ref, ext_ref, h_ref = refs
    else:
        x_ref, cw_ref, cb_ref, wg_ref, bg_ref, nsp_ref, o_ref, ext_ref, h_ref = refs
        add_ref = None
    j = pl.program_id(1)
    tc = x_ref.shape[1]
    dr = x_ref.shape[2]
    sl = SUBLANES
    near = 0 if not reverse else sl + tc

    @pl.when(j == 0)
    def _():
        h_ref[...] = jnp.zeros_like(h_ref)

    @pl.when(j <= 1)
    def _():
        ext_ref[near:near + sl, :] = jnp.zeros((sl, dr), F32)

    x = x_ref[0].astype(F32)
    ext_ref[sl:sl + tc, :] = x
    u = jnp.zeros((tc, dr), F32) + cb_ref[...]
    for kk in range(RNN_CONV_K):
        shift = kk - (RNN_CONV_K - 1) if not reverse else (RNN_CONV_K - 1) - kk
        u = u + ext_ref[sl + shift:sl + shift + tc, :] * cw_ref[kk:kk + 1, :]
    ext_ref[near:near + sl, :] = x[tc - sl:tc] if not reverse else x[0:sl]

    z = _dot(u.astype(BF16), wg_ref[...]) + bg_ref[...]
    r = _sigmoid(z[:, 0:dr])
    ig = _sigmoid(z[:, dr:2 * dr])
    a = jnp.exp(r * nsp_ref[...])
    bx = jnp.sqrt(1.0 - a * a) * (ig * u)

    sub = lax.broadcasted_iota(jnp.int32, (sl, dr), 0)
    h = h_ref[...]
    groups = list(range(tc // sl))
    if reverse:
        groups = groups[::-1]
    outs = {}
    for g in groups:
        ag = a[g * sl:(g + 1) * sl]
        bg = bx[g * sl:(g + 1) * sl]
        for dist in (1, 2, 4):
            amt = dist if not reverse else sl - dist
            keep = (sub >= dist) if not reverse else (sub < sl - dist)
            ra = pltpu.roll(ag, amt, 0)
            rb = pltpu.roll(bg, amt, 0)
            bg = bg + ag * jnp.where(keep, rb, 0.0)
            ag = ag * jnp.where(keep, ra, 1.0)
        og = ag * h + bg
        outs[g] = og
        h = og[sl - 1:sl] if not reverse else og[0:1]
    h_ref[...] = h
    out = jnp.concatenate([outs[g] for g in range(tc // sl)], axis=0)
    if add_ref is not None:
        out = out + add_ref[0].astype(F32)
    o_ref[0] = out.astype(o_ref.dtype)


def _rglru(rx, addend, cw, cb, wg, bg, nsp, reverse):
    bsz, l, dr = rx.shape
    tc = TOKEN_TILE
    n = l // tc
    if reverse:
        chunk = lambda b, j: (b, jnp.where(j == 0, 0, n - j), 0)
    else:
        chunk = lambda b, j: (b, j, 0)
    const = lambda shape: pl.BlockSpec(shape, lambda b, j: (0,) * len(shape))
    tile = pl.BlockSpec((1, tc, dr), chunk)
    ins = [rx] + ([addend] if addend is not None else []) + [cw, cb, wg, bg, nsp]
    specs = [tile] + ([tile] if addend is not None else []) + [
        const(cw.shape), const((1, dr)), const(wg.shape), const((1, 2 * dr)), const((1, dr))]
    return pl.pallas_call(
        functools.partial(_rglru_kernel, reverse, addend is not None),
        grid=(bsz, n),
        in_specs=specs,
        out_specs=tile,
        out_shape=jax.ShapeDtypeStruct((bsz, l, dr), BF16),
        scratch_shapes=[pltpu.VMEM((tc + 2 * SUBLANES, dr), F32), pltpu.VMEM((1, dr), F32)],
        compiler_params=_params(("arbitrary", "arbitrary")),
        name="rglru_bwd" if reverse else "rglru_fwd",
    )(*ins)


def _merge_kernel(x_ref, g1_ref, cv_ref, at_ref, hs_ref, rg_ref, gt_ref,
                  pw_ref, no_ref, ro_ref, ow_ref, o_ref):
    d = x_ref.shape[2]
    conv = _dot(cv_ref[0], pw_ref[...])
    na = _dot(at_ref[0], no_ref[...])
    rnn = _dot(rg_ref[0] * hs_ref[0], ro_ref[...])
    m = (gt_ref[0, :, 0:d].astype(F32) * conv + gt_ref[0, :, d:2 * d].astype(F32) * na
         + gt_ref[0, :, 2 * d:3 * d].astype(F32) * rnn)
    y = _dot(m.astype(BF16), ow_ref[...])
    o_ref[0] = x_ref[0] + g1_ref[0] * y


def _merge(xa, modtab, cv, at, hs, rg, gt, pw, no, ro, ow):
    bsz, l, d = xa.shape
    tm = TOKEN_TILE
    tile = lambda n: pl.BlockSpec((1, tm, n), lambda b, i: (b, i, 0))
    const = lambda a: pl.BlockSpec(a.shape, lambda b, i: (0,) * a.ndim)
    return pl.pallas_call(
        _merge_kernel,
        grid=(bsz, l // tm),
        in_specs=[tile(d), _mod_spec(d, 2), tile(cv.shape[2]), tile(at.shape[2]), tile(hs.shape[2]),
                  tile(rg.shape[2]), tile(3 * d), const(pw), const(no), const(ro), const(ow)],
        out_specs=tile(d),
        out_shape=jax.ShapeDtypeStruct((bsz, l, d), F32),
        compiler_params=_params(("arbitrary", "arbitrary")),
        name="merge",
    )(xa, modtab, cv, at, hs, rg, gt, pw, no, ro, ow)


def _router_kernel(x_ref, sh_ref, sc_ref, g_ref, rw_ref, rb_ref, h_ref, gate_ref):
    x = x_ref[0]
    h = x * lax.rsqrt(jnp.mean(x * x, axis=-1, keepdims=True) + EPS) * g_ref[...]
    h = h * (1.0 + sc_ref[0]) + sh_ref[0]
    hb = h.astype(BF16)
    h_ref[0] = hb
    logits = lax.dot_general(rw_ref[...], h, (((1,), (1,)), ((), ())),
                             preferred_element_type=F32, precision=lax.Precision.HIGHEST)
    scores = _sigmoid(logits)
    sel = scores + rb_ref[...]
    row = lambda t, e: t[e:e + 1, :]
    epg = EXPERTS_PER_GROUP
    gscore = []
    for g in range(N_GROUPS):
        best = None
        for a in range(epg):
            for b in range(a + 1, epg):
                pair = row(sel, g * epg + a) + row(sel, g * epg + b)
                best = pair if best is None else jnp.maximum(best, pair)
        gscore.append(best)
    gbest = gscore[0]
    gidx = jnp.zeros_like(gbest, dtype=jnp.int32)
    for g in range(1, N_GROUPS):
        better = gscore[g] > gbest
        gidx = jnp.where(better, g, gidx)
        gbest = jnp.where(better, gscore[g], gbest)
    picked = []
    for e in range(N_EXPERTS):
        g = e // epg
        rank = jnp.zeros_like(gidx)
        for o in range(g * epg, (g + 1) * epg):
            if o == e:
                continue
            ahead = (row(sel, o) > row(sel, e)) if o > e else (row(sel, o) >= row(sel, e))
            rank = rank + ahead.astype(jnp.int32)
        chosen = jnp.logical_and(gidx == g, rank < 2)
        picked.append(jnp.where(chosen, row(scores, e), 0.0))
    den = picked[0]
    for e in range(1, N_EXPERTS):
        den = den + picked[e]
    gate_ref[...] = jnp.concatenate(picked, axis=0) / den


def _router(xa, modtab, g, rw_t, rb):
    bsz, l, d = xa.shape
    tm = TOKEN_TILE
    nt = l // tm
    const = lambda shape: pl.BlockSpec(shape, lambda b, i: (0,) * len(shape))
    return pl.pallas_call(
        _router_kernel,
        grid=(bsz, nt),
        in_specs=[pl.BlockSpec((1, tm, d), lambda b, i: (b, i, 0)), _mod_spec(d, 3), _mod_spec(d, 4),
                  const((1, d)), const(rw_t.shape), const(rb.shape)],
        out_specs=[pl.BlockSpec((1, tm, d), lambda b, i: (b, i, 0)),
                   pl.BlockSpec((N_EXPERTS, tm), lambda b, i: (0, b * nt + i))],
        out_shape=[jax.ShapeDtypeStruct((bsz, l, d), BF16),
                   jax.ShapeDtypeStruct((N_EXPERTS, bsz * l), F32)],
        compiler_params=_params(("arbitrary", "arbitrary")),
        name="router",
    )(xa, modtab, modtab, g, rw_t, rb)


def _moe_kernel(n_ctx, x_ref, g2c_ref, g2l_ref, h_ref, gate_ref, w1_ref, w3_ref, w2_ref, o_ref, acc_ref):
    i = pl.program_id(1)
    e = pl.program_id(2)
    tm = h_ref.shape[1]

    @pl.when(e == 0)
    def _():
        acc_ref[...] = jnp.zeros_like(acc_ref)

    h = h_ref[0]
    he = _silu(_dot(h, w1_ref[0])) * _dot(h, w3_ref[0])
    y = _dot(he.astype(BF16), w2_ref[0])
    gt = gate_ref[0]
    lane = lax.broadcasted_iota(jnp.int32, gt.shape, 1)
    gcol = jnp.sum(jnp.where(lane == e, gt, 0.0), axis=1, keepdims=True)
    acc_ref[...] += gcol * y

    @pl.when(e == pl.num_programs(2) - 1)
    def _():
        rowi = lax.broadcasted_iota(jnp.int32, (tm, 1), 0) + i * tm
        g2 = jnp.where(rowi < n_ctx, g2c_ref[0], g2l_ref[0])
        o_ref[0] = x_ref[0] + g2 * acc_ref[...]


def _moe_dense(xa, modtab, h2, gate, w1, w3, w2, n_ctx):
    bsz, l, d = xa.shape
    n_e, _, de = w1.shape
    tm = 768 if l % 768 == 0 else TOKEN_TILE
    tile = lambda n: pl.BlockSpec((1, tm, n), lambda b, i, e: (b, i, 0))
    g2 = lambda kind: pl.BlockSpec((1, 1, d), lambda b, i, e: ((2 * b + kind) * N_MOD + 5, 0, 0))
    return pl.pallas_call(
        functools.partial(_moe_kernel, n_ctx),
        grid=(bsz, l // tm, n_e),
        in_specs=[tile(d), g2(0), g2(1), tile(d), tile(n_e),
                  pl.BlockSpec((1, d, de), lambda b, i, e: (e, 0, 0)),
                  pl.BlockSpec((1, d, de), lambda b, i, e: (e, 0, 0)),
                  pl.BlockSpec((1, de, d), lambda b, i, e: (e, 0, 0))],
        out_specs=tile(d),
        out_shape=jax.ShapeDtypeStruct((bsz, l, d), F32),
        scratch_shapes=[pltpu.VMEM((tm, d), F32)],
        compiler_params=_params(("arbitrary", "arbitrary", "arbitrary")),
        name="moe_dense",
    )(xa, modtab, modtab, h2, gate, w1, w3, w2)


def _block_diag(w):
    n, k, _ = w.shape
    eye = jnp.eye(n, dtype=w.dtype)
    return jnp.einsum("nij,nm->nimj", w, eye).reshape(n * k, n * k)


def kernel(x, c, ctx, c_ctx, router_w, router_bias, mod_w, mod_b, norm1_g, norm2_g, in_w, in_b,
           conv_dw_w, conv_dw_b, conv_ln_g, conv_ln_b, conv_pw_w, na_q_g, na_k_g, na_rpb, na_out_w,
           rnn_conv_w, rnn_conv_b, rnn_wa, rnn_ba, rnn_wx, rnn_bx, rnn_lam, rnn_out_w, out_w,
           exp_w1, exp_w3, exp_w2):
    bsz, s, d = x.shape
    n_ctx = ctx.shape[1]
    depth = mod_w.shape[0]
    dc = conv_pw_w.shape[1]
    dn = na_out_w.shape[1]
    dr = rnn_out_w.shape[1]
    sizes = (dc, dn, dr, d)
    assert n_ctx % TOKEN_TILE == 0 and s % TOKEN_TILE == 0 and s % GRID_W == 0
    assert dn == N_HEADS * HEAD_DIM and s // GRID_W >= NA_ROWS

    xa = jnp.concatenate([ctx, x], axis=1)

    c_rows = -(-(bsz + 1) // SUBLANES) * SUBLANES
    c_all = jnp.zeros((c_rows, d), F32).at[:bsz].set(c).at[bsz].set(c_ctx)
    mods = _mod_vectors(c_all, mod_w, mod_b)

    head_mean = _block_diag(jnp.full((N_HEADS, HEAD_DIM, HEAD_DIM), 1.0 / HEAD_DIM, F32)).astype(BF16)
    rw_t = router_w.T
    rb = router_bias.reshape(N_EXPERTS, 1)

    for l in range(depth):
        ml = mods[l].reshape(c_rows, N_MOD, d)
        modtab = jnp.stack([jnp.broadcast_to(ml[bsz], (bsz, N_MOD, d)), ml[:bsz]], axis=1)
        modtab = modtab.reshape(bsz * 2 * N_MOD, 1, d)

        qg = (jnp.tile(na_q_g[l], N_HEADS) * HEAD_DIM ** -0.5).reshape(1, dn)
        kg = jnp.tile(na_k_g[l], N_HEADS).reshape(1, dn)
        u, q, k, v, rx, rg, gt = _inproj(xa, modtab, norm1_g[l].reshape(1, d), in_w[l].astype(BF16),
                                         in_b[l].reshape(1, -1), qg, kg, head_mean, sizes)

        cw = jnp.zeros((CONV_K + 1, dc), F32).at[:CONV_K].set(conv_dw_w[l])
        cv = _conv_branch(u, cw, conv_dw_b[l].reshape(1, dc), conv_ln_g[l].reshape(1, dc),
                          conv_ln_b[l].reshape(1, dc))

        at = _attention(q, k, v, _bias_strips(na_rpb[l]), n_ctx)

        hs = None
        for dirn in range(2):
            wg = jnp.concatenate([_block_diag(rnn_wa[l, dirn]), _block_diag(rnn_wx[l, dirn])], axis=1)
            bg = jnp.concatenate([rnn_ba[l, dirn], rnn_bx[l, dirn]]).reshape(1, 2 * dr)
            nsp = (-LRU_C * jax.nn.softplus(-rnn_lam[l, dirn])).reshape(1, dr)
            rcw = jnp.zeros((SUBLANES, dr), F32).at[:RNN_CONV_K].set(rnn_conv_w[l, dirn])
            hs = _rglru(rx, hs, rcw, rnn_conv_b[l, dirn].reshape(1, dr), wg.astype(BF16), bg, nsp,
                        reverse=(dirn == 1))

        xa = _merge(xa, modtab, cv, at, hs, rg, gt, conv_pw_w[l].astype(BF16), na_out_w[l].astype(BF16),
                    rnn_out_w[l].astype(BF16), out_w[l].astype(BF16))

        h2, gate_t = _router(xa, modtab, norm2_g[l].reshape(1, d), rw_t, rb)
        gate = gate_t.T.reshape(bsz, n_ctx + s, N_EXPERTS)
        xa = _moe_dense(xa, modtab, h2, gate, exp_w1[l].astype(BF16), exp_w3[l].astype(BF16),
                        exp_w2[l].astype(BF16), n_ctx)

    return xa[:, n_ctx:, :]
```

```python
import functools
import math

import jax
import jax.numpy as jnp
from jax import lax
from jax.experimental import pallas as pl
from jax.experimental.pallas import tpu as pltpu

F32 = jnp.float32
BF16 = jnp.bfloat16

GRID_W = 64
CONV_K = 31
N_HEADS = 8
HEAD_DIM = 64
NA_ROWS = 8
NA_COLS = 16
RNN_BLOCKS = 8
RNN_CONV_K = 4
LRU_C = 8.0
N_EXPERTS = 16
N_GROUPS = 4
EXPERTS_PER_GROUP = N_EXPERTS // N_GROUPS
N_MOD = 6
EPS = 1e-6
NEG_INF = -1e30

TOP_K = 2
TOKEN_TILE = 256
ROW_ALIGN = 16
ROW_SHIFT = 4
EXPERT_TILE = 256
SEG_SIZES = tuple(TOKEN_TILE >> k for k in range(TOKEN_TILE.bit_length() - ROW_SHIFT))
TAIL_SIZES = tuple((EXPERT_TILE // 2) >> k for k in range(EXPERT_TILE.bit_length() - 1 - ROW_SHIFT))
MXU_DIM = 256
STAGE_ROWS = -(-(TOP_K * TOKEN_TILE + N_EXPERTS * (ROW_ALIGN - 1)) // MXU_DIM) * MXU_DIM
HALO = 16
SUBLANES = 8
LANES = 128
VMEM_LIMIT = 56 * 1024 * 1024


def _params(sem, vmem=VMEM_LIMIT):
    return pltpu.CompilerParams(dimension_semantics=sem, vmem_limit_bytes=vmem)


def _sigmoid(x):
    return 1.0 / (1.0 + jnp.exp(-x))


def _silu(x):
    return x * _sigmoid(x)


def _gelu_tanh(x):
    return x * (0.5 * (1.0 + jnp.tanh(math.sqrt(2.0 / math.pi) * (x + 0.044715 * (x * x * x)))))


def _dot(a, b):
    return jnp.dot(a, b, preferred_element_type=F32)


def _dot_nt(a, b):
    return lax.dot_general(a, b, (((1,), (1,)), ((), ())), preferred_element_type=F32)


def _mod_kernel(c_ref, w_ref, b_ref, o_ref):
    a = _silu(c_ref[...])
    o_ref[0] = jnp.dot(a, w_ref[0], preferred_element_type=F32,
                       precision=lax.Precision.HIGHEST) + b_ref[0]


def _mod_vectors(c_all, mod_w, mod_b):
    depth, d, n = mod_w.shape
    tn = 1536
    rows = c_all.shape[0]
    return pl.pallas_call(
        _mod_kernel,
        grid=(depth, n // tn),
        in_specs=[pl.BlockSpec((rows, d), lambda l, j: (0, 0)),
                  pl.BlockSpec((1, d, tn), lambda l, j: (l, 0, j)),
                  pl.BlockSpec((1, 1, tn), lambda l, j: (l, 0, j))],
        out_specs=pl.BlockSpec((1, rows, tn), lambda l, j: (l, 0, j)),
        out_shape=jax.ShapeDtypeStruct((depth, rows, n), F32),
        compiler_params=_params(("arbitrary", "arbitrary")),
        name="mod_vectors",
    )(c_all, mod_w, mod_b.reshape(depth, 1, n))


def _mod_spec(d, which):
    return pl.BlockSpec((1, 1, d),
                        lambda b, i: ((2 * b + jnp.minimum(i, 1)) * N_MOD + which, 0, 0))


def _inproj_kernel(sizes, x_ref, sh_ref, sc_ref, g_ref, w_ref, b_ref, qg_ref, kg_ref, bd_ref,
                   u_ref, q_ref, k_ref, v_ref, rx_ref, rg_ref, gt_ref):
    dc, dn, dr, d = sizes
    x = x_ref[0]
    h = x * lax.rsqrt(jnp.mean(x * x, axis=-1, keepdims=True) + EPS) * g_ref[...]
    h = h * (1.0 + sc_ref[0]) + sh_ref[0]
    hb = h.astype(BF16)

    def proj(off, n):
        return _dot(hb, w_ref[:, off:off + n]) + b_ref[:, off:off + n]

    def head_norm(t, gain_ref):
        ms = _dot((t * t).astype(BF16), bd_ref[...])
        return t * lax.rsqrt(ms + EPS) * gain_ref[...]

    off = 0
    a = proj(off, dc); off += dc
    ga = proj(off, dc); off += dc
    u_ref[0] = (a * _sigmoid(ga)).astype(BF16)
    q_ref[0] = head_norm(proj(off, dn), qg_ref).astype(BF16); off += dn
    k_ref[0] = head_norm(proj(off, dn), kg_ref).astype(BF16); off += dn
    v_ref[0] = proj(off, dn).astype(BF16); off += dn
    rx_ref[0] = proj(off, dr).astype(BF16); off += dr
    rg_ref[0] = _gelu_tanh(proj(off, dr)).astype(BF16); off += dr
    for j in range(3):
        gt_ref[0, :, j * d:(j + 1) * d] = _sigmoid(proj(off, d)).astype(BF16); off += d


def _inproj(xa, modtab, g, w, b, qg, kg, bd, sizes):
    bsz, l, d = xa.shape
    dc, dn, dr, _ = sizes
    d_in = w.shape[1]
    tm = TOKEN_TILE
    const = lambda shape: pl.BlockSpec(shape, lambda b, i: (0,) * len(shape))
    tile = lambda n: pl.BlockSpec((1, tm, n), lambda b, i: (b, i, 0))
    out = lambda n: jax.ShapeDtypeStruct((bsz, l, n), BF16)
    return pl.pallas_call(
        functools.partial(_inproj_kernel, sizes),
        grid=(bsz, l // tm),
        in_specs=[tile(d), _mod_spec(d, 0), _mod_spec(d, 1), const((1, d)),
                  pl.BlockSpec((d, d_in), lambda b, i: (0, 0), pipeline_mode=pl.Buffered(1)),
                  const((1, d_in)), const((1, dn)), const((1, dn)), const((dn, dn))],
        out_specs=[tile(dc), tile(dn), tile(dn), tile(dn), tile(dr), tile(dr), tile(3 * d)],
        out_shape=[out(dc), out(dn), out(dn), out(dn), out(dr), out(dr), out(3 * d)],
        compiler_params=_params(("arbitrary", "arbitrary")),
        name="inproj",
    )(xa, modtab, modtab, g, w, b, qg, kg, bd)


def _conv_kernel(u_ref, prev_ref, next_ref, w_ref, b_ref, lg_ref, lb_ref, o_ref, win_ref, y_ref):
    i = pl.program_id(1)
    n = pl.num_programs(1)
    tm = u_ref.shape[1]
    dc = u_ref.shape[2]
    prev_ok = i >= 2
    next_ok = jnp.logical_and(i >= 1, i < n - 1)
    win_ref[0:HALO, :] = jnp.where(prev_ok, prev_ref[0].astype(F32), 0.0)
    win_ref[HALO:HALO + tm, :] = u_ref[0].astype(F32)
    win_ref[HALO + tm:HALO + tm + HALO, :] = jnp.where(next_ok, next_ref[0].astype(F32), 0.0)
    rc = 64
    base = HALO - CONV_K // 2
    for cb in range(dc // LANES):
        cs = slice(cb * LANES, (cb + 1) * LANES)
        for r0 in range(0, tm, rc):
            acc = jnp.zeros((rc, LANES), F32)
            for k in range(CONV_K):
                acc = acc + win_ref[r0 + base + k:r0 + base + k + rc, cs] * w_ref[k:k + 1, cs]
            y_ref[r0:r0 + rc, cs] = acc
    y = y_ref[...] + b_ref[...]
    mu = jnp.mean(y, axis=-1, keepdims=True)
    yc = y - mu
    var = jnp.mean(yc * yc, axis=-1, keepdims=True)
    z = yc * lax.rsqrt(var + EPS) * lg_ref[...] + lb_ref[...]
    o_ref[0] = _silu(z).astype(BF16)


def _conv_branch(u, w, b, lg, lb):
    bsz, l, dc = u.shape
    tm = TOKEN_TILE
    hb = tm // HALO
    nh = l // HALO
    const = lambda shape: pl.BlockSpec(shape, lambda b, i: (0,) * len(shape))
    return pl.pallas_call(
        _conv_kernel,
        grid=(bsz, l // tm),
        in_specs=[pl.BlockSpec((1, tm, dc), lambda b, i: (b, i, 0)),
                  pl.BlockSpec((1, HALO, dc), lambda b, i: (b, jnp.maximum(i * hb - 1, 0), 0)),
                  pl.BlockSpec((1, HALO, dc), lambda b, i: (b, jnp.minimum((i + 1) * hb, nh - 1), 0)),
                  const(w.shape), const((1, dc)), const((1, dc)), const((1, dc))],
        out_specs=pl.BlockSpec((1, tm, dc), lambda b, i: (b, i, 0)),
        out_shape=jax.ShapeDtypeStruct((bsz, l, dc), BF16),
        scratch_shapes=[pltpu.VMEM((tm + 2 * HALO, dc), F32), pltpu.VMEM((tm, dc), F32)],
        compiler_params=_params(("arbitrary", "arbitrary")),
        name="conv_branch",
    )(u, u, u, w, b, lg, lb)


def _softmax_pv(parts):
    m = None
    for s, _ in parts:
        mi = jnp.max(s, axis=-1, keepdims=True)
        m = mi if m is None else jnp.maximum(m, mi)
    den = None
    acc = None
    for s, v in parts:
        p = jnp.exp(s - m)
        di = jnp.sum(p, axis=-1, keepdims=True)
        den = di if den is None else den + di
        pv = _dot(p.astype(BF16), v)
        acc = pv if acc is None else acc + pv
    return acc / den


def _attn_kernel(n_ctx, q_ref, k_ref, v_ref, strip_ref, o_ref):
    j = pl.program_id(1)
    tq = q_ref.shape[1]
    rows_per_step = tq // GRID_W
    n_rows = (k_ref.shape[1] - n_ctx) // GRID_W
    n_loc = NA_ROWS * GRID_W
    pairs = N_HEADS // 2

    def split_heads(q2):
        lane = lax.broadcasted_iota(jnp.int32, q2.shape, 1)
        zero = jnp.zeros_like(q2)
        return jnp.concatenate([jnp.where(lane < HEAD_DIM, q2, zero),
                                jnp.where(lane >= HEAD_DIM, q2, zero)], axis=0)

    def merge_heads(o, m):
        lane = lax.broadcasted_iota(jnp.int32, (m, LANES), 1)
        return jnp.where(lane < HEAD_DIM, o[0:m], o[m:2 * m])

    @pl.when(j == 0)
    def _():
        for hp in range(pairs):
            cs = slice(hp * LANES, (hp + 1) * LANES)
            qs = split_heads(q_ref[0, :, cs])
            s = _dot_nt(qs, k_ref[0, 0:n_ctx, cs])
            o = _softmax_pv([(s, v_ref[0, 0:n_ctx, cs])])
            o_ref[0, :, cs] = merge_heads(o, tq).astype(BF16)

    @pl.when(j > 0)
    def _():
        for rr in range(rows_per_step):
            r = (j - 1) * rows_per_step + rr
            rs = jnp.clip(r - NA_ROWS // 2, 0, n_rows - NA_ROWS)
            case = rs - r + (NA_ROWS - 1)
            kstart = pl.multiple_of(n_ctx + rs * GRID_W, GRID_W)
            rsl = slice(rr * GRID_W, (rr + 1) * GRID_W)
            for hp in range(pairs):
                cs = slice(hp * LANES, (hp + 1) * LANES)
                qs = split_heads(q_ref[0, rsl, cs])
                s_loc = _dot_nt(qs, k_ref[0, pl.ds(kstart, n_loc), cs])
                bias = jnp.concatenate([strip_ref[2 * hp, case], strip_ref[2 * hp + 1, case]], axis=0)
                s_ctx = _dot_nt(qs, k_ref[0, 0:n_ctx, cs])
                o = _softmax_pv([(s_loc + bias, v_ref[0, pl.ds(kstart, n_loc), cs]),
                                 (s_ctx, v_ref[0, 0:n_ctx, cs])])
                o_ref[0, rsl, cs] = merge_heads(o, GRID_W).astype(BF16)


def _attention(q, k, v, strips, n_ctx):
    bsz, l, dn = q.shape
    tq = TOKEN_TILE
    whole = lambda: pl.BlockSpec((1, l, dn), lambda b, j: (b, 0, 0), pipeline_mode=pl.Buffered(1))
    return pl.pallas_call(
        functools.partial(_attn_kernel, n_ctx),
        grid=(bsz, l // tq),
        in_specs=[pl.BlockSpec((1, tq, dn), lambda b, j: (b, j, 0)), whole(), whole(),
                  pl.BlockSpec(strips.shape, lambda b, j: (0, 0, 0, 0), pipeline_mode=pl.Buffered(1))],
        out_specs=pl.BlockSpec((1, tq, dn), lambda b, j: (b, j, 0)),
        out_shape=jax.ShapeDtypeStruct((bsz, l, dn), BF16),
        compiler_params=_params(("arbitrary", "arbitrary")),
        name="attention",
    )(q, k, v, strips)


def _bias_strips(rpb):
    col = jnp.arange(GRID_W)
    start = jnp.clip(col - NA_COLS // 2, 0, GRID_W - NA_COLS)
    mask = (col[None, :] >= start[:, None]) & (col[None, :] < start[:, None] + NA_COLS)
    dc_idx = jnp.clip(col[None, :] - col[:, None] + NA_COLS - 1, 0, 2 * NA_COLS - 2)
    bias = jnp.where(mask[None, None], rpb.astype(F32)[:, :, dc_idx], NEG_INF)
    cases = [jnp.concatenate([bias[:, c + j] for j in range(NA_ROWS)], axis=-1) for c in range(NA_ROWS)]
    return jnp.stack(cases, axis=1)


def _rglru_kernel(reverse, has_addend, *refs):
    if has_addend:
        x_ref, add_ref, cw_ref, cb_ref, wg_ref, bg_ref, nsp_ref, o_ref, ext_ref, h_ref = refs
    else:
        x_ref, cw_ref, cb_ref, wg_ref, bg_ref, nsp_ref, o_ref, ext_ref, h_ref = refs
        add_ref = None
    j = pl.program_id(1)
    tc = x_ref.shape[1]
    dr = x_ref.shape[2]
    sl = SUBLANES
    near = 0 if not reverse else sl + tc

    @pl.when(j == 0)
    def _():
        h_ref[...] = jnp.zeros_like(h_ref)

    @pl.when(j <= 1)
    def _():
        ext_ref[near:near + sl, :] = jnp.zeros((sl, dr), F32)

    x = x_ref[0].astype(F32)
    ext_ref[sl:sl + tc, :] = x
    u = jnp.zeros((tc, dr), F32) + cb_ref[...]
    for kk in range(RNN_CONV_K):
        shift = kk - (RNN_CONV_K - 1) if not reverse else (RNN_CONV_K - 1) - kk
        u = u + ext_ref[sl + shift:sl + shift + tc, :] * cw_ref[kk:kk + 1, :]
    ext_ref[near:near + sl, :] = x[tc - sl:tc] if not reverse else x[0:sl]

    z = _dot(u.astype(BF16), wg_ref[...]) + bg_ref[...]
    r = _sigmoid(z[:, 0:dr])
    ig = _sigmoid(z[:, dr:2 * dr])
    a = jnp.exp(r * nsp_ref[...])
    bx = jnp.sqrt(1.0 - a * a) * (ig * u)

    sub = lax.broadcasted_iota(jnp.int32, (sl, dr), 0)
    h = h_ref[...]
    groups = list(range(tc // sl))
    if reverse:
        groups = groups[::-1]
    outs = {}
    for g in groups:
        ag = a[g * sl:(g + 1) * sl]
        bg = bx[g * sl:(g + 1) * sl]
        for dist in (1, 2, 4):
            amt = dist if not reverse else sl - dist
            keep = (sub >= dist) if not reverse else (sub < sl - dist)
            ra = pltpu.roll(ag, amt, 0)
            rb = pltpu.roll(bg, amt, 0)
            bg = bg + ag * jnp.where(keep, rb, 0.0)
            ag = ag * jnp.where(keep, ra, 1.0)
        og = ag * h + bg
        outs[g] = og
        h = og[sl - 1:sl] if not reverse else og[0:1]
    h_ref[...] = h
    out = jnp.concatenate([outs[g] for g in range(tc // sl)], axis=0)
    if add_ref is not None:
        out = out + add_ref[0].astype(F32)
    o_ref[0] = out.astype(o_ref.dtype)


def _rglru(rx, addend, cw, cb, wg, bg, nsp, reverse):
    bsz, l, dr = rx.shape
    tc = TOKEN_TILE
    n = l // tc
    if reverse:
        chunk = lambda b, j: (b, jnp.where(j == 0, 0, n - j), 0)
    else:
        chunk = lambda b, j: (b, j, 0)
    const = lambda shape: pl.BlockSpec(shape, lambda b, j: (0,) * len(shape))
    tile = pl.BlockSpec((1, tc, dr), chunk)
    ins = [rx] + ([addend] if addend is not None else []) + [cw, cb, wg, bg, nsp]
    specs = [tile] + ([tile] if addend is not None else []) + [
        const(cw.shape), const((1, dr)), const(wg.shape), const((1, 2 * dr)), const((1, dr))]
    return pl.pallas_call(
        functools.partial(_rglru_kernel, reverse, addend is not None),
        grid=(bsz, n),
        in_specs=specs,
        out_specs=tile,
        out_shape=jax.ShapeDtypeStruct((bsz, l, dr), BF16),
        scratch_shapes=[pltpu.VMEM((tc + 2 * SUBLANES, dr), F32), pltpu.VMEM((1, dr), F32)],
        compiler_params=_params(("arbitrary", "arbitrary")),
        name="rglru_bwd" if reverse else "rglru_fwd",
    )(*ins)


def _merge_kernel(x_ref, g1_ref, cv_ref, at_ref, hs_ref, rg_ref, gt_ref,
                  pw_ref, no_ref, ro_ref, ow_ref, o_ref):
    d = x_ref.shape[2]
    conv = _dot(cv_ref[0], pw_ref[...])
    na = _dot(at_ref[0], no_ref[...])
    rnn = _dot(rg_ref[0] * hs_ref[0], ro_ref[...])
    m = (gt_ref[0, :, 0:d].astype(F32) * conv + gt_ref[0, :, d:2 * d].astype(F32) * na
         + gt_ref[0, :, 2 * d:3 * d].astype(F32) * rnn)
    y = _dot(m.astype(BF16), ow_ref[...])
    o_ref[0] = x_ref[0] + g1_ref[0] * y


def _merge(xa, modtab, cv, at, hs, rg, gt, pw, no, ro, ow):
    bsz, l, d = xa.shape
    tm = TOKEN_TILE
    tile = lambda n: pl.BlockSpec((1, tm, n), lambda b, i: (b, i, 0))
    const = lambda a: pl.BlockSpec(a.shape, lambda b, i: (0,) * a.ndim)
    return pl.pallas_call(
        _merge_kernel,
        grid=(bsz, l // tm),
        in_specs=[tile(d), _mod_spec(d, 2), tile(cv.shape[2]), tile(at.shape[2]), tile(hs.shape[2]),
                  tile(rg.shape[2]), tile(3 * d), const(pw), const(no), const(ro), const(ow)],
        out_specs=tile(d),
        out_shape=jax.ShapeDtypeStruct((bsz, l, d), F32),
        compiler_params=_params(("arbitrary", "arbitrary")),
        name="merge",
    )(xa, modtab, cv, at, hs, rg, gt, pw, no, ro, ow)


def _router_kernel(x_ref, sh_ref, sc_ref, g_ref, rw_ref, rb_ref, tri_ref, h_ref, info_ref, cnt_ref):
    x = x_ref[0]
    h = x * lax.rsqrt(jnp.mean(x * x, axis=-1, keepdims=True) + EPS) * g_ref[...]
    h = h * (1.0 + sc_ref[0]) + sh_ref[0]
    hb = h.astype(BF16)
    h_ref[0] = hb
    logits = lax.dot_general(rw_ref[...], h, (((1,), (1,)), ((), ())),
                             preferred_element_type=F32, precision=lax.Precision.HIGHEST)
    scores = _sigmoid(logits)
    sel = scores + rb_ref[...]
    row = lambda t, e: t[e:e + 1, :]
    epg = EXPERTS_PER_GROUP
    gscore = []
    for g in range(N_GROUPS):
        best = None
        for a in range(epg):
            for b in range(a + 1, epg):
                pair = row(sel, g * epg + a) + row(sel, g * epg + b)
                best = pair if best is None else jnp.maximum(best, pair)
        gscore.append(best)
    gbest = gscore[0]
    gidx = jnp.zeros_like(gbest, dtype=jnp.int32)
    for g in range(1, N_GROUPS):
        better = gscore[g] > gbest
        gidx = jnp.where(better, g, gidx)
        gbest = jnp.where(better, gscore[g], gbest)
    picked = []
    flags = []
    for e in range(N_EXPERTS):
        g = e // epg
        rank = jnp.zeros_like(gidx)
        for o in range(g * epg, (g + 1) * epg):
            if o == e:
                continue
            ahead = (row(sel, o) > row(sel, e)) if o > e else (row(sel, o) >= row(sel, e))
            rank = rank + ahead.astype(jnp.int32)
        chosen = jnp.logical_and(gidx == g, rank < 2)
        picked.append(jnp.where(chosen, row(scores, e), 0.0))
        flags.append(jnp.where(chosen, 1.0, 0.0))
    den = picked[0]
    for e in range(1, N_EXPERTS):
        den = den + picked[e]
    gate = jnp.concatenate(picked, axis=0) / den
    flag = jnp.concatenate(flags, axis=0)

    cnt = jnp.sum(flag, axis=1, keepdims=True).astype(jnp.int32)
    pc = jnp.left_shift(jnp.right_shift(cnt + (ROW_ALIGN - 1), ROW_SHIFT), ROW_SHIFT)
    starts = [jnp.zeros((1, 1), jnp.int32)]
    for e in range(1, N_EXPERTS):
        starts.append(starts[-1] + pc[e - 1:e, :])
    seg0 = jnp.concatenate(starts, axis=0).astype(F32)
    pos = seg0 + _dot(flag.astype(BF16), tri_ref[...])
    used = flag > 0.5
    pos_lo = jnp.min(jnp.where(used, pos, 1e9), axis=0, keepdims=True)
    pos_hi = jnp.max(jnp.where(used, pos, -1.0), axis=0, keepdims=True)
    w_lo = jnp.sum(jnp.where(pos == pos_lo, gate, 0.0), axis=0, keepdims=True)
    w_hi = jnp.sum(jnp.where(pos == pos_hi, gate, 0.0), axis=0, keepdims=True)
    info_ref[...] = jnp.concatenate([pos_lo, pos_hi, w_lo, w_hi, jnp.zeros((4, pos.shape[1]), F32)], axis=0)
    cnt_ref[0] = jnp.broadcast_to(cnt, (N_EXPERTS, LANES))


def _router(xa, modtab, g, rw_t, rb, tri):
    bsz, l, d = xa.shape
    tm = TOKEN_TILE
    nt = l // tm
    const = lambda shape: pl.BlockSpec(shape, lambda b, i: (0,) * len(shape))
    return pl.pallas_call(
        _router_kernel,
        grid=(bsz, nt),
        in_specs=[pl.BlockSpec((1, tm, d), lambda b, i: (b, i, 0)), _mod_spec(d, 3), _mod_spec(d, 4),
                  const((1, d)), const(rw_t.shape), const(rb.shape), const(tri.shape)],
        out_specs=[pl.BlockSpec((1, tm, d), lambda b, i: (b, i, 0)),
                   pl.BlockSpec((SUBLANES, tm), lambda b, i: (0, b * nt + i)),
                   pl.BlockSpec((1, N_EXPERTS, LANES), lambda b, i: (b * nt + i, 0, 0))],
        out_shape=[jax.ShapeDtypeStruct((bsz, l, d), BF16),
                   jax.ShapeDtypeStruct((SUBLANES, bsz * l), F32),
                   jax.ShapeDtypeStruct((bsz * nt, N_EXPERTS, LANES), jnp.int32)],
        compiler_params=_params(("arbitrary", "arbitrary")),
        name="router",
    )(xa, modtab, modtab, g, rw_t, rb, tri)


def _route_plan(cnt, n_tiles):
    pc = (cnt + (ROW_ALIGN - 1)) // ROW_ALIGN * ROW_ALIGN
    ls = jnp.cumsum(pc, axis=1) - pc
    seg_tot = jnp.sum(pc, axis=0)
    reg = (seg_tot + (EXPERT_TILE - 1)) // EXPERT_TILE * EXPERT_TILE
    reg_end = jnp.cumsum(reg)
    base = reg_end - reg
    gs = base[None, :] + jnp.cumsum(pc, axis=0) - pc
    n_used = reg_end[-1] // EXPERT_TILE
    tiles = jnp.minimum(jnp.arange(n_tiles, dtype=jnp.int32), n_used - 1)
    tile_expert = jnp.minimum(jnp.searchsorted(reg_end // EXPERT_TILE, tiles, side="right"),
                              N_EXPERTS - 1).astype(jnp.int32)
    i32 = lambda a: a.astype(jnp.int32).reshape(-1)
    return (i32(pc), i32(ls), i32(gs), i32(base + seg_tot), i32(reg - seg_tot),
            tile_expert, i32(n_used))


def _for_each_piece(n, sizes, fn):
    off = jnp.int32(0)
    for size in sizes:
        hit = (n & size) != 0

        @pl.when(hit)
        def _(off=off, size=size):
            fn(pl.multiple_of(off, ROW_ALIGN), size)

        off = off + jnp.where(hit, size, 0)


def _window_copies(pc_ref, ls_ref, gs_ref, win, staged_at, sorted_at, to_sorted, sem, fn):
    for e in range(N_EXPERTS):
        n = pc_ref[win * N_EXPERTS + e]
        src0 = ls_ref[win * N_EXPERTS + e]
        dst0 = gs_ref[win * N_EXPERTS + e]

        def piece(off, size, src0=src0, dst0=dst0):
            a = staged_at(pl.multiple_of(src0 + off, ROW_ALIGN), size)
            b = sorted_at(pl.multiple_of(dst0 + off, ROW_ALIGN), size)
            fn(pltpu.make_async_copy(a, b, sem) if to_sorted else pltpu.make_async_copy(b, a, sem))

        _for_each_piece(n, SEG_SIZES, piece)


def _one_hot_rows(info_ref, rows, lo_val, hi_val):
    tm = info_ref.shape[1]
    rid = lax.broadcasted_iota(jnp.int32, (rows, tm), 0).astype(F32)
    return jnp.where(rid == info_ref[0:1, :], lo_val, jnp.where(rid == info_ref[1:2, :], hi_val, 0.0))


def _dispatch_kernel(pc_ref, ls_ref, gs_ref, ts_ref, tl_ref, nu_ref, h_ref, info_ref, xs_ref,
                     buf_ref, zero_ref, sem_ref, zsem_ref):
    w = pl.program_id(0)
    nw = pl.num_programs(0)
    slot = w % 2
    rows = buf_ref.shape[1]
    p = _one_hot_rows(info_ref, rows, 1.0, 1.0).astype(BF16)
    buf_ref[slot] = _dot(p, h_ref[...]).astype(BF16)

    def copies(win, sl, fn):
        _window_copies(pc_ref, ls_ref, gs_ref, win,
                       lambda r, n: buf_ref.at[sl, pl.ds(r, n)], lambda r, n: xs_ref.at[pl.ds(r, n)],
                       True, sem_ref.at[sl], fn)

    @pl.when(w > 0)
    def _():
        copies(w - 1, 1 - slot, lambda c: c.wait())

    copies(w, slot, lambda c: c.start())

    @pl.when(w == nw - 1)
    def _():
        copies(w, slot, lambda c: c.wait())
        zero_ref[...] = jnp.zeros_like(zero_ref)

        def tails(fn):
            for e in range(N_EXPERTS):
                start = ts_ref[e]

                def piece(off, size, start=start):
                    fn(pltpu.make_async_copy(zero_ref.at[pl.ds(0, size)],
                                             xs_ref.at[pl.ds(pl.multiple_of(start + off, ROW_ALIGN), size)],
                                             zsem_ref.at[0]))

                _for_each_piece(tl_ref[e], TAIL_SIZES, piece)

        tails(lambda c: c.start())
        tails(lambda c: c.wait())

        fill = zero_ref.shape[0]
        used = nu_ref[0] * EXPERT_TILE

        def spare(k):
            return pltpu.make_async_copy(
                zero_ref, xs_ref.at[pl.ds(pl.multiple_of(used + k * fill, ROW_ALIGN), fill)], zsem_ref.at[0])

        n_fill = (xs_ref.shape[0] - used) // fill
        lax.fori_loop(0, n_fill, lambda k, c: (spare(k).start(), c)[1], 0)
        lax.fori_loop(0, n_fill, lambda k, c: (spare(k).wait(), c)[1], 0)


def _dispatch(h2, info, plan, n_rows):
    t, d = h2.shape
    tm = TOKEN_TILE
    pc, ls, gs, ts, tl, _, nu = plan
    rows = STAGE_ROWS
    grid_spec = pltpu.PrefetchScalarGridSpec(
        num_scalar_prefetch=6,
        grid=(t // tm,),
        in_specs=[pl.BlockSpec((tm, d), lambda w, *_: (w, 0)),
                  pl.BlockSpec((SUBLANES, tm), lambda w, *_: (0, w))],
        out_specs=pl.BlockSpec(memory_space=pl.ANY),
        scratch_shapes=[pltpu.VMEM((2, rows, d), BF16), pltpu.VMEM((TAIL_SIZES[0], d), BF16),
                        pltpu.SemaphoreType.DMA((2,)), pltpu.SemaphoreType.DMA((1,))])
    return pl.pallas_call(
        _dispatch_kernel,
        grid_spec=grid_spec,
        out_shape=jax.ShapeDtypeStruct((n_rows, d), BF16),
        compiler_params=_params(("arbitrary",)),
        name="moe_dispatch",
    )(pc, ls, gs, ts, tl, nu, h2, info)


def _experts_kernel(te_ref, nu_ref, xs_ref, w1_ref, w3_ref, w2_ref, ys_ref, w13_ref, w2b_ref):
    i = pl.program_id(0)
    de = w1_ref.shape[2]
    prev = te_ref[jnp.maximum(i - 1, 0)]

    @pl.when(jnp.logical_or(i == 0, te_ref[i] != prev))
    def _():
        w13_ref[:, 0:de] = w1_ref[0].astype(BF16)
        w13_ref[:, de:2 * de] = w3_ref[0].astype(BF16)
        w2b_ref[...] = w2_ref[0].astype(BF16)

    @pl.when(i >= nu_ref[0])
    def _():
        ys_ref[...] = jnp.zeros_like(ys_ref)

    @pl.when(i < nu_ref[0])
    def _():
        z = _dot(xs_ref[...], w13_ref[...])
        he = _silu(z[:, 0:de]) * z[:, de:2 * de]
        ys_ref[...] = _dot(he.astype(BF16), w2b_ref[...]).astype(BF16)


def _experts(xs, plan, w1, w3, w2):
    n_rows, d = xs.shape
    de = w1.shape[2]
    te, nu = plan[5], plan[6]
    tm = EXPERT_TILE
    row_tile = pl.BlockSpec((tm, d), lambda i, te, nu: (jnp.minimum(i, nu[0] - 1), 0))
    grid_spec = pltpu.PrefetchScalarGridSpec(
        num_scalar_prefetch=2,
        grid=(n_rows // tm,),
        in_specs=[row_tile,
                  pl.BlockSpec((1, d, de), lambda i, te, nu: (te[i], 0, 0)),
                  pl.BlockSpec((1, d, de), lambda i, te, nu: (te[i], 0, 0)),
                  pl.BlockSpec((1, de, d), lambda i, te, nu: (te[i], 0, 0))],
        out_specs=pl.BlockSpec((tm, d), lambda i, te, nu: (i, 0)),
        scratch_shapes=[pltpu.VMEM((d, 2 * de), BF16), pltpu.VMEM((de, d), BF16)])
    return pl.pallas_call(
        _experts_kernel,
        grid_spec=grid_spec,
        out_shape=jax.ShapeDtypeStruct((n_rows, d), BF16),
        compiler_params=_params(("arbitrary",)),
        name="moe_experts",
    )(te, nu, xs, w1, w3, w2)


def _combine_kernel(pc_ref, ls_ref, gs_ref, x_ref, g2_ref, info_ref, ys_ref, o_ref, buf_ref, sem_ref):
    nt = pl.num_programs(1)
    w = pl.program_id(0) * nt + pl.program_id(1)
    nw = pl.num_programs(0) * nt
    slot = w % 2
    rows = buf_ref.shape[1]

    def copies(win, sl, fn):
        _window_copies(pc_ref, ls_ref, gs_ref, win,
                       lambda r, n: buf_ref.at[sl, pl.ds(r, n)], lambda r, n: ys_ref.at[pl.ds(r, n)],
                       False, sem_ref.at[sl], fn)

    @pl.when(w == 0)
    def _():
        buf_ref[...] = jnp.zeros_like(buf_ref)
        copies(w, slot, lambda c: c.start())

    @pl.when(w + 1 < nw)
    def _():
        copies(w + 1, 1 - slot, lambda c: c.start())

    copies(w, slot, lambda c: c.wait())
    pw = _one_hot_rows(info_ref, rows, info_ref[2:3, :], info_ref[3:4, :]).astype(BF16)
    moe = lax.dot_general(pw, buf_ref[slot], (((0,), (0,)), ((), ())), preferred_element_type=F32)
    o_ref[0] = x_ref[0] + g2_ref[0] * moe


def _combine(xa, modtab, info, ys, plan, drop):
    bsz, l, d = xa.shape
    tm = TOKEN_TILE
    nt = l // tm
    pc, ls, gs = plan[:3]
    grid_spec = pltpu.PrefetchScalarGridSpec(
        num_scalar_prefetch=3,
        grid=(bsz, nt),
        in_specs=[pl.BlockSpec((1, tm, d), lambda b, i, *_: (b, i, 0)),
                  pl.BlockSpec((1, 1, d), lambda b, i, *_: ((2 * b + jnp.minimum(i, 1)) * N_MOD + 5, 0, 0)),
                  pl.BlockSpec((SUBLANES, tm), lambda b, i, *_: (0, b * nt + i)),
                  pl.BlockSpec(memory_space=pl.ANY)],
        out_specs=pl.BlockSpec((1, tm, d), lambda b, i, *_: (b, jnp.maximum(i - drop, 0), 0)),
        scratch_shapes=[pltpu.VMEM((2, STAGE_ROWS, d), BF16), pltpu.SemaphoreType.DMA((2,))])
    return pl.pallas_call(
        _combine_kernel,
        grid_spec=grid_spec,
        out_shape=jax.ShapeDtypeStruct((bsz, l - drop * tm, d), F32),
        compiler_params=_params(("arbitrary", "arbitrary")),
        name="moe_combine",
    )(pc, ls, gs, xa, modtab, info, ys)


def _moe_sparse(xa, modtab, h2, info, cnt, w1, w3, w2, drop):
    bsz, l, d = xa.shape
    t = bsz * l
    n_win = t // TOKEN_TILE
    bound = TOP_K * t + n_win * N_EXPERTS * (ROW_ALIGN - 1) + N_EXPERTS * (EXPERT_TILE - ROW_ALIGN)
    n_tiles = -(-bound // EXPERT_TILE)
    plan = _route_plan(cnt[:, :, 0], n_tiles)
    xs = _dispatch(h2.reshape(t, d), info, plan, n_tiles * EXPERT_TILE)
    ys = _experts(xs, plan, w1, w3, w2)
    return _combine(xa, modtab, info, ys, plan, drop)


def _block_diag(w):
    n, k, _ = w.shape
    eye = jnp.eye(n, dtype=w.dtype)
    return jnp.einsum("nij,nm->nimj", w, eye).reshape(n * k, n * k)


def kernel(x, c, ctx, c_ctx, router_w, router_bias, mod_w, mod_b, norm1_g, norm2_g, in_w, in_b,
           conv_dw_w, conv_dw_b, conv_ln_g, conv_ln_b, conv_pw_w, na_q_g, na_k_g, na_rpb, na_out_w,
           rnn_conv_w, rnn_conv_b, rnn_wa, rnn_ba, rnn_wx, rnn_bx, rnn_lam, rnn_out_w, out_w,
           exp_w1, exp_w3, exp_w2):
    bsz, s, d = x.shape
    n_ctx = ctx.shape[1]
    depth = mod_w.shape[0]
    dc = conv_pw_w.shape[1]
    dn = na_out_w.shape[1]
    dr = rnn_out_w.shape[1]
    sizes = (dc, dn, dr, d)
    assert n_ctx % TOKEN_TILE == 0 and s % TOKEN_TILE == 0 and s % GRID_W == 0
    assert dn == N_HEADS * HEAD_DIM and s // GRID_W >= NA_ROWS

    xa = jnp.concatenate([ctx, x], axis=1)

    c_rows = -(-(bsz + 1) // SUBLANES) * SUBLANES
    c_all = jnp.zeros((c_rows, d), F32).at[:bsz].set(c).at[bsz].set(c_ctx)
    mods = _mod_vectors(c_all, mod_w, mod_b)

    head_mean = _block_diag(jnp.full((N_HEADS, HEAD_DIM, HEAD_DIM), 1.0 / HEAD_DIM, F32)).astype(BF16)
    rw_t = router_w.T
    rb = router_bias.reshape(N_EXPERTS, 1)
    tri = jnp.triu(jnp.ones((TOKEN_TILE, TOKEN_TILE), BF16), k=1)

    for l in range(depth):
        ml = mods[l].reshape(c_rows, N_MOD, d)
        modtab = jnp.stack([jnp.broadcast_to(ml[bsz], (bsz, N_MOD, d)), ml[:bsz]], axis=1)
        modtab = modtab.reshape(bsz * 2 * N_MOD, 1, d)

        qg = (jnp.tile(na_q_g[l], N_HEADS) * HEAD_DIM ** -0.5).reshape(1, dn)
        kg = jnp.tile(na_k_g[l], N_HEADS).reshape(1, dn)
        u, q, k, v, rx, rg, gt = _inproj(xa, modtab, norm1_g[l].reshape(1, d), in_w[l].astype(BF16),
                                         in_b[l].reshape(1, -1), qg, kg, head_mean, sizes)

        cw = jnp.zeros((CONV_K + 1, dc), F32).at[:CONV_K].set(conv_dw_w[l])
        cv = _conv_branch(u, cw, conv_dw_b[l].reshape(1, dc), conv_ln_g[l].reshape(1, dc),
                          conv_ln_b[l].reshape(1, dc))

        at = _attention(q, k, v, _bias_strips(na_rpb[l]), n_ctx)

        hs = None
        for dirn in range(2):
            wg = jnp.concatenate([_block_diag(rnn_wa[l, dirn]), _block_diag(rnn_wx[l, dirn])], axis=1)
            bg = jnp.concatenate([rnn_ba[l, dirn], rnn_bx[l, dirn]]).reshape(1, 2 * dr)
            nsp = (-LRU_C * jax.nn.softplus(-rnn_lam[l, dirn])).reshape(1, dr)
            rcw = jnp.zeros((SUBLANES, dr), F32).at[:RNN_CONV_K].set(rnn_conv_w[l, dirn])
            hs = _rglru(rx, hs, rcw, rnn_conv_b[l, dirn].reshape(1, dr), wg.astype(BF16), bg, nsp,
                        reverse=(dirn == 1))

        xa = _merge(xa, modtab, cv, at, hs, rg, gt, conv_pw_w[l].astype(BF16), na_out_w[l].astype(BF16),
                    rnn_out_w[l].astype(BF16), out_w[l].astype(BF16))

        h2, info, cnt = _router(xa, modtab, norm2_g[l].reshape(1, d), rw_t, rb, tri)
        drop = n_ctx // TOKEN_TILE if l == depth - 1 else 0
        xa = _moe_sparse(xa, modtab, h2, info, cnt, exp_w1[l], exp_w3[l], exp_w2[l], drop)

    return xa
```

```python
import functools
import math

import jax
import jax.numpy as jnp
import numpy as np
from jax import lax
from jax.experimental import pallas as pl
from jax.experimental.pallas import tpu as pltpu

F32 = jnp.float32
BF16 = jnp.bfloat16

GRID_W = 64
CONV_K = 31
N_HEADS = 8
HEAD_DIM = 64
NA_ROWS = 8
NA_COLS = 16
RNN_BLOCKS = 8
RNN_CONV_K = 4
LRU_C = 8.0
N_EXPERTS = 16
N_GROUPS = 4
EXPERTS_PER_GROUP = N_EXPERTS // N_GROUPS
N_MOD = 6
EPS = 1e-6
NEG_INF = -1e30

SUBLANES = 8
LANES = 128
MXU_DIM = 256
VMEM_LIMIT = 56 * 1024 * 1024

TOP_K = 2
TOKEN_TILE = 256
HALO = 16
ROW_ALIGN = 16
ROW_SHIFT = 4
EXPERT_TILE = 512
EXPERT_SUB = 256
SEG_SIZES = tuple(TOKEN_TILE >> k for k in range(TOKEN_TILE.bit_length() - ROW_SHIFT))
TAIL_SIZES = tuple((EXPERT_TILE // 2) >> k for k in range(EXPERT_TILE.bit_length() - 1 - ROW_SHIFT))
STAGE_ROWS = -(-(TOP_K * TOKEN_TILE + N_EXPERTS * (ROW_ALIGN - 1)) // MXU_DIM) * MXU_DIM


def _params(sem, vmem=VMEM_LIMIT):
    return pltpu.CompilerParams(dimension_semantics=sem, vmem_limit_bytes=vmem)


def _sigmoid(x):
    return 0.5 * (1.0 + jnp.tanh(0.5 * x))


def _silu(x):
    return x * _sigmoid(x)


def _gelu_tanh(x):
    return x * (0.5 * (1.0 + jnp.tanh(math.sqrt(2.0 / math.pi) * (x + 0.044715 * (x * x * x)))))


def _dot(a, b):
    return jnp.dot(a, b, preferred_element_type=F32)


def _dot_nt(a, b):
    return lax.dot_general(a, b, (((1,), (1,)), ((), ())), preferred_element_type=F32)


def _const_spec(shape):
    return pl.BlockSpec(shape, lambda *_: (0,) * len(shape))


def _layer_spec(arr, layer):
    return pl.BlockSpec((1,) + arr.shape[1:], lambda *_: (layer,) + (0,) * (arr.ndim - 1))


def _mod_kernel(c_ref, w_ref, b_ref, o_ref):
    a = _silu(c_ref[...])
    o_ref[0] = jnp.dot(a, w_ref[0], preferred_element_type=F32,
                       precision=lax.Precision.HIGHEST) + b_ref[0]


def _mod_vectors(c_all, mod_w, mod_b):
    depth, d, n = mod_w.shape
    tn = 1536
    rows = c_all.shape[0]
    return pl.pallas_call(
        _mod_kernel,
        grid=(depth, n // tn),
        in_specs=[pl.BlockSpec((rows, d), lambda l, j: (0, 0)),
                  pl.BlockSpec((1, d, tn), lambda l, j: (l, 0, j)),
                  pl.BlockSpec((1, 1, tn), lambda l, j: (l, 0, j))],
        out_specs=pl.BlockSpec((1, rows, tn), lambda l, j: (l, 0, j)),
        out_shape=jax.ShapeDtypeStruct((depth, rows, n), F32),
        compiler_params=_params(("arbitrary", "arbitrary")),
        name="mod_vectors",
    )(c_all, mod_w, mod_b.reshape(depth, 1, n))


def _mod_spec(d, which):
    return pl.BlockSpec((1, 1, d),
                        lambda b, i, *_: ((2 * b + jnp.minimum(i, 1)) * N_MOD + which, 0, 0))


def _inproj_kernel(sizes, x_ref, sh_ref, sc_ref, g_ref, w_ref, b_ref, qg_ref, kg_ref, bd_ref,
                   u_ref, q_ref, k_ref, v_ref, rx_ref, rg_ref, gt_ref):
    dc, dn, dr, d = sizes
    x = x_ref[0]
    h = x * lax.rsqrt(jnp.mean(x * x, axis=-1, keepdims=True) + EPS) * g_ref[...]
    h = h * (1.0 + sc_ref[0]) + sh_ref[0]
    hb = h.astype(BF16)

    def proj(off, n):
        return _dot(hb, w_ref[0, :, off:off + n]) + b_ref[:, off:off + n]

    def head_norm(t, gain_ref):
        ms = _dot((t * t).astype(BF16), bd_ref[...])
        return t * lax.rsqrt(ms + EPS) * gain_ref[...]

    off = 0
    a = proj(off, dc); off += dc
    ga = proj(off, dc); off += dc
    u_ref[0] = (a * _sigmoid(ga)).astype(BF16)
    q_ref[0] = head_norm(proj(off, dn), qg_ref).astype(BF16); off += dn
    k_ref[0] = head_norm(proj(off, dn), kg_ref).astype(BF16); off += dn
    v_ref[0] = proj(off, dn).astype(BF16); off += dn
    rx_ref[0] = proj(off, dr).astype(BF16); off += dr
    rg_ref[0] = _gelu_tanh(proj(off, dr)).astype(BF16); off += dr
    for j in range(3):
        gt_ref[0, :, j * d:(j + 1) * d] = _sigmoid(proj(off, d)).astype(BF16); off += d


def _inproj(xa, modtab, g, w, layer, b, qg, kg, bd, sizes):
    bsz, l, d = xa.shape
    dc, dn, dr, _ = sizes
    d_in = w.shape[2]
    tm = TOKEN_TILE
    tile = lambda n: pl.BlockSpec((1, tm, n), lambda b, i: (b, i, 0))
    out = lambda n: jax.ShapeDtypeStruct((bsz, l, n), BF16)
    return pl.pallas_call(
        functools.partial(_inproj_kernel, sizes),
        grid=(bsz, l // tm),
        in_specs=[tile(d), _mod_spec(d, 0), _mod_spec(d, 1), _const_spec((1, d)),
                  pl.BlockSpec((1, d, d_in), lambda b, i: (layer, 0, 0), pipeline_mode=pl.Buffered(1)),
                  _const_spec((1, d_in)), _const_spec((1, dn)), _const_spec((1, dn)), _const_spec((dn, dn))],
        out_specs=[tile(dc), tile(dn), tile(dn), tile(dn), tile(dr), tile(dr), tile(3 * d)],
        out_shape=[out(dc), out(dn), out(dn), out(dn), out(dr), out(dr), out(3 * d)],
        compiler_params=_params(("arbitrary", "arbitrary")),
        name="inproj",
    )(xa, modtab, modtab, g, w, b, qg, kg, bd)


CONV_SPLIT = 128


def _conv_shift_matrices(rows):
    ka = CONV_SPLIT + HALO
    nb = rows - CONV_SPLIT
    sa = np.zeros(((SUBLANES - 1) * CONV_SPLIT, ka), np.float32)
    sb = np.zeros(((SUBLANES - 1) * nb, nb), np.float32)
    for s in range(1, SUBLANES):
        for r in range(CONV_SPLIT):
            sa[(s - 1) * CONV_SPLIT + r, r + s] = 1.0
        for r in range(nb - s):
            sb[(s - 1) * nb + r, r + s] = 1.0
    return jnp.asarray(sa, BF16), jnp.asarray(sb, BF16)


def _conv_kernel(u_ref, prev_ref, next_ref, sa_ref, sb_ref, w_ref, b_ref, lg_ref, lb_ref, o_ref,
                 sh_ref, y_ref):
    i = pl.program_id(1)
    n = pl.num_programs(1)
    tm = u_ref.shape[1]
    dc = u_ref.shape[2]
    rows = tm + 2 * HALO
    ka = sa_ref.shape[1]
    nb = rows - CONV_SPLIT
    prev_ok = i >= 2
    next_ok = jnp.logical_and(i >= 1, i < n - 1)
    zero = jnp.zeros((HALO, dc), BF16)
    win = jnp.concatenate([jnp.where(prev_ok, prev_ref[0], zero), u_ref[0],
                           jnp.where(next_ok, next_ref[0], zero)], axis=0)
    sh_ref[0] = win.astype(F32)
    rc = 64
    base = HALO - CONV_K // 2
    for c0 in range(0, dc, MXU_DIM):
        hs = slice(c0, c0 + MXU_DIM)
        sha = _dot(sa_ref[...], win[0:ka, hs])
        shb = _dot(sb_ref[...], win[CONV_SPLIT:rows, hs])
        for s in range(1, SUBLANES):
            sh_ref[s, 0:CONV_SPLIT, hs] = sha[(s - 1) * CONV_SPLIT:s * CONV_SPLIT]
            sh_ref[s, CONV_SPLIT:rows, hs] = shb[(s - 1) * nb:s * nb]
        for cb in range(c0 // LANES, (c0 + MXU_DIM) // LANES):
            cs = slice(cb * LANES, (cb + 1) * LANES)
            for r0 in range(0, tm, rc):
                acc = jnp.zeros((rc, LANES), F32)
                for k in range(CONV_K):
                    o = base + k
                    s, a = o % SUBLANES, o // SUBLANES
                    acc = acc + sh_ref[s, r0 + a * SUBLANES:r0 + a * SUBLANES + rc, cs] * w_ref[k:k + 1, cs]
                y_ref[r0:r0 + rc, cs] = acc
    y = y_ref[...] + b_ref[...]
    mu = jnp.mean(y, axis=-1, keepdims=True)
    yc = y - mu
    var = jnp.mean(yc * yc, axis=-1, keepdims=True)
    z = yc * lax.rsqrt(var + EPS) * lg_ref[...] + lb_ref[...]
    o_ref[0] = _silu(z).astype(BF16)


def _conv_branch(u, sa, sb, w, b, lg, lb):
    bsz, l, dc = u.shape
    tm = TOKEN_TILE
    hb = tm // HALO
    nh = l // HALO
    return pl.pallas_call(
        _conv_kernel,
        grid=(bsz, l // tm),
        in_specs=[pl.BlockSpec((1, tm, dc), lambda b, i: (b, i, 0)),
                  pl.BlockSpec((1, HALO, dc), lambda b, i: (b, jnp.maximum(i * hb - 1, 0), 0)),
                  pl.BlockSpec((1, HALO, dc), lambda b, i: (b, jnp.minimum((i + 1) * hb, nh - 1), 0)),
                  _const_spec(sa.shape), _const_spec(sb.shape),
                  _const_spec(w.shape), _const_spec((1, dc)), _const_spec((1, dc)), _const_spec((1, dc))],
        out_specs=pl.BlockSpec((1, tm, dc), lambda b, i: (b, i, 0)),
        out_shape=jax.ShapeDtypeStruct((bsz, l, dc), BF16),
        scratch_shapes=[pltpu.VMEM((SUBLANES, tm + 2 * HALO, dc), F32), pltpu.VMEM((tm, dc), F32)],
        compiler_params=_params(("arbitrary", "arbitrary")),
        name="conv_branch",
    )(u, u, u, sa, sb, w, b, lg, lb)


def _softmax_pv(parts):
    m = None
    for s, _ in parts:
        mi = jnp.max(s, axis=-1, keepdims=True)
        m = mi if m is None else jnp.maximum(m, mi)
    den = None
    acc = None
    for s, v in parts:
        p = jnp.exp(s - m)
        di = jnp.sum(p, axis=-1, keepdims=True)
        den = di if den is None else den + di
        pv = _dot(p.astype(BF16), v)
        acc = pv if acc is None else acc + pv
    return acc / den


def _attn_kernel(n_ctx, q_ref, k_ref, v_ref, strip_ref, o_ref):
    j = pl.program_id(1)
    tq = q_ref.shape[1]
    rows_per_step = tq // GRID_W
    n_rows = (k_ref.shape[1] - n_ctx) // GRID_W
    n_loc = NA_ROWS * GRID_W
    pairs = N_HEADS // 2

    def split_heads(q2):
        lane = lax.broadcasted_iota(jnp.int32, q2.shape, 1)
        zero = jnp.zeros_like(q2)
        return jnp.concatenate([jnp.where(lane < HEAD_DIM, q2, zero),
                                jnp.where(lane >= HEAD_DIM, q2, zero)], axis=0)

    def merge_heads(o, m):
        lane = lax.broadcasted_iota(jnp.int32, (m, LANES), 1)
        return jnp.where(lane < HEAD_DIM, o[0:m], o[m:2 * m])

    @pl.when(j == 0)
    def _():
        for hp in range(pairs):
            cs = slice(hp * LANES, (hp + 1) * LANES)
            qs = split_heads(q_ref[0, :, cs])
            s = _dot_nt(qs, k_ref[0, 0:n_ctx, cs])
            o = _softmax_pv([(s, v_ref[0, 0:n_ctx, cs])])
            o_ref[0, :, cs] = merge_heads(o, tq).astype(BF16)

    @pl.when(j > 0)
    def _():
        for rr in range(rows_per_step):
            r = (j - 1) * rows_per_step + rr
            rs = jnp.clip(r - NA_ROWS // 2, 0, n_rows - NA_ROWS)
            case = rs - r + (NA_ROWS - 1)
            kstart = pl.multiple_of(n_ctx + rs * GRID_W, GRID_W)
            rsl = slice(rr * GRID_W, (rr + 1) * GRID_W)
            for hp in range(pairs):
                cs = slice(hp * LANES, (hp + 1) * LANES)
                qs = split_heads(q_ref[0, rsl, cs])
                s_loc = _dot_nt(qs, k_ref[0, pl.ds(kstart, n_loc), cs])
                bias = jnp.concatenate([strip_ref[2 * hp, case], strip_ref[2 * hp + 1, case]], axis=0)
                s_ctx = _dot_nt(qs, k_ref[0, 0:n_ctx, cs])
                o = _softmax_pv([(s_loc + bias, v_ref[0, pl.ds(kstart, n_loc), cs]),
                                 (s_ctx, v_ref[0, 0:n_ctx, cs])])
                o_ref[0, rsl, cs] = merge_heads(o, GRID_W).astype(BF16)


def _attention(q, k, v, strips, n_ctx):
    bsz, l, dn = q.shape
    tq = TOKEN_TILE
    whole = lambda: pl.BlockSpec((1, l, dn), lambda b, j: (b, 0, 0), pipeline_mode=pl.Buffered(1))
    return pl.pallas_call(
        functools.partial(_attn_kernel, n_ctx),
        grid=(bsz, l // tq),
        in_specs=[pl.BlockSpec((1, tq, dn), lambda b, j: (b, j, 0)), whole(), whole(),
                  pl.BlockSpec(strips.shape, lambda b, j: (0, 0, 0, 0), pipeline_mode=pl.Buffered(1))],
        out_specs=pl.BlockSpec((1, tq, dn), lambda b, j: (b, j, 0)),
        out_shape=jax.ShapeDtypeStruct((bsz, l, dn), BF16),
        compiler_params=_params(("arbitrary", "arbitrary")),
        name="attention",
    )(q, k, v, strips)


def _bias_strips(rpb):
    col = np.arange(GRID_W)
    start = np.clip(col - NA_COLS // 2, 0, GRID_W - NA_COLS)
    mask = (col[None, :] >= start[:, None]) & (col[None, :] < start[:, None] + NA_COLS)
    dc_idx = np.clip(col[None, :] - col[:, None] + NA_COLS - 1, 0, 2 * NA_COLS - 2)
    pick = jnp.asarray(dc_idx[None] == np.arange(2 * NA_COLS - 1)[:, None, None], F32)
    bias = jnp.einsum("hrc,cqk->hrqk", rpb.astype(F32), pick, precision=lax.Precision.HIGHEST)
    bias = jnp.where(jnp.asarray(mask)[None, None], bias, NEG_INF)
    cases = [jnp.concatenate([bias[:, c + j] for j in range(NA_ROWS)], axis=-1) for c in range(NA_ROWS)]
    return jnp.stack(cases, axis=1)


def _scan_permutation(tc):
    p = np.arange(tc)
    t = (p % SUBLANES) * (tc // SUBLANES) + p // SUBLANES
    m = np.zeros((tc, tc), np.float32)
    m[p, t] = 1.0
    return jnp.asarray(m, BF16), jnp.asarray(m.T, BF16)


def _rglru_kernel(xf_ref, xb_ref, p_ref, pt_ref, cw_ref, cb_ref, wg_ref, bg_ref, nsp_ref, of_ref, ob_ref,
                  edge_ref, h_ref):
    j = pl.program_id(1)

    @pl.when(j == 0)
    def _():
        h_ref[...] = jnp.zeros_like(h_ref)

    @pl.when(j <= 1)
    def _():
        edge_ref[...] = jnp.zeros_like(edge_ref)

    for dirn, (x_ref, o_ref) in enumerate(((xf_ref, of_ref), (xb_ref, ob_ref))):
        o_ref[0] = _rglru_chunk(dirn == 1, x_ref[0], p_ref, pt_ref, cw_ref.at[dirn], cb_ref.at[dirn],
                                wg_ref.at[dirn], bg_ref.at[dirn], nsp_ref.at[dirn],
                                edge_ref.at[dirn], h_ref.at[dirn])


def _rglru_chunk(reverse, x, p_ref, pt_ref, cw_ref, cb_ref, wg_ref, bg_ref, nsp_ref, edge_ref, h_ref):
    tc, dr = x.shape
    sl = SUBLANES
    nv = tc // sl
    reach = (RNN_CONV_K - 1) * sl
    xp = _dot(p_ref[...], x)
    sub = lax.broadcasted_iota(jnp.int32, (sl, dr), 0)
    prev = edge_ref[...]
    if not reverse:
        edge = xp[tc - reach:tc]
        fixed = jnp.concatenate(
            [jnp.where(sub == 0, pltpu.roll(prev[v * sl:(v + 1) * sl], 1, 0),
                       pltpu.roll(edge[v * sl:(v + 1) * sl], 1, 0)) for v in range(RNN_CONV_K - 1)], axis=0)
        shifted = [xp] + [jnp.concatenate([fixed[reach - d * sl:reach], xp[0:tc - d * sl]], axis=0)
                          for d in range(1, RNN_CONV_K)]
    else:
        edge = xp[0:reach]
        fixed = jnp.concatenate(
            [jnp.where(sub == sl - 1, pltpu.roll(prev[v * sl:(v + 1) * sl], sl - 1, 0),
                       pltpu.roll(edge[v * sl:(v + 1) * sl], sl - 1, 0)) for v in range(RNN_CONV_K - 1)], axis=0)
        shifted = [xp] + [jnp.concatenate([xp[d * sl:tc], fixed[0:d * sl]], axis=0)
                          for d in range(1, RNN_CONV_K)]
    edge_ref[...] = edge
    u = jnp.zeros((tc, dr), F32) + cb_ref[...]
    for kk in range(RNN_CONV_K):
        u = u + shifted[RNN_CONV_K - 1 - kk] * cw_ref[kk:kk + 1, :]

    z = _dot(u.astype(BF16), wg_ref[...]) + bg_ref[...]
    r = _sigmoid(z[:, 0:dr])
    ig = _sigmoid(z[:, dr:2 * dr])
    a = jnp.exp(r * nsp_ref[...])
    bx = jnp.sqrt(1.0 - a * a) * (ig * u)

    order = list(range(nv)) if not reverse else list(range(nv - 1, -1, -1))
    hs, prods = {}, {}
    h = None
    prod = None
    for v in order:
        av = a[v * sl:(v + 1) * sl]
        bv = bx[v * sl:(v + 1) * sl]
        h = bv if h is None else av * h + bv
        prod = av if prod is None else av * prod
        hs[v], prods[v] = h, prod
    h_end, p_end = hs[order[-1]], prods[order[-1]]
    c = h_ref[...]
    carry = {}
    for s in (range(sl) if not reverse else range(sl - 1, -1, -1)):
        carry[s] = c
        c = p_end[s:s + 1] * c + h_end[s:s + 1]
    h_ref[...] = c
    cin = jnp.concatenate([carry[s] for s in range(sl)], axis=0)
    out = jnp.concatenate([hs[v] + prods[v] * cin for v in range(nv)], axis=0)
    return _dot(pt_ref[...], out.astype(BF16)).astype(BF16)


def _rglru(rx, perm, perm_t, cw, cb, wg, bg, nsp):
    bsz, l, dr = rx.shape
    tc = TOKEN_TILE
    n = l // tc
    fwd = pl.BlockSpec((1, tc, dr), lambda b, j: (b, j, 0))
    bwd = pl.BlockSpec((1, tc, dr), lambda b, j: (b, jnp.where(j == 0, 0, n - j), 0))
    out = jax.ShapeDtypeStruct((bsz, l, dr), BF16)
    return pl.pallas_call(
        _rglru_kernel,
        grid=(bsz, n),
        in_specs=[fwd, bwd, _const_spec(perm.shape), _const_spec(perm_t.shape), _const_spec(cw.shape),
                  _const_spec(cb.shape), _const_spec(wg.shape), _const_spec(bg.shape), _const_spec(nsp.shape)],
        out_specs=[fwd, bwd],
        out_shape=[out, out],
        scratch_shapes=[pltpu.VMEM((2, (RNN_CONV_K - 1) * SUBLANES, dr), F32), pltpu.VMEM((2, 1, dr), F32)],
        compiler_params=_params(("arbitrary", "arbitrary")),
        name="rglru",
    )(rx, rx, perm, perm_t, cw, cb, wg, bg, nsp)


def _route(h, rw_ref, rb_ref, tri_ref):
    tm = h.shape[0]
    hi = h.astype(BF16)
    lo = (h - hi.astype(F32)).astype(BF16)
    d = h.shape[1]
    parts = _dot(hi, rw_ref[0:d, :]) + _dot(lo, rw_ref[d:2 * d, :])
    parts_t = parts.T
    logits = parts_t[0:N_EXPERTS] + parts_t[N_EXPERTS:2 * N_EXPERTS]
    scores = _sigmoid(logits)
    sel = scores + rb_ref[...]
    row = lambda t, e: t[e:e + 1, :]
    epg = EXPERTS_PER_GROUP
    gscore = []
    for g in range(N_GROUPS):
        best = None
        for a in range(epg):
            for b in range(a + 1, epg):
                pair = row(sel, g * epg + a) + row(sel, g * epg + b)
                best = pair if best is None else jnp.maximum(best, pair)
        gscore.append(best)
    gbest = gscore[0]
    gidx = jnp.zeros_like(gbest, dtype=jnp.int32)
    for g in range(1, N_GROUPS):
        better = gscore[g] > gbest
        gidx = jnp.where(better, g, gidx)
        gbest = jnp.where(better, gscore[g], gbest)
    picked = []
    flags = []
    for e in range(N_EXPERTS):
        g = e // epg
        rank = jnp.zeros_like(gidx)
        for o in range(g * epg, (g + 1) * epg):
            if o == e:
                continue
            ahead = (row(sel, o) > row(sel, e)) if o > e else (row(sel, o) >= row(sel, e))
            rank = rank + ahead.astype(jnp.int32)
        chosen = jnp.logical_and(gidx == g, rank < 2)
        picked.append(jnp.where(chosen, row(scores, e), 0.0))
        flags.append(jnp.where(chosen, 1.0, 0.0))
    den = picked[0]
    for e in range(1, N_EXPERTS):
        den = den + picked[e]
    gate = jnp.concatenate(picked, axis=0) / den
    flag = jnp.concatenate(flags, axis=0)

    cnt = jnp.sum(flag, axis=1, keepdims=True).astype(jnp.int32)
    pc = jnp.left_shift(jnp.right_shift(cnt + (ROW_ALIGN - 1), ROW_SHIFT), ROW_SHIFT)
    starts = [jnp.zeros((1, 1), jnp.int32)]
    for e in range(1, N_EXPERTS):
        starts.append(starts[-1] + pc[e - 1:e, :])
    seg0 = jnp.concatenate(starts, axis=0).astype(F32)
    pos = seg0 + _dot(flag.astype(BF16), tri_ref[...])
    used = flag > 0.5
    pos_lo = jnp.min(jnp.where(used, pos, 1e9), axis=0, keepdims=True)
    pos_hi = jnp.max(jnp.where(used, pos, -1.0), axis=0, keepdims=True)
    w_lo = jnp.sum(jnp.where(pos == pos_lo, gate, 0.0), axis=0, keepdims=True)
    w_hi = jnp.sum(jnp.where(pos == pos_hi, gate, 0.0), axis=0, keepdims=True)
    info = jnp.concatenate([pos_lo, pos_hi, w_lo, w_hi, jnp.zeros((4, tm), F32)], axis=0)
    return hi, info, jnp.broadcast_to(cnt, (N_EXPERTS, LANES))


def _merge_kernel(x_ref, g1_ref, sh2_ref, sc2_ref, n2_ref, cv_ref, at_ref, hf_ref, hb_ref, rg_ref, gt_ref,
                  pw_ref, no_ref, ro_ref, ow_ref, rw_ref, rb_ref, tri_ref,
                  o_ref, h2_ref, info_ref, cnt_ref):
    d = x_ref.shape[2]
    conv = _dot(cv_ref[0], pw_ref[0])
    na = _dot(at_ref[0], no_ref[0])
    hsum = hf_ref[0].astype(F32) + hb_ref[0].astype(F32)
    rnn = _dot((rg_ref[0].astype(F32) * hsum).astype(BF16), ro_ref[0])
    m = (gt_ref[0, :, 0:d].astype(F32) * conv + gt_ref[0, :, d:2 * d].astype(F32) * na
         + gt_ref[0, :, 2 * d:3 * d].astype(F32) * rnn)
    y = _dot(m.astype(BF16), ow_ref[0])
    x = x_ref[0] + g1_ref[0] * y
    o_ref[0] = x
    h = x * lax.rsqrt(jnp.mean(x * x, axis=-1, keepdims=True) + EPS) * n2_ref[...]
    h = h * (1.0 + sc2_ref[0]) + sh2_ref[0]
    h2, info, cnt = _route(h, rw_ref, rb_ref, tri_ref)
    h2_ref[0] = h2
    info_ref[...] = info
    cnt_ref[0] = cnt


def _merge(xa, modtab, n2, cv, at, hf, hb, rg, gt, pw, no, ro, ow, layer, rw, rb, tri):
    bsz, l, d = xa.shape
    tm = TOKEN_TILE
    nt = l // tm
    tile = lambda n: pl.BlockSpec((1, tm, n), lambda b, i: (b, i, 0))
    return pl.pallas_call(
        _merge_kernel,
        grid=(bsz, nt),
        in_specs=[tile(d), _mod_spec(d, 2), _mod_spec(d, 3), _mod_spec(d, 4), _const_spec((1, d)),
                  tile(cv.shape[2]), tile(at.shape[2]), tile(hf.shape[2]), tile(hb.shape[2]),
                  tile(rg.shape[2]), tile(3 * d),
                  _layer_spec(pw, layer), _layer_spec(no, layer), _layer_spec(ro, layer), _layer_spec(ow, layer),
                  _const_spec(rw.shape), _const_spec(rb.shape), _const_spec(tri.shape)],
        out_specs=[tile(d), tile(d),
                   pl.BlockSpec((SUBLANES, tm), lambda b, i: (0, b * nt + i)),
                   pl.BlockSpec((1, N_EXPERTS, LANES), lambda b, i: (b * nt + i, 0, 0))],
        out_shape=[jax.ShapeDtypeStruct((bsz, l, d), F32),
                   jax.ShapeDtypeStruct((bsz, l, d), BF16),
                   jax.ShapeDtypeStruct((SUBLANES, bsz * l), F32),
                   jax.ShapeDtypeStruct((bsz * nt, N_EXPERTS, LANES), jnp.int32)],
        compiler_params=_params(("arbitrary", "arbitrary")),
        name="merge_route",
    )(xa, modtab, modtab, modtab, n2, cv, at, hf, hb, rg, gt, pw, no, ro, ow, rw, rb, tri)


def _route_plan(cnt, n_tiles):
    pc = (cnt + (ROW_ALIGN - 1)) // ROW_ALIGN * ROW_ALIGN
    ls = jnp.cumsum(pc, axis=1) - pc
    seg_tot = jnp.sum(pc, axis=0)
    reg = (seg_tot + (EXPERT_TILE - 1)) // EXPERT_TILE * EXPERT_TILE
    reg_end = jnp.cumsum(reg)
    base = reg_end - reg
    gs = base[None, :] + jnp.cumsum(pc, axis=0) - pc
    n_used = reg_end[-1] // EXPERT_TILE
    tiles = jnp.minimum(jnp.arange(n_tiles, dtype=jnp.int32), n_used - 1)
    tile_expert = jnp.minimum(jnp.sum(tiles[:, None] >= (reg_end // EXPERT_TILE)[None, :], axis=1),
                              N_EXPERTS - 1).astype(jnp.int32)
    i32 = lambda a: a.astype(jnp.int32).reshape(-1)
    return (i32(pc), i32(ls), i32(gs), i32(base + seg_tot), i32(reg - seg_tot),
            tile_expert, i32(n_used))


def _for_each_piece(n, sizes, fn):
    off = jnp.int32(0)
    for size in sizes:
        hit = (n & size) != 0

        @pl.when(hit)
        def _(off=off, size=size):
            fn(pl.multiple_of(off, ROW_ALIGN), size)

        off = off + jnp.where(hit, size, 0)


def _window_copies(pc_ref, ls_ref, gs_ref, win, staged_at, sorted_at, to_sorted, sem, fn):
    for e in range(N_EXPERTS):
        n = pc_ref[win * N_EXPERTS + e]
        src0 = ls_ref[win * N_EXPERTS + e]
        dst0 = gs_ref[win * N_EXPERTS + e]

        def piece(off, size, src0=src0, dst0=dst0):
            a = staged_at(pl.multiple_of(src0 + off, ROW_ALIGN), size)
            b = sorted_at(pl.multiple_of(dst0 + off, ROW_ALIGN), size)
            fn(pltpu.make_async_copy(a, b, sem) if to_sorted else pltpu.make_async_copy(b, a, sem))

        _for_each_piece(n, SEG_SIZES, piece)


def _one_hot_rows(info_ref, rows, lo_val, hi_val):
    tm = info_ref.shape[1]
    rid = lax.broadcasted_iota(jnp.int32, (rows, tm), 0).astype(F32)
    return jnp.where(rid == info_ref[0:1, :], lo_val, jnp.where(rid == info_ref[1:2, :], hi_val, 0.0))


def _dispatch_kernel(pc_ref, ls_ref, gs_ref, ts_ref, tl_ref, nu_ref, h_ref, info_ref, xs_ref,
                     buf_ref, zero_ref, sem_ref, zsem_ref):
    w = pl.program_id(0)
    nw = pl.num_programs(0)
    slot = w % 2
    rows = buf_ref.shape[1]
    p = _one_hot_rows(info_ref, rows, 1.0, 1.0).astype(BF16)
    buf_ref[slot] = _dot(p, h_ref[...]).astype(BF16)

    def copies(win, sl, fn):
        _window_copies(pc_ref, ls_ref, gs_ref, win,
                       lambda r, n: buf_ref.at[sl, pl.ds(r, n)], lambda r, n: xs_ref.at[pl.ds(r, n)],
                       True, sem_ref.at[sl], fn)

    @pl.when(w > 0)
    def _():
        copies(w - 1, 1 - slot, lambda c: c.wait())

    copies(w, slot, lambda c: c.start())

    @pl.when(w == nw - 1)
    def _():
        copies(w, slot, lambda c: c.wait())
        zero_ref[...] = jnp.zeros_like(zero_ref)

        def tails(fn):
            for e in range(N_EXPERTS):
                start = ts_ref[e]

                def piece(off, size, start=start):
                    fn(pltpu.make_async_copy(zero_ref.at[pl.ds(0, size)],
                                             xs_ref.at[pl.ds(pl.multiple_of(start + off, ROW_ALIGN), size)],
                                             zsem_ref.at[0]))

                _for_each_piece(tl_ref[e], TAIL_SIZES, piece)

        tails(lambda c: c.start())
        tails(lambda c: c.wait())

        fill = zero_ref.shape[0]
        used = nu_ref[0] * EXPERT_TILE

        def spare(k):
            return pltpu.make_async_copy(
                zero_ref, xs_ref.at[pl.ds(pl.multiple_of(used + k * fill, ROW_ALIGN), fill)], zsem_ref.at[0])

        n_fill = (xs_ref.shape[0] - used) // fill
        lax.fori_loop(0, n_fill, lambda k, c: (spare(k).start(), c)[1], 0)
        lax.fori_loop(0, n_fill, lambda k, c: (spare(k).wait(), c)[1], 0)


def _dispatch(h2, info, plan, n_rows):
    t, d = h2.shape
    tm = TOKEN_TILE
    pc, ls, gs, ts, tl, _, nu = plan
    rows = STAGE_ROWS
    grid_spec = pltpu.PrefetchScalarGridSpec(
        num_scalar_prefetch=6,
        grid=(t // tm,),
        in_specs=[pl.BlockSpec((tm, d), lambda w, *_: (w, 0)),
                  pl.BlockSpec((SUBLANES, tm), lambda w, *_: (0, w))],
        out_specs=pl.BlockSpec(memory_space=pl.ANY),
        scratch_shapes=[pltpu.VMEM((2, rows, d), BF16), pltpu.VMEM((TAIL_SIZES[0], d), BF16),
                        pltpu.SemaphoreType.DMA((2,)), pltpu.SemaphoreType.DMA((1,))])
    return pl.pallas_call(
        _dispatch_kernel,
        grid_spec=grid_spec,
        out_shape=jax.ShapeDtypeStruct((n_rows, d), BF16),
        compiler_params=_params(("arbitrary",)),
        name="moe_dispatch",
    )(pc, ls, gs, ts, tl, nu, h2, info)


def _experts_kernel(te_ref, nu_ref, xs_ref, w1_ref, w3_ref, w2_ref, ys_ref, w13_ref, w2b_ref):
    i = pl.program_id(0)
    de = w1_ref.shape[3]
    prev = te_ref[jnp.maximum(i - 1, 0)]

    @pl.when(jnp.logical_or(i == 0, te_ref[i] != prev))
    def _():
        w13_ref[:, 0:de] = w1_ref[0, 0].astype(BF16)
        w13_ref[:, de:2 * de] = w3_ref[0, 0].astype(BF16)
        w2b_ref[...] = w2_ref[0, 0].astype(BF16)

    @pl.when(i >= nu_ref[0])
    def _():
        ys_ref[...] = jnp.zeros_like(ys_ref)

    @pl.when(i < nu_ref[0])
    def _():
        for r0 in range(0, xs_ref.shape[0], EXPERT_SUB):
            z = _dot(xs_ref[r0:r0 + EXPERT_SUB, :], w13_ref[...])
            he = _silu(z[:, 0:de]) * z[:, de:2 * de]
            ys_ref[r0:r0 + EXPERT_SUB, :] = _dot(he.astype(BF16), w2b_ref[...]).astype(BF16)


def _experts(xs, plan, w1, w3, w2, layer):
    n_rows, d = xs.shape
    de = w1.shape[3]
    te, nu = plan[5], plan[6]
    tm = EXPERT_TILE
    grid_spec = pltpu.PrefetchScalarGridSpec(
        num_scalar_prefetch=2,
        grid=(n_rows // tm,),
        in_specs=[pl.BlockSpec((tm, d), lambda i, te, nu: (jnp.minimum(i, nu[0] - 1), 0)),
                  pl.BlockSpec((1, 1, d, de), lambda i, te, nu: (layer, te[i], 0, 0)),
                  pl.BlockSpec((1, 1, d, de), lambda i, te, nu: (layer, te[i], 0, 0)),
                  pl.BlockSpec((1, 1, de, d), lambda i, te, nu: (layer, te[i], 0, 0))],
        out_specs=pl.BlockSpec((tm, d), lambda i, te, nu: (i, 0)),
        scratch_shapes=[pltpu.VMEM((d, 2 * de), BF16), pltpu.VMEM((de, d), BF16)])
    return pl.pallas_call(
        _experts_kernel,
        grid_spec=grid_spec,
        out_shape=jax.ShapeDtypeStruct((n_rows, d), BF16),
        compiler_params=_params(("arbitrary",)),
        name="moe_experts",
    )(te, nu, xs, w1, w3, w2)


def _combine_kernel(pc_ref, ls_ref, gs_ref, x_ref, g2_ref, info_ref, ys_ref, o_ref, buf_ref, sem_ref):
    nt = pl.num_programs(1)
    w = pl.program_id(0) * nt + pl.program_id(1)
    nw = pl.num_programs(0) * nt
    slot = w % 2
    rows = buf_ref.shape[1]

    def copies(win, sl, fn):
        _window_copies(pc_ref, ls_ref, gs_ref, win,
                       lambda r, n: buf_ref.at[sl, pl.ds(r, n)], lambda r, n: ys_ref.at[pl.ds(r, n)],
                       False, sem_ref.at[sl], fn)

    @pl.when(w == 0)
    def _():
        buf_ref[...] = jnp.zeros_like(buf_ref)
        copies(w, slot, lambda c: c.start())

    @pl.when(w + 1 < nw)
    def _():
        copies(w + 1, 1 - slot, lambda c: c.start())

    copies(w, slot, lambda c: c.wait())
    pw = _one_hot_rows(info_ref, rows, info_ref[2:3, :], info_ref[3:4, :]).astype(BF16)
    moe = lax.dot_general(pw, buf_ref[slot], (((0,), (0,)), ((), ())), preferred_element_type=F32)
    o_ref[0] = x_ref[0] + g2_ref[0] * moe


def _combine(xa, modtab, info, ys, plan, drop):
    bsz, l, d = xa.shape
    tm = TOKEN_TILE
    nt = l // tm
    pc, ls, gs = plan[:3]
    grid_spec = pltpu.PrefetchScalarGridSpec(
        num_scalar_prefetch=3,
        grid=(bsz, nt),
        in_specs=[pl.BlockSpec((1, tm, d), lambda b, i, *_: (b, i, 0)),
                  _mod_spec(d, 5),
                  pl.BlockSpec((SUBLANES, tm), lambda b, i, *_: (0, b * nt + i)),
                  pl.BlockSpec(memory_space=pl.ANY)],
        out_specs=pl.BlockSpec((1, tm, d), lambda b, i, *_: (b, jnp.maximum(i - drop, 0), 0)),
        scratch_shapes=[pltpu.VMEM((2, STAGE_ROWS, d), BF16), pltpu.SemaphoreType.DMA((2,))])
    return pl.pallas_call(
        _combine_kernel,
        grid_spec=grid_spec,
        out_shape=jax.ShapeDtypeStruct((bsz, l - drop * tm, d), F32),
        compiler_params=_params(("arbitrary", "arbitrary")),
        name="moe_combine",
    )(pc, ls, gs, xa, modtab, info, ys)


def _moe_sparse(xa, modtab, h2, info, cnt, w1, w3, w2, layer, drop):
    bsz, l, d = xa.shape
    t = bsz * l
    n_win = t // TOKEN_TILE
    bound = TOP_K * t + n_win * N_EXPERTS * (ROW_ALIGN - 1) + N_EXPERTS * (EXPERT_TILE - ROW_ALIGN)
    n_tiles = -(-bound // EXPERT_TILE)
    plan = _route_plan(cnt[:, :, 0], n_tiles)
    xs = _dispatch(h2.reshape(t, d), info, plan, n_tiles * EXPERT_TILE)
    ys = _experts(xs, plan, w1, w3, w2, layer)
    return _combine(xa, modtab, info, ys, plan, drop)


def _block_diag(w):
    n, k, _ = w.shape
    eye = jnp.eye(n, dtype=w.dtype)
    return jnp.einsum("nij,nm->nimj", w, eye).reshape(n * k, n * k)


def kernel(x, c, ctx, c_ctx, router_w, router_bias, mod_w, mod_b, norm1_g, norm2_g, in_w, in_b,
           conv_dw_w, conv_dw_b, conv_ln_g, conv_ln_b, conv_pw_w, na_q_g, na_k_g, na_rpb, na_out_w,
           rnn_conv_w, rnn_conv_b, rnn_wa, rnn_ba, rnn_wx, rnn_bx, rnn_lam, rnn_out_w, out_w,
           exp_w1, exp_w3, exp_w2):
    bsz, s, d = x.shape
    n_ctx = ctx.shape[1]
    depth = mod_w.shape[0]
    dc = conv_pw_w.shape[1]
    dn = na_out_w.shape[1]
    dr = rnn_out_w.shape[1]
    sizes = (dc, dn, dr, d)
    assert n_ctx % TOKEN_TILE == 0 and s % TOKEN_TILE == 0 and s % GRID_W == 0
    assert dn == N_HEADS * HEAD_DIM and s // GRID_W >= NA_ROWS

    xa = jnp.concatenate([ctx, x], axis=1)

    c_rows = -(-(bsz + 1) // SUBLANES) * SUBLANES
    c_all = jnp.zeros((c_rows, d), F32).at[:bsz].set(c).at[bsz].set(c_ctx)
    mods = _mod_vectors(c_all, mod_w, mod_b)

    head_mean = _block_diag(jnp.full((N_HEADS, HEAD_DIM, HEAD_DIM), 1.0 / HEAD_DIM, F32)).astype(BF16)
    rw_hi = router_w.astype(BF16)
    rw_lo = (router_w - rw_hi.astype(F32)).astype(BF16)
    rw = jnp.zeros((2 * d, LANES), BF16)
    rw = rw.at[:d, 0:N_EXPERTS].set(rw_hi).at[:d, N_EXPERTS:2 * N_EXPERTS].set(rw_lo)
    rw = rw.at[d:, 0:N_EXPERTS].set(rw_hi)
    rb = router_bias.reshape(N_EXPERTS, 1)
    tri = jnp.asarray(np.triu(np.ones((TOKEN_TILE, TOKEN_TILE), np.float32), k=1), BF16)
    shift_a, shift_b = _conv_shift_matrices(TOKEN_TILE + 2 * HALO)
    perm, perm_t = _scan_permutation(TOKEN_TILE)

    in_w_b = in_w.astype(BF16)
    pw_b, no_b, ro_b, ow_b = (a.astype(BF16) for a in (conv_pw_w, na_out_w, rnn_out_w, out_w))

    for l in range(depth):
        ml = mods[l].reshape(c_rows, N_MOD, d)
        modtab = jnp.stack([jnp.broadcast_to(ml[bsz], (bsz, N_MOD, d)), ml[:bsz]], axis=1)
        modtab = modtab.reshape(bsz * 2 * N_MOD, 1, d)

        qg = (jnp.tile(na_q_g[l], N_HEADS) * HEAD_DIM ** -0.5).reshape(1, dn)
        kg = jnp.tile(na_k_g[l], N_HEADS).reshape(1, dn)
        u, q, k, v, rx, rg, gt = _inproj(xa, modtab, norm1_g[l].reshape(1, d), in_w_b, l,
                                         in_b[l].reshape(1, -1), qg, kg, head_mean, sizes)

        cw = jnp.zeros((CONV_K + 1, dc), F32).at[:CONV_K].set(conv_dw_w[l])
        cv = _conv_branch(u, shift_a, shift_b, cw, conv_dw_b[l].reshape(1, dc),
                          conv_ln_g[l].reshape(1, dc), conv_ln_b[l].reshape(1, dc))

        at = _attention(q, k, v, _bias_strips(na_rpb[l]), n_ctx)

        wg = jnp.stack([jnp.concatenate([_block_diag(rnn_wa[l, dn_]), _block_diag(rnn_wx[l, dn_])], axis=1)
                        for dn_ in range(2)]).astype(BF16)
        bg = jnp.concatenate([rnn_ba[l], rnn_bx[l]], axis=-1).reshape(2, 1, 2 * dr)
        nsp = (-LRU_C * jax.nn.softplus(-rnn_lam[l])).reshape(2, 1, dr)
        rcw = jnp.zeros((2, SUBLANES, dr), F32).at[:, :RNN_CONV_K].set(rnn_conv_w[l])
        hf, hb = _rglru(rx, perm, perm_t, rcw, rnn_conv_b[l].reshape(2, 1, dr), wg, bg, nsp)

        xa, h2, info, cnt = _merge(xa, modtab, norm2_g[l].reshape(1, d), cv, at, hf, hb, rg, gt,
                                   pw_b, no_b, ro_b, ow_b, l, rw, rb, tri)

        drop = n_ctx // TOKEN_TILE if l == depth - 1 else 0
        xa = _moe_sparse(xa, modtab, h2, info, cnt, exp_w1, exp_w3, exp_w2, l, drop)

    return xa
```

```python
import functools
import math

import jax
import jax.numpy as jnp
import numpy as np
from jax import lax
from jax.experimental import pallas as pl
from jax.experimental.pallas import tpu as pltpu

F32 = jnp.float32
BF16 = jnp.bfloat16

GRID_W = 64
CONV_K = 31
N_HEADS = 8
HEAD_DIM = 64
NA_ROWS = 8
NA_COLS = 16
RNN_BLOCKS = 8
RNN_CONV_K = 4
LRU_C = 8.0
N_EXPERTS = 16
N_GROUPS = 4
EXPERTS_PER_GROUP = N_EXPERTS // N_GROUPS
N_MOD = 6
EPS = 1e-6
NEG_INF = -1e30

SUBLANES = 8
LANES = 128
MXU_DIM = 256
VMEM_LIMIT = 56 * 1024 * 1024

TOP_K = 2
TOKEN_TILE = 256
HALO = 16
ROW_ALIGN = 16
ROW_SHIFT = 4
EXPERT_TILE = 512
EXPERT_SUB = 256
SEG_SIZES = tuple(TOKEN_TILE >> k for k in range(TOKEN_TILE.bit_length() - ROW_SHIFT))
TAIL_SIZES = tuple((EXPERT_TILE // 2) >> k for k in range(EXPERT_TILE.bit_length() - 1 - ROW_SHIFT))
STAGE_ROWS = -(-(TOP_K * TOKEN_TILE + N_EXPERTS * (ROW_ALIGN - 1)) // MXU_DIM) * MXU_DIM
WAIT_SIZES = tuple(1 << k for k in range(STAGE_ROWS.bit_length() - 1, ROW_SHIFT - 1, -1))


def _params(sem, vmem=VMEM_LIMIT):
    return pltpu.CompilerParams(dimension_semantics=sem, vmem_limit_bytes=vmem)


def _sigmoid(x):
    return 0.5 * (1.0 + jnp.tanh(0.5 * x))


def _silu(x):
    return x * _sigmoid(x)


def _gelu_tanh(x):
    return x * (0.5 * (1.0 + jnp.tanh(math.sqrt(2.0 / math.pi) * (x + 0.044715 * (x * x * x)))))


def _dot(a, b):
    return jnp.dot(a, b, preferred_element_type=F32)


def _dot_nt(a, b):
    return lax.dot_general(a, b, (((1,), (1,)), ((), ())), preferred_element_type=F32)


def _const_spec(shape):
    return pl.BlockSpec(shape, lambda *_: (0,) * len(shape))


def _layer_spec(arr, layer):
    return pl.BlockSpec((1,) + arr.shape[1:], lambda *_: (layer,) + (0,) * (arr.ndim - 1))


def _mod_kernel(c_ref, w_ref, b_ref, o_ref):
    a = _silu(c_ref[...])
    o_ref[0] = jnp.dot(a, w_ref[0], preferred_element_type=F32,
                       precision=lax.Precision.HIGHEST) + b_ref[0]


def _mod_vectors(c_all, mod_w, mod_b):
    depth, d, n = mod_w.shape
    tn = 1536
    rows = c_all.shape[0]
    return pl.pallas_call(
        _mod_kernel,
        grid=(depth, n // tn),
        in_specs=[pl.BlockSpec((rows, d), lambda l, j: (0, 0)),
                  pl.BlockSpec((1, d, tn), lambda l, j: (l, 0, j)),
                  pl.BlockSpec((1, 1, tn), lambda l, j: (l, 0, j))],
        out_specs=pl.BlockSpec((1, rows, tn), lambda l, j: (l, 0, j)),
        out_shape=jax.ShapeDtypeStruct((depth, rows, n), F32),
        compiler_params=_params(("arbitrary", "arbitrary")),
        name="mod_vectors",
    )(c_all, mod_w, mod_b.reshape(depth, 1, n))


def _mod_spec(d, which):
    return pl.BlockSpec((1, 1, d),
                        lambda b, i, *_: ((2 * b + jnp.minimum(i, 1)) * N_MOD + which, 0, 0))


def _stream_specs(src, tm):
    if not isinstance(src, tuple):
        return [pl.BlockSpec((1, tm, src.shape[2]), lambda b, i, *_: (b, i, 0))], [src], 0
    ctx, lat = src
    nc = ctx.shape[1] // tm
    d = ctx.shape[2]
    return ([pl.BlockSpec((1, tm, d), lambda b, i, *_: (b, jnp.minimum(i, nc - 1), 0)),
             pl.BlockSpec((1, tm, d), lambda b, i, *_: (b, jnp.maximum(i - nc, 0), 0))], [ctx, lat], nc)


def _stream_tile(x_refs, n_ctx_tiles):
    if len(x_refs) == 1:
        return x_refs[0][0]
    return jnp.where(pl.program_id(1) < n_ctx_tiles, x_refs[0][0], x_refs[1][0])


def _inproj_kernel(sizes, n_src, n_ctx_tiles, *refs):
    x_refs = refs[:n_src]
    (sh_ref, sc_ref, g_ref, w_ref, b_ref, qg_ref, kg_ref, bd_ref,
     u_ref, q_ref, k_ref, v_ref, rx_ref, rg_ref, gt_ref, z_ref) = refs[n_src:]
    dc, dn, dr, d = sizes
    x = _stream_tile(x_refs, n_ctx_tiles)
    h = x * lax.rsqrt(jnp.mean(x * x, axis=-1, keepdims=True) + EPS) * g_ref[...]
    h = h * (1.0 + sc_ref[0]) + sh_ref[0]
    hb = h.astype(BF16)

    d_in = w_ref.shape[2]
    for c0 in range(0, d_in, dn):
        z_ref[:, c0:c0 + dn] = _dot(hb, w_ref[0, :, c0:c0 + dn]) + b_ref[:, c0:c0 + dn]

    def proj(off, n):
        return z_ref[:, off:off + n]

    def head_norm(t, gain_ref):
        ms = _dot((t * t).astype(BF16), bd_ref[...])
        return t * lax.rsqrt(ms + EPS) * gain_ref[...]

    off = 0
    a = proj(off, dc); off += dc
    ga = proj(off, dc); off += dc
    u_ref[0] = (a * _sigmoid(ga)).astype(BF16)
    q_ref[0] = head_norm(proj(off, dn), qg_ref).astype(BF16); off += dn
    k_ref[0] = head_norm(proj(off, dn), kg_ref).astype(BF16); off += dn
    v_ref[0] = proj(off, dn).astype(BF16); off += dn
    rx_ref[0] = proj(off, dr).astype(BF16); off += dr
    rg_ref[0] = _gelu_tanh(proj(off, dr)).astype(BF16); off += dr
    for j in range(3):
        gt_ref[0, :, j * d:(j + 1) * d] = _sigmoid(proj(off, d)).astype(BF16); off += d


def _inproj(src, bsz, l, modtab, g, w, layer, b, qg, kg, bd, sizes):
    dc, dn, dr, d = sizes
    d_in = w.shape[2]
    tm = TOKEN_TILE
    x_specs, x_args, nc = _stream_specs(src, tm)
    tile = lambda n: pl.BlockSpec((1, tm, n), lambda b, i: (b, i, 0))
    out = lambda n: jax.ShapeDtypeStruct((bsz, l, n), BF16)
    return pl.pallas_call(
        functools.partial(_inproj_kernel, sizes, len(x_args), nc),
        grid=(bsz, l // tm),
        in_specs=x_specs + [_mod_spec(d, 0), _mod_spec(d, 1), _const_spec((1, d)),
                            pl.BlockSpec((1, d, d_in), lambda b, i: (layer, 0, 0), pipeline_mode=pl.Buffered(1)),
                            _const_spec((1, d_in)), _const_spec((1, dn)), _const_spec((1, dn)),
                            _const_spec((dn, dn))],
        out_specs=[tile(dc), tile(dn), tile(dn), tile(dn), tile(dr), tile(dr), tile(3 * d)],
        out_shape=[out(dc), out(dn), out(dn), out(dn), out(dr), out(dr), out(3 * d)],
        scratch_shapes=[pltpu.VMEM((tm, d_in), F32)],
        compiler_params=_params(("arbitrary", "arbitrary")),
        name="inproj",
    )(*x_args, modtab, modtab, g, w, b, qg, kg, bd)


CONV_SPLIT = 128


def _conv_shift_matrices(rows):
    ka = CONV_SPLIT + HALO
    nb = rows - CONV_SPLIT
    sa = np.zeros(((SUBLANES - 1) * CONV_SPLIT, ka), np.float32)
    sb = np.zeros(((SUBLANES - 1) * nb, nb), np.float32)
    for s in range(1, SUBLANES):
        for r in range(CONV_SPLIT):
            sa[(s - 1) * CONV_SPLIT + r, r + s] = 1.0
        for r in range(nb - s):
            sb[(s - 1) * nb + r, r + s] = 1.0
    return jnp.asarray(sa, BF16), jnp.asarray(sb, BF16)


def _conv_kernel(u_ref, prev_ref, next_ref, sa_ref, sb_ref, w_ref, b_ref, lg_ref, lb_ref, o_ref,
                 sh_ref, y_ref):
    i = pl.program_id(1)
    n = pl.num_programs(1)
    tm = u_ref.shape[1]
    dc = u_ref.shape[2]
    rows = tm + 2 * HALO
    ka = sa_ref.shape[1]
    nb = rows - CONV_SPLIT
    prev_ok = i >= 2
    next_ok = jnp.logical_and(i >= 1, i < n - 1)
    zero = jnp.zeros((HALO, dc), BF16)
    win = jnp.concatenate([jnp.where(prev_ok, prev_ref[0], zero), u_ref[0],
                           jnp.where(next_ok, next_ref[0], zero)], axis=0)
    sh_ref[0] = win.astype(F32)
    rc = 64
    base = HALO - CONV_K // 2
    for c0 in range(0, dc, MXU_DIM):
        hs = slice(c0, c0 + MXU_DIM)
        sha = _dot(sa_ref[...], win[0:ka, hs])
        shb = _dot(sb_ref[...], win[CONV_SPLIT:rows, hs])
        for s in range(1, SUBLANES):
            sh_ref[s, 0:CONV_SPLIT, hs] = sha[(s - 1) * CONV_SPLIT:s * CONV_SPLIT]
            sh_ref[s, CONV_SPLIT:rows, hs] = shb[(s - 1) * nb:s * nb]
        for cb in range(c0 // LANES, (c0 + MXU_DIM) // LANES):
            cs = slice(cb * LANES, (cb + 1) * LANES)
            for r0 in range(0, tm, rc):
                acc = jnp.zeros((rc, LANES), F32)
                for k in range(CONV_K):
                    o = base + k
                    s, a = o % SUBLANES, o // SUBLANES
                    acc = acc + sh_ref[s, r0 + a * SUBLANES:r0 + a * SUBLANES + rc, cs] * w_ref[k:k + 1, cs]
                y_ref[r0:r0 + rc, cs] = acc
    y = y_ref[...] + b_ref[...]
    mu = jnp.mean(y, axis=-1, keepdims=True)
    yc = y - mu
    var = jnp.mean(yc * yc, axis=-1, keepdims=True)
    z = yc * lax.rsqrt(var + EPS) * lg_ref[...] + lb_ref[...]
    o_ref[0] = _silu(z).astype(BF16)


def _conv_branch(u, sa, sb, w, b, lg, lb):
    bsz, l, dc = u.shape
    tm = TOKEN_TILE
    hb = tm // HALO
    nh = l // HALO
    return pl.pallas_call(
        _conv_kernel,
        grid=(bsz, l // tm),
        in_specs=[pl.BlockSpec((1, tm, dc), lambda b, i: (b, i, 0)),
                  pl.BlockSpec((1, HALO, dc), lambda b, i: (b, jnp.maximum(i * hb - 1, 0), 0)),
                  pl.BlockSpec((1, HALO, dc), lambda b, i: (b, jnp.minimum((i + 1) * hb, nh - 1), 0)),
                  _const_spec(sa.shape), _const_spec(sb.shape),
                  _const_spec(w.shape), _const_spec((1, dc)), _const_spec((1, dc)), _const_spec((1, dc))],
        out_specs=pl.BlockSpec((1, tm, dc), lambda b, i: (b, i, 0)),
        out_shape=jax.ShapeDtypeStruct((bsz, l, dc), BF16),
        scratch_shapes=[pltpu.VMEM((SUBLANES, tm + 2 * HALO, dc), F32), pltpu.VMEM((tm, dc), F32)],
        compiler_params=_params(("arbitrary", "arbitrary")),
        name="conv_branch",
    )(u, u, u, sa, sb, w, b, lg, lb)


def _softmax_pv(parts):
    m = None
    for s, _ in parts:
        mi = jnp.max(s, axis=-1, keepdims=True)
        m = mi if m is None else jnp.maximum(m, mi)
    colsum = None
    acc = None
    for s, v in parts:
        p = jnp.exp((s - m).astype(BF16))
        for c0 in range(0, p.shape[1], LANES):
            blk = p[:, c0:c0 + LANES]
            colsum = blk if colsum is None else colsum + blk
        pv = _dot(p, v)
        acc = pv if acc is None else acc + pv
    den = jnp.sum(colsum.astype(F32), axis=-1, keepdims=True)
    return acc / den


def _attn_kernel(n_ctx, q_ref, k_ref, v_ref, strip_ref, o_ref, s_ref):
    j = pl.program_id(1)
    tq = q_ref.shape[1]
    rows_per_step = tq // GRID_W
    n_rows = (k_ref.shape[1] - n_ctx) // GRID_W
    n_loc = NA_ROWS * GRID_W
    pairs = N_HEADS // 2

    def split_heads(q2):
        lane = lax.broadcasted_iota(jnp.int32, q2.shape, 1)
        zero = jnp.zeros_like(q2)
        return jnp.concatenate([jnp.where(lane < HEAD_DIM, q2, zero),
                                jnp.where(lane >= HEAD_DIM, q2, zero)], axis=0)

    def merge_heads(o, m):
        lane = lax.broadcasted_iota(jnp.int32, (m, LANES), 1)
        return jnp.where(lane < HEAD_DIM, o[0:m], o[m:2 * m])

    @pl.when(j == 0)
    def _():
        for hp in range(pairs):
            cs = slice(hp * LANES, (hp + 1) * LANES)
            qs = split_heads(q_ref[0, :, cs])
            s = _dot_nt(qs, k_ref[0, 0:n_ctx, cs])
            o = _softmax_pv([(s, v_ref[0, 0:n_ctx, cs])])
            o_ref[0, :, cs] = merge_heads(o, tq).astype(BF16)

    @pl.when(j > 0)
    def _():
        rows2 = 2 * GRID_W
        starts = []
        for rr in range(rows_per_step):
            r = (j - 1) * rows_per_step + rr
            rs = jnp.clip(r - NA_ROWS // 2, 0, n_rows - NA_ROWS)
            starts.append((rs - r + (NA_ROWS - 1), pl.multiple_of(n_ctx + rs * GRID_W, GRID_W)))
        for hp in range(pairs):
            cs = slice(hp * LANES, (hp + 1) * LANES)
            qs_all = jnp.concatenate([split_heads(q_ref[0, rr * GRID_W:(rr + 1) * GRID_W, cs])
                                      for rr in range(rows_per_step)], axis=0)
            s_ctx = _dot_nt(qs_all, k_ref[0, 0:n_ctx, cs])
            for rr in range(rows_per_step):
                case, kstart = starts[rr]
                slot = rr * pairs + hp
                bias = jnp.concatenate([strip_ref[2 * hp, case], strip_ref[2 * hp + 1, case]], axis=0)
                s_ref[slot, :, 0:n_loc] = _dot_nt(qs_all[rr * rows2:(rr + 1) * rows2],
                                                  k_ref[0, pl.ds(kstart, n_loc), cs]) + bias
                s_ref[slot, :, n_loc:n_loc + n_ctx] = s_ctx[rr * rows2:(rr + 1) * rows2]
        for rr in range(rows_per_step):
            _, kstart = starts[rr]
            rsl = slice(rr * GRID_W, (rr + 1) * GRID_W)
            for hp in range(pairs):
                cs = slice(hp * LANES, (hp + 1) * LANES)
                slot = rr * pairs + hp
                o = _softmax_pv([(s_ref[slot, :, 0:n_loc], v_ref[0, pl.ds(kstart, n_loc), cs]),
                                 (s_ref[slot, :, n_loc:n_loc + n_ctx], v_ref[0, 0:n_ctx, cs])])
                o_ref[0, rsl, cs] = merge_heads(o, GRID_W).astype(BF16)


def _attention(q, k, v, strips, n_ctx):
    bsz, l, dn = q.shape
    tq = TOKEN_TILE
    whole = lambda: pl.BlockSpec((1, l, dn), lambda b, j: (b, 0, 0), pipeline_mode=pl.Buffered(1))
    return pl.pallas_call(
        functools.partial(_attn_kernel, n_ctx),
        grid=(bsz, l // tq),
        in_specs=[pl.BlockSpec((1, tq, dn), lambda b, j: (b, j, 0)), whole(), whole(),
                  pl.BlockSpec(strips.shape, lambda b, j: (0, 0, 0, 0), pipeline_mode=pl.Buffered(1))],
        out_specs=pl.BlockSpec((1, tq, dn), lambda b, j: (b, j, 0)),
        out_shape=jax.ShapeDtypeStruct((bsz, l, dn), BF16),
        scratch_shapes=[pltpu.VMEM(((tq // GRID_W) * (N_HEADS // 2), 2 * GRID_W, NA_ROWS * GRID_W + n_ctx), F32)],
        compiler_params=_params(("arbitrary", "arbitrary")),
        name="attention",
    )(q, k, v, strips)


def _bias_strips(rpb):
    col = np.arange(GRID_W)
    start = np.clip(col - NA_COLS // 2, 0, GRID_W - NA_COLS)
    mask = (col[None, :] >= start[:, None]) & (col[None, :] < start[:, None] + NA_COLS)
    dc_idx = np.clip(col[None, :] - col[:, None] + NA_COLS - 1, 0, 2 * NA_COLS - 2)
    pick = jnp.asarray(dc_idx[None] == np.arange(2 * NA_COLS - 1)[:, None, None], F32)
    bias = jnp.einsum("hrc,cqk->hrqk", rpb.astype(F32), pick, precision=lax.Precision.HIGHEST)
    bias = jnp.where(jnp.asarray(mask)[None, None], bias, NEG_INF)
    cases = [jnp.concatenate([bias[:, c + j] for j in range(NA_ROWS)], axis=-1) for c in range(NA_ROWS)]
    return jnp.stack(cases, axis=1)


def _scan_permutation(tc):
    p = np.arange(tc)
    t = (p % SUBLANES) * (tc // SUBLANES) + p // SUBLANES
    m = np.zeros((tc, tc), np.float32)
    m[p, t] = 1.0
    return jnp.asarray(m, BF16), jnp.asarray(m.T, BF16)


def _rglru_kernel(xf_ref, xb_ref, p_ref, pt_ref, cw_ref, cb_ref, wg_ref, bg_ref, nsp_ref, of_ref, ob_ref,
                  edge_ref, h_ref):
    j = pl.program_id(1)

    @pl.when(j == 0)
    def _():
        h_ref[...] = jnp.zeros_like(h_ref)

    @pl.when(j <= 1)
    def _():
        edge_ref[...] = jnp.zeros_like(edge_ref)

    for dirn, (x_ref, o_ref) in enumerate(((xf_ref, of_ref), (xb_ref, ob_ref))):
        o_ref[0] = _rglru_chunk(dirn == 1, x_ref[0], p_ref, pt_ref, cw_ref.at[dirn], cb_ref.at[dirn],
                                wg_ref.at[dirn], bg_ref.at[dirn], nsp_ref.at[dirn],
                                edge_ref.at[dirn], h_ref.at[dirn])


def _rglru_chunk(reverse, x, p_ref, pt_ref, cw_ref, cb_ref, wg_ref, bg_ref, nsp_ref, edge_ref, h_ref):
    tc, dr = x.shape
    sl = SUBLANES
    nv = tc // sl
    reach = (RNN_CONV_K - 1) * sl
    xp = _dot(p_ref[...], x)
    sub = lax.broadcasted_iota(jnp.int32, (sl, dr), 0)
    prev = edge_ref[...]
    if not reverse:
        edge = xp[tc - reach:tc]
        fixed = jnp.concatenate(
            [jnp.where(sub == 0, pltpu.roll(prev[v * sl:(v + 1) * sl], 1, 0),
                       pltpu.roll(edge[v * sl:(v + 1) * sl], 1, 0)) for v in range(RNN_CONV_K - 1)], axis=0)
        shifted = [xp] + [jnp.concatenate([fixed[reach - d * sl:reach], xp[0:tc - d * sl]], axis=0)
                          for d in range(1, RNN_CONV_K)]
    else:
        edge = xp[0:reach]
        fixed = jnp.concatenate(
            [jnp.where(sub == sl - 1, pltpu.roll(prev[v * sl:(v + 1) * sl], sl - 1, 0),
                       pltpu.roll(edge[v * sl:(v + 1) * sl], sl - 1, 0)) for v in range(RNN_CONV_K - 1)], axis=0)
        shifted = [xp] + [jnp.concatenate([xp[d * sl:tc], fixed[0:d * sl]], axis=0)
                          for d in range(1, RNN_CONV_K)]
    edge_ref[...] = edge
    u = jnp.zeros((tc, dr), F32) + cb_ref[...]
    for kk in range(RNN_CONV_K):
        u = u + shifted[RNN_CONV_K - 1 - kk] * cw_ref[kk:kk + 1, :]

    z = _dot(u.astype(BF16), wg_ref[...]) + bg_ref[...]
    r = _sigmoid(z[:, 0:dr])
    ig = _sigmoid(z[:, dr:2 * dr])
    a = jnp.exp(r * nsp_ref[...])
    bx = jnp.sqrt(1.0 - a * a) * (ig * u)

    order = list(range(nv)) if not reverse else list(range(nv - 1, -1, -1))
    hs, prods = {}, {}
    h = None
    prod = None
    for v in order:
        av = a[v * sl:(v + 1) * sl]
        bv = bx[v * sl:(v + 1) * sl]
        h = bv if h is None else av * h + bv
        prod = av if prod is None else av * prod
        hs[v], prods[v] = h, prod
    h_end, p_end = hs[order[-1]], prods[order[-1]]
    c = h_ref[...]
    carry = {}
    for s in (range(sl) if not reverse else range(sl - 1, -1, -1)):
        carry[s] = c
        c = p_end[s:s + 1] * c + h_end[s:s + 1]
    h_ref[...] = c
    cin = jnp.concatenate([carry[s] for s in range(sl)], axis=0)
    out = jnp.concatenate([hs[v] + prods[v] * cin for v in range(nv)], axis=0)
    return _dot(pt_ref[...], out.astype(BF16)).astype(BF16)


def _rglru(rx, perm, perm_t, cw, cb, wg, bg, nsp):
    bsz, l, dr = rx.shape
    tc = TOKEN_TILE
    n = l // tc
    fwd = pl.BlockSpec((1, tc, dr), lambda b, j: (b, j, 0))
    bwd = pl.BlockSpec((1, tc, dr), lambda b, j: (b, jnp.where(j == 0, 0, n - j), 0))
    out = jax.ShapeDtypeStruct((bsz, l, dr), BF16)
    return pl.pallas_call(
        _rglru_kernel,
        grid=(bsz, n),
        in_specs=[fwd, bwd, _const_spec(perm.shape), _const_spec(perm_t.shape), _const_spec(cw.shape),
                  _const_spec(cb.shape), _const_spec(wg.shape), _const_spec(bg.shape), _const_spec(nsp.shape)],
        out_specs=[fwd, bwd],
        out_shape=[out, out],
        scratch_shapes=[pltpu.VMEM((2, (RNN_CONV_K - 1) * SUBLANES, dr), F32), pltpu.VMEM((2, 1, dr), F32)],
        compiler_params=_params(("arbitrary", "arbitrary")),
        name="rglru",
    )(rx, rx, perm, perm_t, cw, cb, wg, bg, nsp)


def _route(h, rw_ref, rb_ref, tri_ref):
    tm = h.shape[0]
    hi = h.astype(BF16)
    lo = (h - hi.astype(F32)).astype(BF16)
    d = h.shape[1]
    parts = _dot(hi, rw_ref[0:d, :]) + _dot(lo, rw_ref[d:2 * d, :])
    parts_t = parts.T
    logits = parts_t[0:N_EXPERTS] + parts_t[N_EXPERTS:2 * N_EXPERTS]
    scores = _sigmoid(logits)
    sel = scores + rb_ref[...]
    row = lambda t, e: t[e:e + 1, :]
    epg = EXPERTS_PER_GROUP
    gscore = []
    for g in range(N_GROUPS):
        best = None
        for a in range(epg):
            for b in range(a + 1, epg):
                pair = row(sel, g * epg + a) + row(sel, g * epg + b)
                best = pair if best is None else jnp.maximum(best, pair)
        gscore.append(best)
    gbest = gscore[0]
    gidx = jnp.zeros_like(gbest, dtype=jnp.int32)
    for g in range(1, N_GROUPS):
        better = gscore[g] > gbest
        gidx = jnp.where(better, g, gidx)
        gbest = jnp.where(better, gscore[g], gbest)
    picked = []
    flags = []
    for e in range(N_EXPERTS):
        g = e // epg
        rank = jnp.zeros_like(gidx)
        for o in range(g * epg, (g + 1) * epg):
            if o == e:
                continue
            ahead = (row(sel, o) > row(sel, e)) if o > e else (row(sel, o) >= row(sel, e))
            rank = rank + ahead.astype(jnp.int32)
        chosen = jnp.logical_and(gidx == g, rank < 2)
        picked.append(jnp.where(chosen, row(scores, e), 0.0))
        flags.append(jnp.where(chosen, 1.0, 0.0))
    den = picked[0]
    for e in range(1, N_EXPERTS):
        den = den + picked[e]
    gate = jnp.concatenate(picked, axis=0) / den
    flag = jnp.concatenate(flags, axis=0)

    cnt = jnp.sum(flag, axis=1, keepdims=True).astype(jnp.int32)
    pc = jnp.left_shift(jnp.right_shift(cnt + (ROW_ALIGN - 1), ROW_SHIFT), ROW_SHIFT)
    starts = [jnp.zeros((1, 1), jnp.int32)]
    for e in range(1, N_EXPERTS):
        starts.append(starts[-1] + pc[e - 1:e, :])
    seg0 = jnp.concatenate(starts, axis=0).astype(F32)
    pos = seg0 + _dot(flag.astype(BF16), tri_ref[...])
    used = flag > 0.5
    pos_lo = jnp.min(jnp.where(used, pos, 1e9), axis=0, keepdims=True)
    pos_hi = jnp.max(jnp.where(used, pos, -1.0), axis=0, keepdims=True)
    w_lo = jnp.sum(jnp.where(pos == pos_lo, gate, 0.0), axis=0, keepdims=True)
    w_hi = jnp.sum(jnp.where(pos == pos_hi, gate, 0.0), axis=0, keepdims=True)
    info = jnp.concatenate([pos_lo, pos_hi, w_lo, w_hi, jnp.zeros((4, tm), F32)], axis=0)
    return hi, info, jnp.broadcast_to(cnt, (N_EXPERTS, LANES))


def _merge_kernel(n_src, n_ctx_tiles, *refs):
    x_refs = refs[:n_src]
    (g1_ref, sh2_ref, sc2_ref, n2_ref, cv_ref, at_ref, hf_ref, hb_ref, rg_ref, gt_ref,
     pw_ref, no_ref, ro_ref, ow_ref, rw_ref, rb_ref, tri_ref,
     o_ref, h2_ref, info_ref, cnt_ref) = refs[n_src:]
    d = o_ref.shape[2]
    conv = _dot(cv_ref[0], pw_ref[0])
    na = _dot(at_ref[0], no_ref[0])
    hsum = hf_ref[0].astype(F32) + hb_ref[0].astype(F32)
    rnn = _dot((rg_ref[0].astype(F32) * hsum).astype(BF16), ro_ref[0])
    m = (gt_ref[0, :, 0:d].astype(F32) * conv + gt_ref[0, :, d:2 * d].astype(F32) * na
         + gt_ref[0, :, 2 * d:3 * d].astype(F32) * rnn)
    y = _dot(m.astype(BF16), ow_ref[0])
    x = _stream_tile(x_refs, n_ctx_tiles) + g1_ref[0] * y
    o_ref[0] = x
    h = x * lax.rsqrt(jnp.mean(x * x, axis=-1, keepdims=True) + EPS) * n2_ref[...]
    h = h * (1.0 + sc2_ref[0]) + sh2_ref[0]
    h2, info, cnt = _route(h, rw_ref, rb_ref, tri_ref)
    h2_ref[0] = h2
    info_ref[...] = info
    cnt_ref[0] = cnt


def _merge(src, modtab, n2, cv, at, hf, hb, rg, gt, pw, no, ro, ow, layer, rw, rb, tri):
    bsz, l, _ = cv.shape
    d = ow.shape[2]
    tm = TOKEN_TILE
    nt = l // tm
    x_specs, x_args, nc = _stream_specs(src, tm)
    tile = lambda n: pl.BlockSpec((1, tm, n), lambda b, i: (b, i, 0))
    return pl.pallas_call(
        functools.partial(_merge_kernel, len(x_args), nc),
        grid=(bsz, nt),
        in_specs=x_specs + [
            _mod_spec(d, 2), _mod_spec(d, 3), _mod_spec(d, 4), _const_spec((1, d)),
            tile(cv.shape[2]), tile(at.shape[2]), tile(hf.shape[2]), tile(hb.shape[2]),
            tile(rg.shape[2]), tile(3 * d),
            _layer_spec(pw, layer), _layer_spec(no, layer), _layer_spec(ro, layer), _layer_spec(ow, layer),
            _const_spec(rw.shape), _const_spec(rb.shape), _const_spec(tri.shape)],
        out_specs=[tile(d), tile(d),
                   pl.BlockSpec((SUBLANES, tm), lambda b, i: (0, b * nt + i)),
                   pl.BlockSpec((1, N_EXPERTS, LANES), lambda b, i: (b * nt + i, 0, 0))],
        out_shape=[jax.ShapeDtypeStruct((bsz, l, d), F32),
                   jax.ShapeDtypeStruct((bsz, l, d), BF16),
                   jax.ShapeDtypeStruct((SUBLANES, bsz * l), F32),
                   jax.ShapeDtypeStruct((bsz * nt, N_EXPERTS, LANES), jnp.int32)],
        compiler_params=_params(("arbitrary", "arbitrary")),
        name="merge_route",
    )(*x_args, modtab, modtab, modtab, n2, cv, at, hf, hb, rg, gt, pw, no, ro, ow, rw, rb, tri)


def _route_plan(cnt, n_tiles):
    pc = (cnt + (ROW_ALIGN - 1)) // ROW_ALIGN * ROW_ALIGN
    ls = jnp.cumsum(pc, axis=1) - pc
    seg_tot = jnp.sum(pc, axis=0)
    reg = (seg_tot + (EXPERT_TILE - 1)) // EXPERT_TILE * EXPERT_TILE
    reg_end = jnp.cumsum(reg)
    base = reg_end - reg
    gs = base[None, :] + jnp.cumsum(pc, axis=0) - pc
    n_used = reg_end[-1] // EXPERT_TILE
    tiles = jnp.minimum(jnp.arange(n_tiles, dtype=jnp.int32), n_used - 1)
    tile_expert = jnp.minimum(jnp.sum(tiles[:, None] >= (reg_end // EXPERT_TILE)[None, :], axis=1),
                              N_EXPERTS - 1).astype(jnp.int32)
    i32 = lambda a: a.astype(jnp.int32).reshape(-1)
    return dict(pc=i32(pc), ls=i32(ls), gs=i32(gs), tot=i32(jnp.sum(pc, axis=1)),
                tail_start=i32(base + seg_tot), tail_len=i32(reg - seg_tot),
                tile_expert=tile_expert, n_used=i32(n_used))


def _for_each_piece(n, sizes, fn):
    off = jnp.int32(0)
    for size in sizes:
        hit = (n & size) != 0

        @pl.when(hit)
        def _(off=off, size=size):
            fn(pl.multiple_of(off, ROW_ALIGN), size)

        off = off + jnp.where(hit, size, 0)


def _window_copies(pc_ref, ls_ref, gs_ref, win, staged_at, sorted_at, to_sorted, sem, fn):
    for e in range(N_EXPERTS):
        n = pc_ref[win * N_EXPERTS + e]
        src0 = ls_ref[win * N_EXPERTS + e]
        dst0 = gs_ref[win * N_EXPERTS + e]

        def piece(off, size, src0=src0, dst0=dst0):
            a = staged_at(pl.multiple_of(src0 + off, ROW_ALIGN), size)
            b = sorted_at(pl.multiple_of(dst0 + off, ROW_ALIGN), size)
            fn(pltpu.make_async_copy(a, b, sem) if to_sorted else pltpu.make_async_copy(b, a, sem))

        _for_each_piece(n, SEG_SIZES, piece)


def _wait_rows(n, desc):
    for size in WAIT_SIZES:
        @pl.when((n & size) != 0)
        def _(size=size):
            desc(size).wait()


def _one_hot_rows(info_ref, rows, lo_val, hi_val):
    tm = info_ref.shape[1]
    rid = lax.broadcasted_iota(jnp.int32, (rows, tm), 0).astype(F32)
    return jnp.where(rid == info_ref[0:1, :], lo_val, jnp.where(rid == info_ref[1:2, :], hi_val, 0.0))


def _dispatch_kernel(pc_ref, ls_ref, gs_ref, tot_ref, ts_ref, tl_ref, nu_ref, h_ref, info_ref, xs_ref,
                     buf_ref, zero_ref, sem_ref, zsem_ref):
    w = pl.program_id(0)
    nw = pl.num_programs(0)
    slot = w % 2
    rows = buf_ref.shape[1]
    p = _one_hot_rows(info_ref, rows, 1.0, 1.0).astype(BF16)
    buf_ref[slot] = _dot(p, h_ref[...]).astype(BF16)

    def start_copies(win, sl):
        _window_copies(pc_ref, ls_ref, gs_ref, win,
                       lambda r, n: buf_ref.at[sl, pl.ds(r, n)], lambda r, n: xs_ref.at[pl.ds(r, n)],
                       True, sem_ref.at[sl], lambda c: c.start())

    def wait_copies(win, sl):
        _wait_rows(tot_ref[win], lambda n: pltpu.make_async_copy(
            buf_ref.at[sl, pl.ds(0, n)], xs_ref.at[pl.ds(0, n)], sem_ref.at[sl]))

    @pl.when(w > 0)
    def _():
        wait_copies(w - 1, 1 - slot)

    start_copies(w, slot)

    @pl.when(w == nw - 1)
    def _():
        wait_copies(w, slot)
        zero_ref[...] = jnp.zeros_like(zero_ref)

        def tails(fn):
            for e in range(N_EXPERTS):
                start = ts_ref[e]

                def piece(off, size, start=start):
                    fn(pltpu.make_async_copy(zero_ref.at[pl.ds(0, size)],
                                             xs_ref.at[pl.ds(pl.multiple_of(start + off, ROW_ALIGN), size)],
                                             zsem_ref.at[0]))

                _for_each_piece(tl_ref[e], TAIL_SIZES, piece)

        tails(lambda c: c.start())
        tails(lambda c: c.wait())

        fill = zero_ref.shape[0]
        used = nu_ref[0] * EXPERT_TILE

        def spare(k):
            return pltpu.make_async_copy(
                zero_ref, xs_ref.at[pl.ds(pl.multiple_of(used + k * fill, ROW_ALIGN), fill)], zsem_ref.at[0])

        n_fill = (xs_ref.shape[0] - used) // fill
        lax.fori_loop(0, n_fill, lambda k, c: (spare(k).start(), c)[1], 0)
        lax.fori_loop(0, n_fill, lambda k, c: (spare(k).wait(), c)[1], 0)


def _dispatch(h2, info, plan, n_rows):
    t, d = h2.shape
    tm = TOKEN_TILE
    rows = STAGE_ROWS
    grid_spec = pltpu.PrefetchScalarGridSpec(
        num_scalar_prefetch=7,
        grid=(t // tm,),
        in_specs=[pl.BlockSpec((tm, d), lambda w, *_: (w, 0)),
                  pl.BlockSpec((SUBLANES, tm), lambda w, *_: (0, w))],
        out_specs=pl.BlockSpec(memory_space=pl.ANY),
        scratch_shapes=[pltpu.VMEM((2, rows, d), BF16), pltpu.VMEM((TAIL_SIZES[0], d), BF16),
                        pltpu.SemaphoreType.DMA((2,)), pltpu.SemaphoreType.DMA((1,))])
    return pl.pallas_call(
        _dispatch_kernel,
        grid_spec=grid_spec,
        out_shape=jax.ShapeDtypeStruct((n_rows, d), BF16),
        compiler_params=_params(("arbitrary",)),
        name="moe_dispatch",
    )(plan["pc"], plan["ls"], plan["gs"], plan["tot"], plan["tail_start"], plan["tail_len"], plan["n_used"],
      h2, info)


def _experts_kernel(te_ref, nu_ref, xs_ref, w1_ref, w3_ref, w2_ref, ys_ref, w13_ref, w2b_ref, z_ref):
    i = pl.program_id(0)
    de = w1_ref.shape[3]
    prev = te_ref[jnp.maximum(i - 1, 0)]

    @pl.when(jnp.logical_or(i == 0, te_ref[i] != prev))
    def _():
        w13_ref[:, 0:de] = w1_ref[0, 0].astype(BF16)
        w13_ref[:, de:2 * de] = w3_ref[0, 0].astype(BF16)
        w2b_ref[...] = w2_ref[0, 0].astype(BF16)

    @pl.when(i >= nu_ref[0])
    def _():
        ys_ref[...] = jnp.zeros_like(ys_ref)

    @pl.when(i < nu_ref[0])
    def _():
        for blk, r0 in enumerate(range(0, xs_ref.shape[0], EXPERT_SUB)):
            z_ref[blk] = _dot(xs_ref[r0:r0 + EXPERT_SUB, :], w13_ref[...])
        for blk, r0 in enumerate(range(0, xs_ref.shape[0], EXPERT_SUB)):
            he = _silu(z_ref[blk, :, 0:de]) * z_ref[blk, :, de:2 * de]
            ys_ref[r0:r0 + EXPERT_SUB, :] = _dot(he.astype(BF16), w2b_ref[...]).astype(BF16)


def _experts(xs, plan, w1, w3, w2, layer):
    n_rows, d = xs.shape
    de = w1.shape[3]
    te, nu = plan["tile_expert"], plan["n_used"]
    tm = EXPERT_TILE
    grid_spec = pltpu.PrefetchScalarGridSpec(
        num_scalar_prefetch=2,
        grid=(n_rows // tm,),
        in_specs=[pl.BlockSpec((tm, d), lambda i, te, nu: (jnp.minimum(i, nu[0] - 1), 0)),
                  pl.BlockSpec((1, 1, d, de), lambda i, te, nu: (layer, te[i], 0, 0)),
                  pl.BlockSpec((1, 1, d, de), lambda i, te, nu: (layer, te[i], 0, 0)),
                  pl.BlockSpec((1, 1, de, d), lambda i, te, nu: (layer, te[i], 0, 0))],
        out_specs=pl.BlockSpec((tm, d), lambda i, te, nu: (i, 0)),
        scratch_shapes=[pltpu.VMEM((d, 2 * de), BF16), pltpu.VMEM((de, d), BF16),
                        pltpu.VMEM((tm // EXPERT_SUB, EXPERT_SUB, 2 * de), F32)])
    return pl.pallas_call(
        _experts_kernel,
        grid_spec=grid_spec,
        out_shape=jax.ShapeDtypeStruct((n_rows, d), BF16),
        compiler_params=_params(("arbitrary",)),
        name="moe_experts",
    )(te, nu, xs, w1, w3, w2)


def _combine_kernel(pc_ref, ls_ref, gs_ref, tot_ref, x_ref, g2_ref, info_ref, ys_ref, o_ref, buf_ref, sem_ref):
    nt = pl.num_programs(1)
    w = pl.program_id(0) * nt + pl.program_id(1)
    nw = pl.num_programs(0) * nt
    slot = w % 2
    rows = buf_ref.shape[1]

    def start_copies(win, sl):
        _window_copies(pc_ref, ls_ref, gs_ref, win,
                       lambda r, n: buf_ref.at[sl, pl.ds(r, n)], lambda r, n: ys_ref.at[pl.ds(r, n)],
                       False, sem_ref.at[sl], lambda c: c.start())

    @pl.when(w == 0)
    def _():
        buf_ref[...] = jnp.zeros_like(buf_ref)
        start_copies(w, slot)

    @pl.when(w + 1 < nw)
    def _():
        start_copies(w + 1, 1 - slot)

    _wait_rows(tot_ref[w], lambda n: pltpu.make_async_copy(
        ys_ref.at[pl.ds(0, n)], buf_ref.at[slot, pl.ds(0, n)], sem_ref.at[slot]))
    pw =_one_hot_rows(info_ref, rows, info_ref[2:3, :], info_ref[3:4, :]).astype(BF16)
    moe = lax.dot_general(pw, buf_ref[slot], (((0,), (0,)), ((), ())), preferred_element_type=F32)
    o_ref[0] = x_ref[0] + g2_ref[0] * moe


def _combine(xa, modtab, info, ys, plan, drop):
    bsz, l, d = xa.shape
    tm = TOKEN_TILE
    nt = l // tm
    grid_spec = pltpu.PrefetchScalarGridSpec(
        num_scalar_prefetch=4,
        grid=(bsz, nt),
        in_specs=[pl.BlockSpec((1, tm, d), lambda b, i, *_: (b, i, 0)),
                  _mod_spec(d, 5),
                  pl.BlockSpec((SUBLANES, tm), lambda b, i, *_: (0, b * nt + i)),
                  pl.BlockSpec(memory_space=pl.ANY)],
        out_specs=pl.BlockSpec((1, tm, d), lambda b, i, *_: (b, jnp.maximum(i - drop, 0), 0)),
        scratch_shapes=[pltpu.VMEM((2, STAGE_ROWS, d), BF16), pltpu.SemaphoreType.DMA((2,))])
    return pl.pallas_call(
        _combine_kernel,
        grid_spec=grid_spec,
        out_shape=jax.ShapeDtypeStruct((bsz, l - drop * tm, d), F32),
        compiler_params=_params(("arbitrary", "arbitrary")),
        name="moe_combine",
    )(plan["pc"], plan["ls"], plan["gs"], plan["tot"], xa, modtab, info, ys)


def _moe_sparse(xa, modtab, h2, info, cnt, w1, w3, w2, layer, drop):
    bsz, l, d = xa.shape
    t = bsz * l
    n_win = t // TOKEN_TILE
    bound = TOP_K * t + n_win * N_EXPERTS * (ROW_ALIGN - 1) + N_EXPERTS * (EXPERT_TILE - ROW_ALIGN)
    n_tiles = -(-bound // EXPERT_TILE)
    plan = _route_plan(cnt[:, :, 0], n_tiles)
    xs = _dispatch(h2.reshape(t, d), info, plan, n_tiles * EXPERT_TILE)
    ys = _experts(xs, plan, w1, w3, w2, layer)
    return _combine(xa, modtab, info, ys, plan, drop)


def _block_diag(w):
    n, k, _ = w.shape
    eye = jnp.eye(n, dtype=w.dtype)
    return jnp.einsum("nij,nm->nimj", w, eye).reshape(n * k, n * k)


def kernel(x, c, ctx, c_ctx, router_w, router_bias, mod_w, mod_b, norm1_g, norm2_g, in_w, in_b,
           conv_dw_w, conv_dw_b, conv_ln_g, conv_ln_b, conv_pw_w, na_q_g, na_k_g, na_rpb, na_out_w,
           rnn_conv_w, rnn_conv_b, rnn_wa, rnn_ba, rnn_wx, rnn_bx, rnn_lam, rnn_out_w, out_w,
           exp_w1, exp_w3, exp_w2):
    bsz, s, d = x.shape
    n_ctx = ctx.shape[1]
    depth = mod_w.shape[0]
    dc = conv_pw_w.shape[1]
    dn = na_out_w.shape[1]
    dr = rnn_out_w.shape[1]
    sizes = (dc, dn, dr, d)
    assert n_ctx % TOKEN_TILE == 0 and s % TOKEN_TILE == 0 and s % GRID_W == 0
    assert dn == N_HEADS * HEAD_DIM and s // GRID_W >= NA_ROWS

    xa = (ctx, x)
    seq = n_ctx + s

    c_rows = -(-(bsz + 1) // SUBLANES) * SUBLANES
    c_all = jnp.zeros((c_rows, d), F32).at[:bsz].set(c).at[bsz].set(c_ctx)
    mods = _mod_vectors(c_all, mod_w, mod_b)

    head_mean = _block_diag(jnp.full((N_HEADS, HEAD_DIM, HEAD_DIM), 1.0 / HEAD_DIM, F32)).astype(BF16)
    rw_hi = router_w.astype(BF16)
    rw_lo = (router_w - rw_hi.astype(F32)).astype(BF16)
    rw = jnp.zeros((2 * d, LANES), BF16)
    rw = rw.at[:d, 0:N_EXPERTS].set(rw_hi).at[:d, N_EXPERTS:2 * N_EXPERTS].set(rw_lo)
    rw = rw.at[d:, 0:N_EXPERTS].set(rw_hi)
    rb = router_bias.reshape(N_EXPERTS, 1)
    tri = jnp.asarray(np.triu(np.ones((TOKEN_TILE, TOKEN_TILE), np.float32), k=1), BF16)
    shift_a, shift_b = _conv_shift_matrices(TOKEN_TILE + 2 * HALO)
    perm, perm_t = _scan_permutation(TOKEN_TILE)

    in_w_b = in_w.astype(BF16)
    pw_b, no_b, ro_b, ow_b = (a.astype(BF16) for a in (conv_pw_w, na_out_w, rnn_out_w, out_w))

    for l in range(depth):
        ml = mods[l].reshape(c_rows, N_MOD, d)
        modtab = jnp.stack([jnp.broadcast_to(ml[bsz], (bsz, N_MOD, d)), ml[:bsz]], axis=1)
        modtab = modtab.reshape(bsz * 2 * N_MOD, 1, d)

        qg = (jnp.tile(na_q_g[l], N_HEADS) * HEAD_DIM ** -0.5).reshape(1, dn)
        kg = jnp.tile(na_k_g[l], N_HEADS).reshape(1, dn)
        u, q, k, v, rx, rg, gt = _inproj(xa, bsz, seq, modtab, norm1_g[l].reshape(1, d), in_w_b, l,
                                         in_b[l].reshape(1, -1), qg, kg, head_mean, sizes)

        cw = jnp.zeros((CONV_K + 1, dc), F32).at[:CONV_K].set(conv_dw_w[l])
        cv = _conv_branch(u, shift_a, shift_b, cw, conv_dw_b[l].reshape(1, dc),
                          conv_ln_g[l].reshape(1, dc), conv_ln_b[l].reshape(1, dc))

        at = _attention(q, k, v, _bias_strips(na_rpb[l]), n_ctx)

        wg = jnp.stack([jnp.concatenate([_block_diag(rnn_wa[l, dn_]), _block_diag(rnn_wx[l, dn_])], axis=1)
                        for dn_ in range(2)]).astype(BF16)
        bg = jnp.concatenate([rnn_ba[l], rnn_bx[l]], axis=-1).reshape(2, 1, 2 * dr)
        nsp = (-LRU_C * jax.nn.softplus(-rnn_lam[l])).reshape(2, 1, dr)
        rcw = jnp.zeros((2, SUBLANES, dr), F32).at[:, :RNN_CONV_K].set(rnn_conv_w[l])
        hf, hb = _rglru(rx, perm, perm_t, rcw, rnn_conv_b[l].reshape(2, 1, dr), wg, bg, nsp)

        xa, h2, info, cnt = _merge(xa, modtab, norm2_g[l].reshape(1, d), cv, at, hf, hb, rg, gt,
                                   pw_b, no_b, ro_b, ow_b, l, rw, rb, tri)

        drop = n_ctx // TOKEN_TILE if l == depth - 1 else 0
        xa = _moe_sparse(xa, modtab, h2, info, cnt, exp_w1, exp_w3, exp_w2, l, drop)

    return xa
```

```python
import functools
import math

import jax
import jax.numpy as jnp
import numpy as np
from jax import lax
from jax.experimental import pallas as pl
from jax.experimental.pallas import tpu as pltpu

F32 = jnp.float32
BF16 = jnp.bfloat16

GRID_W = 64
CONV_K = 31
N_HEADS = 8
HEAD_DIM = 64
NA_ROWS = 8
NA_COLS = 16
RNN_BLOCKS = 8
RNN_CONV_K = 4
LRU_C = 8.0
N_EXPERTS = 16
N_GROUPS = 4
EXPERTS_PER_GROUP = N_EXPERTS // N_GROUPS
N_MOD = 6
EPS = 1e-6
NEG_INF = -1e30

SUBLANES = 8
LANES = 128
MXU_DIM = 256
VMEM_LIMIT = 56 * 1024 * 1024

TOP_K = 2
TOKEN_TILE = 256
HALO = 16
ROW_ALIGN = 16
ROW_SHIFT = 4
EXPERT_TILE = 512
EXPERT_SUB = 256
SEG_SIZES = tuple(TOKEN_TILE >> k for k in range(TOKEN_TILE.bit_length() - ROW_SHIFT))
TAIL_SIZES = tuple((EXPERT_TILE // 2) >> k for k in range(EXPERT_TILE.bit_length() - 1 - ROW_SHIFT))
STAGE_ROWS = -(-(TOP_K * TOKEN_TILE + N_EXPERTS * (ROW_ALIGN - 1)) // MXU_DIM) * MXU_DIM
WAIT_SIZES = tuple(1 << k for k in range(STAGE_ROWS.bit_length() - 1, ROW_SHIFT - 1, -1))


def _params(sem, vmem=VMEM_LIMIT):
    return pltpu.CompilerParams(dimension_semantics=sem, vmem_limit_bytes=vmem)


def _sigmoid(x):
    return 0.5 * (1.0 + jnp.tanh(0.5 * x))


def _silu(x):
    return x * _sigmoid(x)


def _gelu_tanh(x):
    return x * (0.5 * (1.0 + jnp.tanh(math.sqrt(2.0 / math.pi) * (x + 0.044715 * (x * x * x)))))


def _dot(a, b):
    return jnp.dot(a, b, preferred_element_type=F32)


def _dot_nt(a, b):
    return lax.dot_general(a, b, (((1,), (1,)), ((), ())), preferred_element_type=F32)


def _const_spec(shape):
    return pl.BlockSpec(shape, lambda *_: (0,) * len(shape))


def _layer_spec(arr, layer):
    return pl.BlockSpec((1,) + arr.shape[1:], lambda *_: (layer,) + (0,) * (arr.ndim - 1))


def _mod_kernel(c_ref, w_ref, b_ref, o_ref):
    a = _silu(c_ref[...])
    o_ref[0] = jnp.dot(a, w_ref[0], preferred_element_type=F32,
                       precision=lax.Precision.HIGHEST) + b_ref[0]


def _mod_vectors(c_all, mod_w, mod_b):
    depth, d, n = mod_w.shape
    tn = 1536
    rows = c_all.shape[0]
    return pl.pallas_call(
        _mod_kernel,
        grid=(depth, n // tn),
        in_specs=[pl.BlockSpec((rows, d), lambda l, j: (0, 0)),
                  pl.BlockSpec((1, d, tn), lambda l, j: (l, 0, j)),
                  pl.BlockSpec((1, 1, tn), lambda l, j: (l, 0, j))],
        out_specs=pl.BlockSpec((1, rows, tn), lambda l, j: (l, 0, j)),
        out_shape=jax.ShapeDtypeStruct((depth, rows, n), F32),
        compiler_params=_params(("arbitrary", "arbitrary")),
        name="mod_vectors",
    )(c_all, mod_w, mod_b.reshape(depth, 1, n))


def _mod_spec(d, which):
    return pl.BlockSpec((1, 1, d),
                        lambda b, i, *_: ((2 * b + jnp.minimum(i, 1)) * N_MOD + which, 0, 0))


def _stream_specs(src, tm):
    if not isinstance(src, tuple):
        return [pl.BlockSpec((1, tm, src.shape[2]), lambda b, i, *_: (b, i, 0))], [src], 0
    ctx, lat = src
    nc = ctx.shape[1] // tm
    d = ctx.shape[2]
    return ([pl.BlockSpec((1, tm, d), lambda b, i, *_: (b, jnp.minimum(i, nc - 1), 0)),
             pl.BlockSpec((1, tm, d), lambda b, i, *_: (b, jnp.maximum(i - nc, 0), 0))], [ctx, lat], nc)


def _stream_tile(x_refs, n_ctx_tiles):
    if len(x_refs) == 1:
        return x_refs[0][0]
    return jnp.where(pl.program_id(1) < n_ctx_tiles, x_refs[0][0], x_refs[1][0])


def _inproj_kernel(sizes, n_src, n_ctx_tiles, *refs):
    x_refs = refs[:n_src]
    (sh_ref, sc_ref, g_ref, w_ref, b_ref, qg_ref, kg_ref, bd_ref,
     u_ref, q_ref, k_ref, v_ref, rx_ref, rg_ref, gt_ref, z_ref) = refs[n_src:]
    dc, dn, dr, d = sizes
    x = _stream_tile(x_refs, n_ctx_tiles)
    h = x * lax.rsqrt(jnp.mean(x * x, axis=-1, keepdims=True) + EPS) * g_ref[...]
    h = h * (1.0 + sc_ref[0]) + sh_ref[0]
    hb = h.astype(BF16)

    d_in = w_ref.shape[2]
    for c0 in range(0, d_in, dn):
        z_ref[:, c0:c0 + dn] = _dot(hb, w_ref[0, :, c0:c0 + dn]) + b_ref[:, c0:c0 + dn]

    def proj(off, n):
        return z_ref[:, off:off + n]

    def head_norm(t, gain_ref):
        ms = _dot((t * t).astype(BF16), bd_ref[...])
        return t * lax.rsqrt(ms + EPS) * gain_ref[...]

    off = 0
    a = proj(off, dc); off += dc
    ga = proj(off, dc); off += dc
    u_ref[0] = (a * _sigmoid(ga)).astype(BF16)
    q_ref[0] = head_norm(proj(off, dn), qg_ref).astype(BF16); off += dn
    k_ref[0] = head_norm(proj(off, dn), kg_ref).astype(BF16); off += dn
    v_ref[0] = proj(off, dn).astype(BF16); off += dn
    rx_ref[0] = proj(off, dr).astype(BF16); off += dr
    rg_ref[0] = _gelu_tanh(proj(off, dr)).astype(BF16); off += dr
    for j in range(3):
        gt_ref[0, :, j * d:(j + 1) * d] = _sigmoid(proj(off, d)).astype(BF16); off += d


def _inproj(src, bsz, l, modtab, g, w, layer, b, qg, kg, bd, sizes):
    dc, dn, dr, d = sizes
    d_in = w.shape[2]
    tm = TOKEN_TILE
    x_specs, x_args, nc = _stream_specs(src, tm)
    tile = lambda n: pl.BlockSpec((1, tm, n), lambda b, i: (b, i, 0))
    out = lambda n: jax.ShapeDtypeStruct((bsz, l, n), BF16)
    return pl.pallas_call(
        functools.partial(_inproj_kernel, sizes, len(x_args), nc),
        grid=(bsz, l // tm),
        in_specs=x_specs + [_mod_spec(d, 0), _mod_spec(d, 1), _const_spec((1, d)),
                            pl.BlockSpec((1, d, d_in), lambda b, i: (layer, 0, 0), pipeline_mode=pl.Buffered(1)),
                            _const_spec((1, d_in)), _const_spec((1, dn)), _const_spec((1, dn)),
                            _const_spec((dn, dn))],
        out_specs=[tile(dc), tile(dn), tile(dn), tile(dn), tile(dr), tile(dr), tile(3 * d)],
        out_shape=[out(dc), out(dn), out(dn), out(dn), out(dr), out(dr), out(3 * d)],
        scratch_shapes=[pltpu.VMEM((tm, d_in), F32)],
        compiler_params=_params(("arbitrary", "arbitrary")),
        name="inproj",
    )(*x_args, modtab, modtab, g, w, b, qg, kg, bd)


CONV_SPLIT = 128


def _conv_shift_matrices(rows):
    ka = CONV_SPLIT + HALO
    nb = rows - CONV_SPLIT
    sa = np.zeros(((SUBLANES - 1) * CONV_SPLIT, ka), np.float32)
    sb = np.zeros(((SUBLANES - 1) * nb, nb), np.float32)
    for s in range(1, SUBLANES):
        for r in range(CONV_SPLIT):
            sa[(s - 1) * CONV_SPLIT + r, r + s] = 1.0
        for r in range(nb - s):
            sb[(s - 1) * nb + r, r + s] = 1.0
    return jnp.asarray(sa, BF16), jnp.asarray(sb, BF16)


def _conv_kernel(u_ref, prev_ref, next_ref, sa_ref, sb_ref, w_ref, b_ref, lg_ref, lb_ref, o_ref,
                 sh_ref, y_ref):
    i = pl.program_id(1)
    n = pl.num_programs(1)
    tm = u_ref.shape[1]
    dc = u_ref.shape[2]
    rows = tm + 2 * HALO
    ka = sa_ref.shape[1]
    nb = rows - CONV_SPLIT
    prev_ok = i >= 2
    next_ok = jnp.logical_and(i >= 1, i < n - 1)
    zero = jnp.zeros((HALO, dc), BF16)
    win = jnp.concatenate([jnp.where(prev_ok, prev_ref[0], zero), u_ref[0],
                           jnp.where(next_ok, next_ref[0], zero)], axis=0)
    sh_ref[0] = win.astype(F32)
    rc = 64
    base = HALO - CONV_K // 2
    for c0 in range(0, dc, MXU_DIM):
        hs = slice(c0, c0 + MXU_DIM)
        sha = _dot(sa_ref[...], win[0:ka, hs])
        shb = _dot(sb_ref[...], win[CONV_SPLIT:rows, hs])
        for s in range(1, SUBLANES):
            sh_ref[s, 0:CONV_SPLIT, hs] = sha[(s - 1) * CONV_SPLIT:s * CONV_SPLIT]
            sh_ref[s, CONV_SPLIT:rows, hs] = shb[(s - 1) * nb:s * nb]
    for c0 in range(0, dc, MXU_DIM):
        for cb in range(c0 // LANES, (c0 + MXU_DIM) // LANES):
            cs = slice(cb * LANES, (cb + 1) * LANES)
            for r0 in range(0, tm, rc):
                acc = jnp.zeros((rc, LANES), F32)
                for k in range(CONV_K):
                    o = base + k
                    s, a = o % SUBLANES, o // SUBLANES
                    acc = acc + sh_ref[s, r0 + a * SUBLANES:r0 + a * SUBLANES + rc, cs] * w_ref[k:k + 1, cs]
                y_ref[r0:r0 + rc, cs] = acc
    y = y_ref[...] + b_ref[...]
    mu = jnp.mean(y, axis=-1, keepdims=True)
    yc = y - mu
    var = jnp.mean(yc * yc, axis=-1, keepdims=True)
    z = yc * lax.rsqrt(var + EPS) * lg_ref[...] + lb_ref[...]
    o_ref[0] = _silu(z).astype(BF16)


def _conv_branch(u, sa, sb, w, b, lg, lb):
    bsz, l, dc = u.shape
    tm = TOKEN_TILE
    hb = tm // HALO
    nh = l // HALO
    return pl.pallas_call(
        _conv_kernel,
        grid=(bsz, l // tm),
        in_specs=[pl.BlockSpec((1, tm, dc), lambda b, i: (b, i, 0)),
                  pl.BlockSpec((1, HALO, dc), lambda b, i: (b, jnp.maximum(i * hb - 1, 0), 0)),
                  pl.BlockSpec((1, HALO, dc), lambda b, i: (b, jnp.minimum((i + 1) * hb, nh - 1), 0)),
                  _const_spec(sa.shape), _const_spec(sb.shape),
                  _const_spec(w.shape), _const_spec((1, dc)), _const_spec((1, dc)), _const_spec((1, dc))],
        out_specs=pl.BlockSpec((1, tm, dc), lambda b, i: (b, i, 0)),
        out_shape=jax.ShapeDtypeStruct((bsz, l, dc), BF16),
        scratch_shapes=[pltpu.VMEM((SUBLANES, tm + 2 * HALO, dc), F32), pltpu.VMEM((tm, dc), F32)],
        compiler_params=_params(("arbitrary", "arbitrary")),
        name="conv_branch",
    )(u, u, u, sa, sb, w, b, lg, lb)


def _softmax_pv(parts):
    m = None
    for s, _ in parts:
        mi = jnp.max(s, axis=-1, keepdims=True)
        m = mi if m is None else jnp.maximum(m, mi)
    colsum = None
    acc = None
    for s, v in parts:
        p = jnp.exp((s - m).astype(BF16))
        for c0 in range(0, p.shape[1], LANES):
            blk = p[:, c0:c0 + LANES]
            colsum = blk if colsum is None else colsum + blk
        pv = _dot(p, v)
        acc = pv if acc is None else acc + pv
    den = jnp.sum(colsum.astype(F32), axis=-1, keepdims=True)
    return acc / den


def _attn_kernel(n_ctx, q_ref, k_ref, v_ref, strip_ref, o_ref, s_ref):
    j = pl.program_id(1)
    tq = q_ref.shape[1]
    rows_per_step = tq // GRID_W
    n_rows = (k_ref.shape[1] - n_ctx) // GRID_W
    n_loc = NA_ROWS * GRID_W
    pairs = N_HEADS // 2

    def split_heads(q2):
        lane = lax.broadcasted_iota(jnp.int32, q2.shape, 1)
        zero = jnp.zeros_like(q2)
        return jnp.concatenate([jnp.where(lane < HEAD_DIM, q2, zero),
                                jnp.where(lane >= HEAD_DIM, q2, zero)], axis=0)

    def merge_heads(o, m):
        lane = lax.broadcasted_iota(jnp.int32, (m, LANES), 1)
        return jnp.where(lane < HEAD_DIM, o[0:m], o[m:2 * m])

    @pl.when(j == 0)
    def _():
        for hp in range(pairs):
            cs = slice(hp * LANES, (hp + 1) * LANES)
            qs = split_heads(q_ref[0, :, cs])
            s = _dot_nt(qs, k_ref[0, 0:n_ctx, cs])
            o = _softmax_pv([(s, v_ref[0, 0:n_ctx, cs])])
            o_ref[0, :, cs] = merge_heads(o, tq).astype(BF16)

    @pl.when(j > 0)
    def _():
        rows2 = 2 * GRID_W
        starts = []
        for rr in range(rows_per_step):
            r = (j - 1) * rows_per_step + rr
            rs = jnp.clip(r - NA_ROWS // 2, 0, n_rows - NA_ROWS)
            starts.append((rs - r + (NA_ROWS - 1), pl.multiple_of(n_ctx + rs * GRID_W, GRID_W)))
        for hp in range(pairs):
            cs = slice(hp * LANES, (hp + 1) * LANES)
            qs_all = jnp.concatenate([split_heads(q_ref[0, rr * GRID_W:(rr + 1) * GRID_W, cs])
                                      for rr in range(rows_per_step)], axis=0)
            s_ctx = _dot_nt(qs_all, k_ref[0, 0:n_ctx, cs])
            for rr in range(rows_per_step):
                case, kstart = starts[rr]
                slot = rr * pairs + hp
                bias = jnp.concatenate([strip_ref[2 * hp, case], strip_ref[2 * hp + 1, case]], axis=0)
                s_ref[slot, :, 0:n_loc] = _dot_nt(qs_all[rr * rows2:(rr + 1) * rows2],
                                                  k_ref[0, pl.ds(kstart, n_loc), cs]) + bias
                s_ref[slot, :, n_loc:n_loc + n_ctx] = s_ctx[rr * rows2:(rr + 1) * rows2]
        for rr in range(rows_per_step):
            _, kstart = starts[rr]
            rsl = slice(rr * GRID_W, (rr + 1) * GRID_W)
            for hp in range(pairs):
                cs = slice(hp * LANES, (hp + 1) * LANES)
                slot = rr * pairs + hp
                o = _softmax_pv([(s_ref[slot, :, 0:n_loc], v_ref[0, pl.ds(kstart, n_loc), cs]),
                                 (s_ref[slot, :, n_loc:n_loc + n_ctx], v_ref[0, 0:n_ctx, cs])])
                o_ref[0, rsl, cs] = merge_heads(o, GRID_W).astype(BF16)


def _attention(q, k, v, strips, n_ctx):
    bsz, l, dn = q.shape
    tq = TOKEN_TILE
    whole = lambda: pl.BlockSpec((1, l, dn), lambda b, j: (b, 0, 0), pipeline_mode=pl.Buffered(1))
    return pl.pallas_call(
        functools.partial(_attn_kernel, n_ctx),
        grid=(bsz, l // tq),
        in_specs=[pl.BlockSpec((1, tq, dn), lambda b, j: (b, j, 0)), whole(), whole(),
                  pl.BlockSpec(strips.shape, lambda b, j: (0, 0, 0, 0), pipeline_mode=pl.Buffered(1))],
        out_specs=pl.BlockSpec((1, tq, dn), lambda b, j: (b, j, 0)),
        out_shape=jax.ShapeDtypeStruct((bsz, l, dn), BF16),
        scratch_shapes=[pltpu.VMEM(((tq // GRID_W) * (N_HEADS // 2), 2 * GRID_W, NA_ROWS * GRID_W + n_ctx), F32)],
        compiler_params=_params(("arbitrary", "arbitrary")),
        name="attention",
    )(q, k, v, strips)


def _bias_strips(rpb):
    col = np.arange(GRID_W)
    start = np.clip(col - NA_COLS // 2, 0, GRID_W - NA_COLS)
    mask = (col[None, :] >= start[:, None]) & (col[None, :] < start[:, None] + NA_COLS)
    dc_idx = np.clip(col[None, :] - col[:, None] + NA_COLS - 1, 0, 2 * NA_COLS - 2)
    pick = jnp.asarray(dc_idx[None] == np.arange(2 * NA_COLS - 1)[:, None, None], F32)
    bias = jnp.einsum("hrc,cqk->hrqk", rpb.astype(F32), pick, precision=lax.Precision.HIGHEST)
    bias = jnp.where(jnp.asarray(mask)[None, None], bias, NEG_INF)
    cases = [jnp.concatenate([bias[:, c + j] for j in range(NA_ROWS)], axis=-1) for c in range(NA_ROWS)]
    return jnp.stack(cases, axis=1)


def _scan_permutation(tc):
    p = np.arange(tc)
    t = (p % SUBLANES) * (tc // SUBLANES) + p // SUBLANES
    m = np.zeros((tc, tc), np.float32)
    m[p, t] = 1.0
    return jnp.asarray(m, BF16), jnp.asarray(m.T, BF16)


def _rglru_kernel(xf_ref, xb_ref, p_ref, pt_ref, cw_ref, cb_ref, wg_ref, bg_ref, nsp_ref, of_ref, ob_ref,
                  edge_ref, h_ref):
    j = pl.program_id(1)

    @pl.when(j == 0)
    def _():
        h_ref[...] = jnp.zeros_like(h_ref)

    @pl.when(j <= 1)
    def _():
        edge_ref[...] = jnp.zeros_like(edge_ref)

    chains = [_rglru_chunk(dirn == 1, x_ref[0], p_ref, pt_ref, cw_ref.at[dirn], cb_ref.at[dirn],
                           wg_ref.at[dirn], bg_ref.at[dirn], nsp_ref.at[dirn],
                           edge_ref.at[dirn], h_ref.at[dirn])
              for dirn, x_ref in enumerate((xf_ref, xb_ref))]
    outs = [None, None]
    for _ in range(RGLRU_PHASES):
        outs = [next(chain) for chain in chains]
    of_ref[0] = outs[0]
    ob_ref[0] = outs[1]


RGLRU_PHASES = 6


def _rglru_chunk(reverse, x, p_ref, pt_ref, cw_ref, cb_ref, wg_ref, bg_ref, nsp_ref, edge_ref, h_ref):
    tc, dr = x.shape
    sl = SUBLANES
    nv = tc // sl
    reach = (RNN_CONV_K - 1) * sl
    xp = _dot(p_ref[...], x)
    yield None
    sub = lax.broadcasted_iota(jnp.int32, (sl, dr), 0)
    prev = edge_ref[...]
    if not reverse:
        edge = xp[tc - reach:tc]
        fixed = jnp.concatenate(
            [jnp.where(sub == 0, pltpu.roll(prev[v * sl:(v + 1) * sl], 1, 0),
                       pltpu.roll(edge[v * sl:(v + 1) * sl], 1, 0)) for v in range(RNN_CONV_K - 1)], axis=0)
        shifted = [xp] + [jnp.concatenate([fixed[reach - d * sl:reach], xp[0:tc - d * sl]], axis=0)
                          for d in range(1, RNN_CONV_K)]
    else:
        edge = xp[0:reach]
        fixed = jnp.concatenate(
            [jnp.where(sub == sl - 1, pltpu.roll(prev[v * sl:(v + 1) * sl], sl - 1, 0),
                       pltpu.roll(edge[v * sl:(v + 1) * sl], sl - 1, 0)) for v in range(RNN_CONV_K - 1)], axis=0)
        shifted = [xp] + [jnp.concatenate([xp[d * sl:tc], fixed[0:d * sl]], axis=0)
                          for d in range(1, RNN_CONV_K)]
    edge_ref[...] = edge
    u = jnp.zeros((tc, dr), F32) + cb_ref[...]
    for kk in range(RNN_CONV_K):
        u = u + shifted[RNN_CONV_K - 1 - kk] * cw_ref[kk:kk + 1, :]
    yield None

    z = _dot(u.astype(BF16), wg_ref[...]) + bg_ref[...]
    yield None
    r = _sigmoid(z[:, 0:dr])
    ig = _sigmoid(z[:, dr:2 * dr])
    a = jnp.exp(r * nsp_ref[...])
    bx = jnp.sqrt(1.0 - a * a) * (ig * u)
    yield None

    order = list(range(nv)) if not reverse else list(range(nv - 1, -1, -1))
    hs, prods = {}, {}
    h = None
    prod = None
    for v in order:
        av = a[v * sl:(v + 1) * sl]
        bv = bx[v * sl:(v + 1) * sl]
        h = bv if h is None else av * h + bv
        prod = av if prod is None else av * prod
        hs[v], prods[v] = h, prod
    h_end, p_end = hs[order[-1]], prods[order[-1]]
    c = h_ref[...]
    carry = {}
    for s in (range(sl) if not reverse else range(sl - 1, -1, -1)):
        carry[s] = c
        c = p_end[s:s + 1] * c + h_end[s:s + 1]
    h_ref[...] = c
    cin = jnp.concatenate([carry[s] for s in range(sl)], axis=0)
    out = jnp.concatenate([hs[v] + prods[v] * cin for v in range(nv)], axis=0)
    yield None
    yield _dot(pt_ref[...], out.astype(BF16)).astype(BF16)


def _rglru(rx, perm, perm_t, cw, cb, wg, bg, nsp):
    bsz, l, dr = rx.shape
    tc = TOKEN_TILE
    n = l // tc
    fwd = pl.BlockSpec((1, tc, dr), lambda b, j: (b, j, 0))
    bwd = pl.BlockSpec((1, tc, dr), lambda b, j: (b, jnp.where(j == 0, 0, n - j), 0))
    out = jax.ShapeDtypeStruct((bsz, l, dr), BF16)
    return pl.pallas_call(
        _rglru_kernel,
        grid=(bsz, n),
        in_specs=[fwd, bwd, _const_spec(perm.shape), _const_spec(perm_t.shape), _const_spec(cw.shape),
                  _const_spec(cb.shape), _const_spec(wg.shape), _const_spec(bg.shape), _const_spec(nsp.shape)],
        out_specs=[fwd, bwd],
        out_shape=[out, out],
        scratch_shapes=[pltpu.VMEM((2, (RNN_CONV_K - 1) * SUBLANES, dr), F32), pltpu.VMEM((2, 1, dr), F32)],
        compiler_params=_params(("arbitrary", "arbitrary")),
        name="rglru",
    )(rx, rx, perm, perm_t, cw, cb, wg, bg, nsp)


def _route(h, rw_ref, rb_ref, tri_ref):
    tm = h.shape[0]
    hi = h.astype(BF16)
    lo = (h - hi.astype(F32)).astype(BF16)
    d = h.shape[1]
    parts = _dot(hi, rw_ref[0:d, :]) + _dot(lo, rw_ref[d:2 * d, :])
    parts_t = parts.T
    logits = parts_t[0:N_EXPERTS] + parts_t[N_EXPERTS:2 * N_EXPERTS]
    scores = _sigmoid(logits)
    sel = scores + rb_ref[...]
    row = lambda t, e: t[e:e + 1, :]
    epg = EXPERTS_PER_GROUP
    gscore = []
    for g in range(N_GROUPS):
        best = None
        for a in range(epg):
            for b in range(a + 1, epg):
                pair = row(sel, g * epg + a) + row(sel, g * epg + b)
                best = pair if best is None else jnp.maximum(best, pair)
        gscore.append(best)
    gbest = gscore[0]
    gidx = jnp.zeros_like(gbest, dtype=jnp.int32)
    for g in range(1, N_GROUPS):
        better = gscore[g] > gbest
        gidx = jnp.where(better, g, gidx)
        gbest = jnp.where(better, gscore[g], gbest)
    picked = []
    flags = []
    for e in range(N_EXPERTS):
        g = e // epg
        rank = jnp.zeros_like(gidx)
        for o in range(g * epg, (g + 1) * epg):
            if o == e:
                continue
            ahead = (row(sel, o) > row(sel, e)) if o > e else (row(sel, o) >= row(sel, e))
            rank = rank + ahead.astype(jnp.int32)
        chosen = jnp.logical_and(gidx == g, rank < 2)
        picked.append(jnp.where(chosen, row(scores, e), 0.0))
        flags.append(jnp.where(chosen, 1.0, 0.0))
    den = picked[0]
    for e in range(1, N_EXPERTS):
        den = den + picked[e]
    gate = jnp.concatenate(picked, axis=0) / den
    flag = jnp.concatenate(flags, axis=0)

    cnt = jnp.sum(flag, axis=1, keepdims=True).astype(jnp.int32)
    pc = jnp.left_shift(jnp.right_shift(cnt + (ROW_ALIGN - 1), ROW_SHIFT), ROW_SHIFT)
    starts = [jnp.zeros((1, 1), jnp.int32)]
    for e in range(1, N_EXPERTS):
        starts.append(starts[-1] + pc[e - 1:e, :])
    seg0 = jnp.concatenate(starts, axis=0).astype(F32)
    pos = seg0 + _dot(flag.astype(BF16), tri_ref[...])
    used = flag > 0.5
    pos_lo = jnp.min(jnp.where(used, pos, 1e9), axis=0, keepdims=True)
    pos_hi = jnp.max(jnp.where(used, pos, -1.0), axis=0, keepdims=True)
    w_lo = jnp.sum(jnp.where(pos == pos_lo, gate, 0.0), axis=0, keepdims=True)
    w_hi = jnp.sum(jnp.where(pos == pos_hi, gate, 0.0), axis=0, keepdims=True)
    info = jnp.concatenate([pos_lo, pos_hi, w_lo, w_hi, jnp.zeros((4, tm), F32)], axis=0)
    return hi, info, jnp.broadcast_to(cnt, (N_EXPERTS, LANES))


MERGE_TILES = 2


def _merge_kernel(n_src, n_ctx_tiles, tiles_per_batch, *refs):
    n_in = n_src + 9
    shared = refs[MERGE_TILES * n_in:MERGE_TILES * n_in + 8]
    n2_ref, pw_ref, no_ref, ro_ref, ow_ref, rw_ref, rb_ref, tri_ref = shared
    o_ref, h2_ref, info_ref, cnt_ref = refs[MERGE_TILES * n_in + 8:]
    d = o_ref.shape[1]
    tm = o_ref.shape[0] // MERGE_TILES
    chains = []
    for h in range(MERGE_TILES):
        tile_refs = refs[h * n_in:(h + 1) * n_in]
        chains.append((h, tile_refs[:n_src]) + tuple(tile_refs[n_src:]))
    branch = []
    for h, x_refs, g1_ref, sh2_ref, sc2_ref, cv_ref, at_ref, hf_ref, hb_ref, rg_ref, gt_ref in chains:
        hsum = hf_ref[0].astype(F32) + hb_ref[0].astype(F32)
        branch.append((_dot(cv_ref[0], pw_ref[0]), _dot(at_ref[0], no_ref[0]),
                       _dot((rg_ref[0].astype(F32) * hsum).astype(BF16), ro_ref[0])))
    ys = []
    for (conv, na, rnn), chain in zip(branch, chains):
        gt_ref = chain[-1]
        m = (gt_ref[0, :, 0:d].astype(F32) * conv + gt_ref[0, :, d:2 * d].astype(F32) * na
             + gt_ref[0, :, 2 * d:3 * d].astype(F32) * rnn)
        ys.append(_dot(m.astype(BF16), ow_ref[0]))
    hms = []
    for y, (h, x_refs, g1_ref, sh2_ref, sc2_ref, *_) in zip(ys, chains):
        if n_src == 1:
            x_in = x_refs[0][0]
        else:
            tile = lax.rem(pl.program_id(0) * MERGE_TILES + h, tiles_per_batch)
            x_in = jnp.where(tile < n_ctx_tiles, x_refs[0][0], x_refs[1][0])
        x = x_in + g1_ref[0] * y
        o_ref[h * tm:(h + 1) * tm, :] = x
        hm = x * lax.rsqrt(jnp.mean(x * x, axis=-1, keepdims=True) + EPS) * n2_ref[...]
        hms.append(hm * (1.0 + sc2_ref[0]) + sh2_ref[0])
    for h, hm in enumerate(hms):
        h2, info, cnt = _route(hm, rw_ref, rb_ref, tri_ref)
        h2_ref[h * tm:(h + 1) * tm, :] = h2
        info_ref[:, h * tm:(h + 1) * tm] = info
        cnt_ref[h] = cnt


def _merge(src, modtab, n2, cv, at, hf, hb, rg, gt, pw, no, ro, ow, layer, rw, rb, tri):
    bsz, l, _ = cv.shape
    d = ow.shape[2]
    tm = TOKEN_TILE
    nt = l // tm
    assert (bsz * nt) % MERGE_TILES == 0
    is_pair = isinstance(src, tuple)
    nc = src[0].shape[1] // tm if is_pair else 0
    in_specs, args = [], []
    for h in range(MERGE_TILES):
        def where(k, h=h):
            t = k * MERGE_TILES + h
            return t // nt, lax.rem(t, nt), t

        def tile(n, where=where):
            return pl.BlockSpec((1, tm, n), lambda k: where(k)[:2] + (0,))

        def mod(which, where=where):
            def index(k):
                b, i, _ = where(k)
                return ((2 * b + jnp.minimum(i, 1)) * N_MOD + which, 0, 0)
            return pl.BlockSpec((1, 1, d), index)

        if is_pair:
            in_specs += [pl.BlockSpec((1, tm, d), lambda k, where=where: (where(k)[0], jnp.minimum(where(k)[1], nc - 1), 0)),
                         pl.BlockSpec((1, tm, d), lambda k, where=where: (where(k)[0], jnp.maximum(where(k)[1] - nc, 0), 0))]
            args += list(src)
        else:
            in_specs.append(tile(d))
            args.append(src)
        in_specs += [mod(2), mod(3), mod(4), tile(cv.shape[2]), tile(at.shape[2]), tile(hf.shape[2]),
                     tile(hb.shape[2]), tile(rg.shape[2]), tile(3 * d)]
        args += [modtab, modtab, modtab, cv, at, hf, hb, rg, gt]
    in_specs += [_const_spec((1, d)), _layer_spec(pw, layer), _layer_spec(no, layer), _layer_spec(ro, layer),
                 _layer_spec(ow, layer), _const_spec(rw.shape), _const_spec(rb.shape), _const_spec(tri.shape)]
    args += [n2, pw, no, ro, ow, rw, rb, tri]
    rows = MERGE_TILES * tm
    xa, h2, info, cnt = pl.pallas_call(
        functools.partial(_merge_kernel, 2 if is_pair else 1, nc, nt),
        grid=(bsz * nt // MERGE_TILES,),
        in_specs=in_specs,
        out_specs=[pl.BlockSpec((rows, d), lambda k: (k, 0)), pl.BlockSpec((rows, d), lambda k: (k, 0)),
                   pl.BlockSpec((SUBLANES, rows), lambda k: (0, k)),
                   pl.BlockSpec((MERGE_TILES, N_EXPERTS, LANES), lambda k: (k, 0, 0))],
        out_shape=[jax.ShapeDtypeStruct((bsz * l, d), F32), jax.ShapeDtypeStruct((bsz * l, d), BF16),
                   jax.ShapeDtypeStruct((SUBLANES, bsz * l), F32),
                   jax.ShapeDtypeStruct((bsz * nt, N_EXPERTS, LANES), jnp.int32)],
        compiler_params=_params(("arbitrary",)),
        name="merge_route",
    )(*args)
    return xa.reshape(bsz, l, d), h2.reshape(bsz, l, d), info, cnt


def _route_plan(cnt, n_tiles):
    pc = (cnt + (ROW_ALIGN - 1)) // ROW_ALIGN * ROW_ALIGN
    ls = jnp.cumsum(pc, axis=1) - pc
    seg_tot = jnp.sum(pc, axis=0)
    reg = (seg_tot + (EXPERT_TILE - 1)) // EXPERT_TILE * EXPERT_TILE
    reg_end = jnp.cumsum(reg)
    base = reg_end - reg
    gs = base[None, :] + jnp.cumsum(pc, axis=0) - pc
    n_used = reg_end[-1] // EXPERT_TILE
    tiles = jnp.minimum(jnp.arange(n_tiles, dtype=jnp.int32), n_used - 1)
    tile_expert = jnp.minimum(jnp.sum(tiles[:, None] >= (reg_end // EXPERT_TILE)[None, :], axis=1),
                              N_EXPERTS - 1).astype(jnp.int32)
    i32 = lambda a: a.astype(jnp.int32).reshape(-1)
    return dict(pc=i32(pc), ls=i32(ls), gs=i32(gs), tot=i32(jnp.sum(pc, axis=1)),
                tail_start=i32(base + seg_tot), tail_len=i32(reg - seg_tot),
                tile_expert=tile_expert, n_used=i32(n_used))


def _for_each_piece(n, sizes, fn):
    off = jnp.int32(0)
    for size in sizes:
        hit = (n & size) != 0

        @pl.when(hit)
        def _(off=off, size=size):
            fn(pl.multiple_of(off, ROW_ALIGN), size)

        off = off + jnp.where(hit, size, 0)


def _window_copies(pc_ref, ls_ref, gs_ref, win, staged_at, sorted_at, to_sorted, sem, fn):
    for e in range(N_EXPERTS):
        n = pc_ref[win * N_EXPERTS + e]
        src0 = ls_ref[win * N_EXPERTS + e]
        dst0 = gs_ref[win * N_EXPERTS + e]

        def piece(off, size, src0=src0, dst0=dst0):
            a = staged_at(pl.multiple_of(src0 + off, ROW_ALIGN), size)
            b = sorted_at(pl.multiple_of(dst0 + off, ROW_ALIGN), size)
            fn(pltpu.make_async_copy(a, b, sem) if to_sorted else pltpu.make_async_copy(b, a, sem))

        _for_each_piece(n, SEG_SIZES, piece)


def _wait_rows(n, desc):
    for size in WAIT_SIZES:
        @pl.when((n & size) != 0)
        def _(size=size):
            desc(size).wait()


def _one_hot_rows(info_ref, rows, lo_val, hi_val):
    tm = info_ref.shape[1]
    rid = lax.broadcasted_iota(jnp.int32, (rows, tm), 0).astype(F32)
    return jnp.where(rid == info_ref[0:1, :], lo_val, jnp.where(rid == info_ref[1:2, :], hi_val, 0.0))


def _dispatch_kernel(pc_ref, ls_ref, gs_ref, tot_ref, ts_ref, tl_ref, nu_ref, h_ref, info_ref, xs_ref,
                     buf_ref, zero_ref, sem_ref, zsem_ref):
    w = pl.program_id(0)
    nw = pl.num_programs(0)
    slot = w % 2
    rows = buf_ref.shape[1]
    p = _one_hot_rows(info_ref, rows, 1.0, 1.0).astype(BF16)
    buf_ref[slot] = _dot(p, h_ref[...]).astype(BF16)

    def start_copies(win, sl):
        _window_copies(pc_ref, ls_ref, gs_ref, win,
                       lambda r, n: buf_ref.at[sl, pl.ds(r, n)], lambda r, n: xs_ref.at[pl.ds(r, n)],
                       True, sem_ref.at[sl], lambda c: c.start())

    def wait_copies(win, sl):
        _wait_rows(tot_ref[win], lambda n: pltpu.make_async_copy(
            buf_ref.at[sl, pl.ds(0, n)], xs_ref.at[pl.ds(0, n)], sem_ref.at[sl]))

    @pl.when(w > 0)
    def _():
        wait_copies(w - 1, 1 - slot)

    start_copies(w, slot)

    @pl.when(w == nw - 1)
    def _():
        wait_copies(w, slot)
        zero_ref[...] = jnp.zeros_like(zero_ref)

        def tails(fn):
            for e in range(N_EXPERTS):
                start = ts_ref[e]

                def piece(off, size, start=start):
                    fn(pltpu.make_async_copy(zero_ref.at[pl.ds(0, size)],
                                             xs_ref.at[pl.ds(pl.multiple_of(start + off, ROW_ALIGN), size)],
                                             zsem_ref.at[0]))

                _for_each_piece(tl_ref[e], TAIL_SIZES, piece)

        tails(lambda c: c.start())
        tails(lambda c: c.wait())

        fill = zero_ref.shape[0]
        used = nu_ref[0] * EXPERT_TILE

        def spare(k):
            return pltpu.make_async_copy(
                zero_ref, xs_ref.at[pl.ds(pl.multiple_of(used + k * fill, ROW_ALIGN), fill)], zsem_ref.at[0])

        n_fill = (xs_ref.shape[0] - used) // fill
        lax.fori_loop(0, n_fill, lambda k, c: (spare(k).start(), c)[1], 0)
        lax.fori_loop(0, n_fill, lambda k, c: (spare(k).wait(), c)[1], 0)


def _dispatch(h2, info, plan, n_rows):
    t, d = h2.shape
    tm = TOKEN_TILE
    rows = STAGE_ROWS
    grid_spec = pltpu.PrefetchScalarGridSpec(
        num_scalar_prefetch=7,
        grid=(t // tm,),
        in_specs=[pl.BlockSpec((tm, d), lambda w, *_: (w, 0)),
                  pl.BlockSpec((SUBLANES, tm), lambda w, *_: (0, w))],
        out_specs=pl.BlockSpec(memory_space=pl.ANY),
        scratch_shapes=[pltpu.VMEM((2, rows, d), BF16), pltpu.VMEM((TAIL_SIZES[0], d), BF16),
                        pltpu.SemaphoreType.DMA((2,)), pltpu.SemaphoreType.DMA((1,))])
    return pl.pallas_call(
        _dispatch_kernel,
        grid_spec=grid_spec,
        out_shape=jax.ShapeDtypeStruct((n_rows, d), BF16),
        compiler_params=_params(("arbitrary",)),
        name="moe_dispatch",
    )(plan["pc"], plan["ls"], plan["gs"], plan["tot"], plan["tail_start"], plan["tail_len"], plan["n_used"],
      h2, info)


def _experts_kernel(te_ref, nu_ref, xs_ref, w1_ref, w3_ref, w2_ref, ys_ref, w13_ref, w2b_ref, z_ref):
    i = pl.program_id(0)
    de = w1_ref.shape[3]
    prev = te_ref[jnp.maximum(i - 1, 0)]

    @pl.when(jnp.logical_or(i == 0, te_ref[i] != prev))
    def _():
        w13_ref[:, 0:de] = w1_ref[0, 0].astype(BF16)
        w13_ref[:, de:2 * de] = w3_ref[0, 0].astype(BF16)
        w2b_ref[...] = w2_ref[0, 0].astype(BF16)

    @pl.when(i >= nu_ref[0])
    def _():
        ys_ref[...] = jnp.zeros_like(ys_ref)

    @pl.when(i < nu_ref[0])
    def _():
        for blk, r0 in enumerate(range(0, xs_ref.shape[0], EXPERT_SUB)):
            z_ref[blk] = _dot(xs_ref[r0:r0 + EXPERT_SUB, :], w13_ref[...])
        for blk, r0 in enumerate(range(0, xs_ref.shape[0], EXPERT_SUB)):
            he = _silu(z_ref[blk, :, 0:de]) * z_ref[blk, :, de:2 * de]
            ys_ref[r0:r0 + EXPERT_SUB, :] = _dot(he.astype(BF16), w2b_ref[...]).astype(BF16)


def _experts(xs, plan, w1, w3, w2, layer):
    n_rows, d = xs.shape
    de = w1.shape[3]
    te, nu = plan["tile_expert"], plan["n_used"]
    tm = EXPERT_TILE
    grid_spec = pltpu.PrefetchScalarGridSpec(
        num_scalar_prefetch=2,
        grid=(n_rows // tm,),
        in_specs=[pl.BlockSpec((tm, d), lambda i, te, nu: (jnp.minimum(i, nu[0] - 1), 0)),
                  pl.BlockSpec((1, 1, d, de), lambda i, te, nu: (layer, te[i], 0, 0)),
                  pl.BlockSpec((1, 1, d, de), lambda i, te, nu: (layer, te[i], 0, 0)),
                  pl.BlockSpec((1, 1, de, d), lambda i, te, nu: (layer, te[i], 0, 0))],
        out_specs=pl.BlockSpec((tm, d), lambda i, te, nu: (i, 0)),
        scratch_shapes=[pltpu.VMEM((d, 2 * de), BF16), pltpu.VMEM((de, d), BF16),
                        pltpu.VMEM((tm // EXPERT_SUB, EXPERT_SUB, 2 * de), F32)])
    return pl.pallas_call(
        _experts_kernel,
        grid_spec=grid_spec,
        out_shape=jax.ShapeDtypeStruct((n_rows, d), BF16),
        compiler_params=_params(("arbitrary",)),
        name="moe_experts",
    )(te, nu, xs, w1, w3, w2)


def _combine_kernel(pc_ref, ls_ref, gs_ref, tot_ref, x_ref, g2_ref, info_ref, ys_ref, o_ref, buf_ref, sem_ref):
    nt = pl.num_programs(1)
    w = pl.program_id(0) * nt + pl.program_id(1)
    nw = pl.num_programs(0) * nt
    slot = w % 2
    rows = buf_ref.shape[1]

    def start_copies(win, sl):
        _window_copies(pc_ref, ls_ref, gs_ref, win,
                       lambda r, n: buf_ref.at[sl, pl.ds(r, n)], lambda r, n: ys_ref.at[pl.ds(r, n)],
                       False, sem_ref.at[sl], lambda c: c.start())

    @pl.when(w == 0)
    def _():
        buf_ref[...] = jnp.zeros_like(buf_ref)
        start_copies(w, slot)

    @pl.when(w + 1 < nw)
    def _():
        start_copies(w + 1, 1 - slot)

    _wait_rows(tot_ref[w], lambda n: pltpu.make_async_copy(
        ys_ref.at[pl.ds(0, n)], buf_ref.at[slot, pl.ds(0, n)], sem_ref.at[slot]))
    pw =_one_hot_rows(info_ref, rows, info_ref[2:3, :], info_ref[3:4, :]).astype(BF16)
    moe = lax.dot_general(pw, buf_ref[slot], (((0,), (0,)), ((), ())), preferred_element_type=F32)
    o_ref[0] = x_ref[0] + g2_ref[0] * moe


def _combine(xa, modtab, info, ys, plan, drop):
    bsz, l, d = xa.shape
    tm = TOKEN_TILE
    nt = l // tm
    grid_spec = pltpu.PrefetchScalarGridSpec(
        num_scalar_prefetch=4,
        grid=(bsz, nt),
        in_specs=[pl.BlockSpec((1, tm, d), lambda b, i, *_: (b, i, 0)),
                  _mod_spec(d, 5),
                  pl.BlockSpec((SUBLANES, tm), lambda b, i, *_: (0, b * nt + i)),
                  pl.BlockSpec(memory_space=pl.ANY)],
        out_specs=pl.BlockSpec((1, tm, d), lambda b, i, *_: (b, jnp.maximum(i - drop, 0), 0)),
        scratch_shapes=[pltpu.VMEM((2, STAGE_ROWS, d), BF16), pltpu.SemaphoreType.DMA((2,))])
    return pl.pallas_call(
        _combine_kernel,
        grid_spec=grid_spec,
        out_shape=jax.ShapeDtypeStruct((bsz, l - drop * tm, d), F32),
        compiler_params=_params(("arbitrary", "arbitrary")),
        name="moe_combine",
    )(plan["pc"], plan["ls"], plan["gs"], plan["tot"], xa, modtab, info, ys)


def _moe_sparse(xa, modtab, h2, info, cnt, w1, w3, w2, layer, drop):
    bsz, l, d = xa.shape
    t = bsz * l
    n_win = t // TOKEN_TILE
    bound = TOP_K * t + n_win * N_EXPERTS * (ROW_ALIGN - 1) + N_EXPERTS * (EXPERT_TILE - ROW_ALIGN)
    n_tiles = -(-bound // EXPERT_TILE)
    plan = _route_plan(cnt[:, :, 0], n_tiles)
    xs = _dispatch(h2.reshape(t, d), info, plan, n_tiles * EXPERT_TILE)
    ys = _experts(xs, plan, w1, w3, w2, layer)
    return _combine(xa, modtab, info, ys, plan, drop)


def _block_diag(w):
    n, k, _ = w.shape
    eye = jnp.eye(n, dtype=w.dtype)
    return jnp.einsum("nij,nm->nimj", w, eye).reshape(n * k, n * k)


def kernel(x, c, ctx, c_ctx, router_w, router_bias, mod_w, mod_b, norm1_g, norm2_g, in_w, in_b,
           conv_dw_w, conv_dw_b, conv_ln_g, conv_ln_b, conv_pw_w, na_q_g, na_k_g, na_rpb, na_out_w,
           rnn_conv_w, rnn_conv_b, rnn_wa, rnn_ba, rnn_wx, rnn_bx, rnn_lam, rnn_out_w, out_w,
           exp_w1, exp_w3, exp_w2):
    bsz, s, d = x.shape
    n_ctx = ctx.shape[1]
    depth = mod_w.shape[0]
    dc = conv_pw_w.shape[1]
    dn = na_out_w.shape[1]
    dr = rnn_out_w.shape[1]
    sizes = (dc, dn, dr, d)
    assert n_ctx % TOKEN_TILE == 0 and s % TOKEN_TILE == 0 and s % GRID_W == 0
    assert dn == N_HEADS * HEAD_DIM and s // GRID_W >= NA_ROWS

    xa = (ctx, x)
    seq = n_ctx + s

    c_rows = -(-(bsz + 1) // SUBLANES) * SUBLANES
    c_all = jnp.zeros((c_rows, d), F32).at[:bsz].set(c).at[bsz].set(c_ctx)
    mods = _mod_vectors(c_all, mod_w, mod_b)

    head_mean = _block_diag(jnp.full((N_HEADS, HEAD_DIM, HEAD_DIM), 1.0 / HEAD_DIM, F32)).astype(BF16)
    rw_hi = router_w.astype(BF16)
    rw_lo = (router_w - rw_hi.astype(F32)).astype(BF16)
    rw = jnp.zeros((2 * d, LANES), BF16)
    rw = rw.at[:d, 0:N_EXPERTS].set(rw_hi).at[:d, N_EXPERTS:2 * N_EXPERTS].set(rw_lo)
    rw = rw.at[d:, 0:N_EXPERTS].set(rw_hi)
    rb = router_bias.reshape(N_EXPERTS, 1)
    tri = jnp.asarray(np.triu(np.ones((TOKEN_TILE, TOKEN_TILE), np.float32), k=1), BF16)
    shift_a, shift_b = _conv_shift_matrices(TOKEN_TILE + 2 * HALO)
    perm, perm_t = _scan_permutation(TOKEN_TILE)

    in_w_b = in_w.astype(BF16)
    pw_b, no_b, ro_b, ow_b = (a.astype(BF16) for a in (conv_pw_w, na_out_w, rnn_out_w, out_w))

    for l in range(depth):
        ml = mods[l].reshape(c_rows, N_MOD, d)
        modtab = jnp.stack([jnp.broadcast_to(ml[bsz], (bsz, N_MOD, d)), ml[:bsz]], axis=1)
        modtab = modtab.reshape(bsz * 2 * N_MOD, 1, d)

        qg = (jnp.tile(na_q_g[l], N_HEADS) * HEAD_DIM ** -0.5).reshape(1, dn)
        kg = jnp.tile(na_k_g[l], N_HEADS).reshape(1, dn)
        u, q, k, v, rx, rg, gt = _inproj(xa, bsz, seq, modtab, norm1_g[l].reshape(1, d), in_w_b, l,
                                         in_b[l].reshape(1, -1), qg, kg, head_mean, sizes)

        cw = jnp.zeros((CONV_K + 1, dc), F32).at[:CONV_K].set(conv_dw_w[l])
        cv = _conv_branch(u, shift_a, shift_b, cw, conv_dw_b[l].reshape(1, dc),
                          conv_ln_g[l].reshape(1, dc), conv_ln_b[l].reshape(1, dc))

        at = _attention(q, k, v, _bias_strips(na_rpb[l]), n_ctx)

        wg = jnp.stack([jnp.concatenate([_block_diag(rnn_wa[l, dn_]), _block_diag(rnn_wx[l, dn_])], axis=1)
                        for dn_ in range(2)]).astype(BF16)
        bg = jnp.concatenate([rnn_ba[l], rnn_bx[l]], axis=-1).reshape(2, 1, 2 * dr)
        nsp = (-LRU_C * jax.nn.softplus(-rnn_lam[l])).reshape(2, 1, dr)
        rcw = jnp.zeros((2, SUBLANES, dr), F32).at[:, :RNN_CONV_K].set(rnn_conv_w[l])
        hf, hb = _rglru(rx, perm, perm_t, rcw, rnn_conv_b[l].reshape(2, 1, dr), wg, bg, nsp)

        xa, h2, info, cnt = _merge(xa, modtab, norm2_g[l].reshape(1, d), cv, at, hf, hb, rg, gt,
                                   pw_b, no_b, ro_b, ow_b, l, rw, rb, tri)

        drop = n_ctx // TOKEN_TILE if l == depth - 1 else 0
        xa = _moe_sparse(xa, modtab, h2, info, cnt, exp_w1, exp_w3, exp_w2, l, drop)

    return xa
```

```python
import functools
import math

import jax
import jax.numpy as jnp
import numpy as np
from jax import lax
from jax.experimental import pallas as pl
from jax.experimental.pallas import tpu as pltpu

F32 = jnp.float32
BF16 = jnp.bfloat16

GRID_W = 64
CONV_K = 31
N_HEADS = 8
HEAD_DIM = 64
NA_ROWS = 8
NA_COLS = 16
RNN_BLOCKS = 8
RNN_CONV_K = 4
LRU_C = 8.0
N_EXPERTS = 16
N_GROUPS = 4
EXPERTS_PER_GROUP = N_EXPERTS // N_GROUPS
N_MOD = 6
EPS = 1e-6
NEG_INF = -1e30

SUBLANES = 8
LANES = 128
MXU_DIM = 256
VMEM_LIMIT = 56 * 1024 * 1024

TOP_K = 2
TOKEN_TILE = 256
HALO = 16
ROW_ALIGN = 16
ROW_SHIFT = 4
EXPERT_TILE = 512
EXPERT_SUB = 256
SEG_SIZES = tuple(TOKEN_TILE >> k for k in range(TOKEN_TILE.bit_length() - ROW_SHIFT))
SEG_CAP = 64
OVER_SIZES = tuple(1 << k for k in range((TOKEN_TILE - SEG_CAP).bit_length() - 1, ROW_SHIFT - 1, -1))
TAIL_SIZES = tuple(1 << k for k in range((EXPERT_TILE + SEG_CAP - ROW_ALIGN).bit_length() - 1, ROW_SHIFT - 1, -1))
STAGE_ROWS = -(-(TOP_K * TOKEN_TILE + N_EXPERTS * (ROW_ALIGN - 1)) // MXU_DIM) * MXU_DIM
WAIT_SIZES = tuple(1 << k for k in range(STAGE_ROWS.bit_length() - 1, ROW_SHIFT - 1, -1))


def _params(sem, vmem=VMEM_LIMIT):
    return pltpu.CompilerParams(dimension_semantics=sem, vmem_limit_bytes=vmem)


def _sigmoid(x):
    return 0.5 * (1.0 + jnp.tanh(0.5 * x))


def _silu(x):
    return x * _sigmoid(x)


def _gelu_tanh(x):
    return x * (0.5 * (1.0 + jnp.tanh(math.sqrt(2.0 / math.pi) * (x + 0.044715 * (x * x * x)))))


def _dot(a, b):
    return jnp.dot(a, b, preferred_element_type=F32)


def _dot_nt(a, b):
    return lax.dot_general(a, b, (((1,), (1,)), ((), ())), preferred_element_type=F32)


def _const_spec(shape):
    return pl.BlockSpec(shape, lambda *_: (0,) * len(shape))


def _layer_spec(arr, layer):
    return pl.BlockSpec((1,) + arr.shape[1:], lambda *_: (layer,) + (0,) * (arr.ndim - 1))


def _mod_kernel(c_ref, w_ref, b_ref, o_ref):
    a = _silu(c_ref[...])
    o_ref[0] = jnp.dot(a, w_ref[0], preferred_element_type=F32,
                       precision=lax.Precision.HIGHEST) + b_ref[0]


def _mod_vectors(c_all, mod_w, mod_b):
    depth, d, n = mod_w.shape
    tn = 1536
    rows = c_all.shape[0]
    return pl.pallas_call(
        _mod_kernel,
        grid=(depth, n // tn),
        in_specs=[pl.BlockSpec((rows, d), lambda l, j: (0, 0)),
                  pl.BlockSpec((1, d, tn), lambda l, j: (l, 0, j)),
                  pl.BlockSpec((1, 1, tn), lambda l, j: (l, 0, j))],
        out_specs=pl.BlockSpec((1, rows, tn), lambda l, j: (l, 0, j)),
        out_shape=jax.ShapeDtypeStruct((depth, rows, n), F32),
        compiler_params=_params(("arbitrary", "arbitrary")),
        name="mod_vectors",
    )(c_all, mod_w, mod_b.reshape(depth, 1, n))


def _mod_spec(d, which):
    return pl.BlockSpec((1, 1, d),
                        lambda b, i, *_: ((2 * b + jnp.minimum(i, 1)) * N_MOD + which, 0, 0))


def _stream_specs(src, tm):
    if not isinstance(src, tuple):
        return [pl.BlockSpec((1, tm, src.shape[2]), lambda b, i, *_: (b, i, 0))], [src], 0
    ctx, lat = src
    nc = ctx.shape[1] // tm
    d = ctx.shape[2]
    return ([pl.BlockSpec((1, tm, d), lambda b, i, *_: (b, jnp.minimum(i, nc - 1), 0)),
             pl.BlockSpec((1, tm, d), lambda b, i, *_: (b, jnp.maximum(i - nc, 0), 0))], [ctx, lat], nc)


def _stream_tile(x_refs, n_ctx_tiles):
    if len(x_refs) == 1:
        return x_refs[0][0]
    return jnp.where(pl.program_id(1) < n_ctx_tiles, x_refs[0][0], x_refs[1][0])


def _inproj_kernel(sizes, n_src, n_ctx_tiles, *refs):
    x_refs = refs[:n_src]
    (sh_ref, sc_ref, g_ref, w_ref, b_ref, qg_ref, kg_ref, bd_ref,
     u_ref, q_ref, k_ref, v_ref, rx_ref, rg_ref, gt_ref, z_ref) = refs[n_src:]
    dc, dn, dr, d = sizes
    x = _stream_tile(x_refs, n_ctx_tiles)
    h = x * lax.rsqrt(jnp.mean(x * x, axis=-1, keepdims=True) + EPS) * g_ref[...]
    h = h * (1.0 + sc_ref[0]) + sh_ref[0]
    hb = h.astype(BF16)

    d_in = w_ref.shape[2]
    for c0 in range(0, d_in, dn):
        z_ref[:, c0:c0 + dn] = _dot(hb, w_ref[0, :, c0:c0 + dn]) + b_ref[:, c0:c0 + dn]

    def proj(off, n):
        return z_ref[:, off:off + n]

    def head_norm(t, gain_ref):
        ms = _dot((t * t).astype(BF16), bd_ref[...])
        return t * lax.rsqrt(ms + EPS) * gain_ref[...]

    off = 0
    a = proj(off, dc); off += dc
    ga = proj(off, dc); off += dc
    u_ref[0] = (a * _sigmoid(ga)).astype(BF16)
    q_ref[0] = head_norm(proj(off, dn), qg_ref).astype(BF16); off += dn
    k_ref[0] = head_norm(proj(off, dn), kg_ref).astype(BF16); off += dn
    v_ref[0] = proj(off, dn).astype(BF16); off += dn
    rx_ref[0] = proj(off, dr).astype(BF16); off += dr
    rg_ref[0] = _gelu_tanh(proj(off, dr)).astype(BF16); off += dr
    for j in range(3):
        gt_ref[0, :, j * d:(j + 1) * d] = _sigmoid(proj(off, d)).astype(BF16); off += d


def _inproj(src, bsz, l, modtab, g, w, layer, b, qg, kg, bd, sizes):
    dc, dn, dr, d = sizes
    d_in = w.shape[2]
    tm = TOKEN_TILE
    x_specs, x_args, nc = _stream_specs(src, tm)
    tile = lambda n: pl.BlockSpec((1, tm, n), lambda b, i: (b, i, 0))
    out = lambda n: jax.ShapeDtypeStruct((bsz, l, n), BF16)
    return pl.pallas_call(
        functools.partial(_inproj_kernel, sizes, len(x_args), nc),
        grid=(bsz, l // tm),
        in_specs=x_specs + [_mod_spec(d, 0), _mod_spec(d, 1), _const_spec((1, d)),
                            pl.BlockSpec((1, d, d_in), lambda b, i: (layer, 0, 0), pipeline_mode=pl.Buffered(1)),
                            _const_spec((1, d_in)), _const_spec((1, dn)), _const_spec((1, dn)),
                            _const_spec((dn, dn))],
        out_specs=[tile(dc), tile(dn), tile(dn), tile(dn), tile(dr), tile(dr), tile(3 * d)],
        out_shape=[out(dc), out(dn), out(dn), out(dn), out(dr), out(dr), out(3 * d)],
        scratch_shapes=[pltpu.VMEM((tm, d_in), F32)],
        compiler_params=_params(("arbitrary", "arbitrary")),
        name="inproj",
    )(*x_args, modtab, modtab, g, w, b, qg, kg, bd)


CONV_SPLIT = 128


def _conv_shift_matrices(rows):
    ka = CONV_SPLIT + HALO
    nb = rows - CONV_SPLIT
    sa = np.zeros(((SUBLANES - 1) * CONV_SPLIT, ka), np.float32)
    sb = np.zeros(((SUBLANES - 1) * nb, nb), np.float32)
    for s in range(1, SUBLANES):
        for r in range(CONV_SPLIT):
            sa[(s - 1) * CONV_SPLIT + r, r + s] = 1.0
        for r in range(nb - s):
            sb[(s - 1) * nb + r, r + s] = 1.0
    return jnp.asarray(sa, BF16), jnp.asarray(sb, BF16)


def _conv_kernel(u_ref, prev_ref, next_ref, sa_ref, sb_ref, w_ref, b_ref, lg_ref, lb_ref, o_ref,
                 sh_ref, y_ref):
    i = pl.program_id(1)
    n = pl.num_programs(1)
    tm = u_ref.shape[1]
    dc = u_ref.shape[2]
    rows = tm + 2 * HALO
    ka = sa_ref.shape[1]
    nb = rows - CONV_SPLIT
    prev_ok = i >= 2
    next_ok = jnp.logical_and(i >= 1, i < n - 1)
    zero = jnp.zeros((HALO, dc), BF16)
    win = jnp.concatenate([jnp.where(prev_ok, prev_ref[0], zero), u_ref[0],
                           jnp.where(next_ok, next_ref[0], zero)], axis=0)
    sh_ref[0] = win.astype(F32)
    rc = 64
    base = HALO - CONV_K // 2
    for c0 in range(0, dc, MXU_DIM):
        hs = slice(c0, c0 + MXU_DIM)
        sha = _dot(sa_ref[...], win[0:ka, hs])
        shb = _dot(sb_ref[...], win[CONV_SPLIT:rows, hs])
        for s in range(1, SUBLANES):
            sh_ref[s, 0:CONV_SPLIT, hs] = sha[(s - 1) * CONV_SPLIT:s * CONV_SPLIT]
            sh_ref[s, CONV_SPLIT:rows, hs] = shb[(s - 1) * nb:s * nb]
    for cb in range(dc // LANES):
        cs = slice(cb * LANES, (cb + 1) * LANES)
        for r0 in range(0, tm, rc):
            acc = jnp.zeros((rc, LANES), F32)
            for k in range(CONV_K):
                o = base + k
                s, a = o % SUBLANES, o // SUBLANES
                acc = acc + sh_ref[s, r0 + a * SUBLANES:r0 + a * SUBLANES + rc, cs] * w_ref[k:k + 1, cs]
            y_ref[r0:r0 + rc, cs] = acc
    y = y_ref[...] + b_ref[...]
    mu = jnp.mean(y, axis=-1, keepdims=True)
    yc = y - mu
    var = jnp.mean(yc * yc, axis=-1, keepdims=True)
    z = yc * lax.rsqrt(var + EPS) * lg_ref[...] + lb_ref[...]
    o_ref[0] = _silu(z).astype(BF16)


def _conv_branch(u, sa, sb, w, b, lg, lb):
    bsz, l, dc = u.shape
    tm = TOKEN_TILE
    hb = tm // HALO
    nh = l // HALO
    return pl.pallas_call(
        _conv_kernel,
        grid=(bsz, l // tm),
        in_specs=[pl.BlockSpec((1, tm, dc), lambda b, i: (b, i, 0)),
                  pl.BlockSpec((1, HALO, dc), lambda b, i: (b, jnp.maximum(i * hb - 1, 0), 0)),
                  pl.BlockSpec((1, HALO, dc), lambda b, i: (b, jnp.minimum((i + 1) * hb, nh - 1), 0)),
                  _const_spec(sa.shape), _const_spec(sb.shape),
                  _const_spec(w.shape), _const_spec((1, dc)), _const_spec((1, dc)), _const_spec((1, dc))],
        out_specs=pl.BlockSpec((1, tm, dc), lambda b, i: (b, i, 0)),
        out_shape=jax.ShapeDtypeStruct((bsz, l, dc), BF16),
        scratch_shapes=[pltpu.VMEM((SUBLANES, tm + 2 * HALO, dc), F32), pltpu.VMEM((tm, dc), F32)],
        compiler_params=_params(("arbitrary", "arbitrary")),
        name="conv_branch",
    )(u, u, u, sa, sb, w, b, lg, lb)


def _softmax_pv(parts):
    m = None
    for s, _ in parts:
        mi = jnp.max(s, axis=-1, keepdims=True)
        m = mi if m is None else jnp.maximum(m, mi)
    colsum = None
    acc = None
    for s, v in parts:
        p = jnp.exp((s - m).astype(BF16))
        for c0 in range(0, p.shape[1], LANES):
            blk = p[:, c0:c0 + LANES]
            colsum = blk if colsum is None else colsum + blk
        pv = _dot(p, v)
        acc = pv if acc is None else acc + pv
    den = jnp.sum(colsum.astype(F32), axis=-1, keepdims=True)
    return acc / den


def _attn_kernel(n_ctx, q_ref, k_ref, v_ref, strip_ref, o_ref, s_ref):
    j = pl.program_id(1)
    tq = q_ref.shape[1]
    rows_per_step = tq // GRID_W
    n_rows = (k_ref.shape[1] - n_ctx) // GRID_W
    n_loc = NA_ROWS * GRID_W
    pairs = N_HEADS // 2

    def split_heads(q2):
        lane = lax.broadcasted_iota(jnp.int32, q2.shape, 1)
        zero = jnp.zeros_like(q2)
        return jnp.concatenate([jnp.where(lane < HEAD_DIM, q2, zero),
                                jnp.where(lane >= HEAD_DIM, q2, zero)], axis=0)

    def merge_heads(o, m):
        lane = lax.broadcasted_iota(jnp.int32, (m, LANES), 1)
        return jnp.where(lane < HEAD_DIM, o[0:m], o[m:2 * m])

    @pl.when(j == 0)
    def _():
        for hp in range(pairs):
            cs = slice(hp * LANES, (hp + 1) * LANES)
            qs = split_heads(q_ref[0, :, cs])
            s = _dot_nt(qs, k_ref[0, 0:n_ctx, cs])
            o = _softmax_pv([(s, v_ref[0, 0:n_ctx, cs])])
            o_ref[0, :, cs] = merge_heads(o, tq).astype(BF16)

    @pl.when(j > 0)
    def _():
        rows2 = 2 * GRID_W
        starts = []
        for rr in range(rows_per_step):
            r = (j - 1) * rows_per_step + rr
            rs = jnp.clip(r - NA_ROWS // 2, 0, n_rows - NA_ROWS)
            starts.append((rs - r + (NA_ROWS - 1), pl.multiple_of(n_ctx + rs * GRID_W, GRID_W)))
        for hp in range(pairs):
            cs = slice(hp * LANES, (hp + 1) * LANES)
            qs_all = jnp.concatenate([split_heads(q_ref[0, rr * GRID_W:(rr + 1) * GRID_W, cs])
                                      for rr in range(rows_per_step)], axis=0)
            s_ctx = _dot_nt(qs_all, k_ref[0, 0:n_ctx, cs])
            for rr in range(rows_per_step):
                case, kstart = starts[rr]
                slot = rr * pairs + hp
                bias = jnp.concatenate([strip_ref[2 * hp, case], strip_ref[2 * hp + 1, case]], axis=0)
                s_ref[slot, :, 0:n_loc] = _dot_nt(qs_all[rr * rows2:(rr + 1) * rows2],
                                                  k_ref[0, pl.ds(kstart, n_loc), cs]) + bias
                s_ref[slot, :, n_loc:n_loc + n_ctx] = s_ctx[rr * rows2:(rr + 1) * rows2]
        for rr in range(rows_per_step):
            _, kstart = starts[rr]
            rsl = slice(rr * GRID_W, (rr + 1) * GRID_W)
            for hp in range(pairs):
                cs = slice(hp * LANES, (hp + 1) * LANES)
                slot = rr * pairs + hp
                o = _softmax_pv([(s_ref[slot, :, 0:n_loc], v_ref[0, pl.ds(kstart, n_loc), cs]),
                                 (s_ref[slot, :, n_loc:n_loc + n_ctx], v_ref[0, 0:n_ctx, cs])])
                o_ref[0, rsl, cs] = merge_heads(o, GRID_W).astype(BF16)


def _attention(q, k, v, strips, n_ctx):
    bsz, l, dn = q.shape
    tq = TOKEN_TILE
    whole = lambda: pl.BlockSpec((1, l, dn), lambda b, j: (b, 0, 0), pipeline_mode=pl.Buffered(1))
    return pl.pallas_call(
        functools.partial(_attn_kernel, n_ctx),
        grid=(bsz, l // tq),
        in_specs=[pl.BlockSpec((1, tq, dn), lambda b, j: (b, j, 0)), whole(), whole(),
                  pl.BlockSpec(strips.shape, lambda b, j: (0, 0, 0, 0), pipeline_mode=pl.Buffered(1))],
        out_specs=pl.BlockSpec((1, tq, dn), lambda b, j: (b, j, 0)),
        out_shape=jax.ShapeDtypeStruct((bsz, l, dn), BF16),
        scratch_shapes=[pltpu.VMEM(((tq // GRID_W) * (N_HEADS // 2), 2 * GRID_W, NA_ROWS * GRID_W + n_ctx), F32)],
        compiler_params=_params(("arbitrary", "arbitrary")),
        name="attention",
    )(q, k, v, strips)


def _bias_strips(rpb):
    col = np.arange(GRID_W)
    start = np.clip(col - NA_COLS // 2, 0, GRID_W - NA_COLS)
    mask = (col[None, :] >= start[:, None]) & (col[None, :] < start[:, None] + NA_COLS)
    dc_idx = np.clip(col[None, :] - col[:, None] + NA_COLS - 1, 0, 2 * NA_COLS - 2)
    pick = jnp.asarray(dc_idx[None] == np.arange(2 * NA_COLS - 1)[:, None, None], F32)
    bias = jnp.einsum("hrc,cqk->hrqk", rpb.astype(F32), pick, precision=lax.Precision.HIGHEST)
    bias = jnp.where(jnp.asarray(mask)[None, None], bias, NEG_INF)
    cases = [jnp.concatenate([bias[:, c + j] for j in range(NA_ROWS)], axis=-1) for c in range(NA_ROWS)]
    return jnp.stack(cases, axis=1)


def _scan_permutation(tc):
    p = np.arange(tc)
    t = (p % SUBLANES) * (tc // SUBLANES) + p // SUBLANES
    m = np.zeros((tc, tc), np.float32)
    m[p, t] = 1.0
    return jnp.asarray(m, BF16), jnp.asarray(m.T, BF16)


def _rglru_kernel(xf_ref, xb_ref, p_ref, pt_ref, cw_ref, cb_ref, wg_ref, bg_ref, nsp_ref, of_ref, ob_ref,
                  edge_ref, h_ref):
    j = pl.program_id(1)

    @pl.when(j == 0)
    def _():
        h_ref[...] = jnp.zeros_like(h_ref)

    @pl.when(j <= 1)
    def _():
        edge_ref[...] = jnp.zeros_like(edge_ref)

    chains = [_rglru_chunk(dirn == 1, x_ref[0], p_ref, pt_ref, cw_ref.at[dirn], cb_ref.at[dirn],
                           wg_ref.at[dirn], bg_ref.at[dirn], nsp_ref.at[dirn],
                           edge_ref.at[dirn], h_ref.at[dirn])
              for dirn, x_ref in enumerate((xf_ref, xb_ref))]
    outs = [None, None]
    for _ in range(RGLRU_PHASES):
        outs = [next(chain) for chain in chains]
    of_ref[0] = outs[0]
    ob_ref[0] = outs[1]


RGLRU_PHASES = 6


def _rglru_chunk(reverse, x, p_ref, pt_ref, cw_ref, cb_ref, wg_ref, bg_ref, nsp_ref, edge_ref, h_ref):
    tc, dr = x.shape
    sl = SUBLANES
    nv = tc // sl
    reach = (RNN_CONV_K - 1) * sl
    xp = _dot(p_ref[...], x)
    yield None
    sub = lax.broadcasted_iota(jnp.int32, (sl, dr), 0)
    prev = edge_ref[...]
    if not reverse:
        edge = xp[tc - reach:tc]
        fixed = jnp.concatenate(
            [jnp.where(sub == 0, pltpu.roll(prev[v * sl:(v + 1) * sl], 1, 0),
                       pltpu.roll(edge[v * sl:(v + 1) * sl], 1, 0)) for v in range(RNN_CONV_K - 1)], axis=0)
        shifted = [xp] + [jnp.concatenate([fixed[reach - d * sl:reach], xp[0:tc - d * sl]], axis=0)
                          for d in range(1, RNN_CONV_K)]
    else:
        edge = xp[0:reach]
        fixed = jnp.concatenate(
            [jnp.where(sub == sl - 1, pltpu.roll(prev[v * sl:(v + 1) * sl], sl - 1, 0),
                       pltpu.roll(edge[v * sl:(v + 1) * sl], sl - 1, 0)) for v in range(RNN_CONV_K - 1)], axis=0)
        shifted = [xp] + [jnp.concatenate([xp[d * sl:tc], fixed[0:d * sl]], axis=0)
                          for d in range(1, RNN_CONV_K)]
    edge_ref[...] = edge
    u = jnp.zeros((tc, dr), F32) + cb_ref[...]
    for kk in range(RNN_CONV_K):
        u = u + shifted[RNN_CONV_K - 1 - kk] * cw_ref[kk:kk + 1, :]
    yield None

    z = _dot(u.astype(BF16), wg_ref[...]) + bg_ref[...]
    yield None
    r = _sigmoid(z[:, 0:dr])
    ig = _sigmoid(z[:, dr:2 * dr])
    a = jnp.exp(r * nsp_ref[...])
    bx = jnp.sqrt(1.0 - a * a) * (ig * u)
    yield None

    order = list(range(nv)) if not reverse else list(range(nv - 1, -1, -1))
    hs, prods = {}, {}
    h = None
    prod = None
    for v in order:
        av = a[v * sl:(v + 1) * sl]
        bv = bx[v * sl:(v + 1) * sl]
        h = bv if h is None else av * h + bv
        prod = av if prod is None else av * prod
        hs[v], prods[v] = h, prod
    h_end, p_end = hs[order[-1]], prods[order[-1]]
    c = h_ref[...]
    carry = {}
    for s in (range(sl) if not reverse else range(sl - 1, -1, -1)):
        carry[s] = c
        c = p_end[s:s + 1] * c + h_end[s:s + 1]
    h_ref[...] = c
    cin = jnp.concatenate([carry[s] for s in range(sl)], axis=0)
    out = jnp.concatenate([hs[v] + prods[v] * cin for v in range(nv)], axis=0)
    yield None
    yield _dot(pt_ref[...], out.astype(BF16)).astype(BF16)


def _rglru(rx, perm, perm_t, cw, cb, wg, bg, nsp):
    bsz, l, dr = rx.shape
    tc = TOKEN_TILE
    n = l // tc
    fwd = pl.BlockSpec((1, tc, dr), lambda b, j: (b, j, 0))
    bwd = pl.BlockSpec((1, tc, dr), lambda b, j: (b, jnp.where(j == 0, 0, n - j), 0))
    out = jax.ShapeDtypeStruct((bsz, l, dr), BF16)
    return pl.pallas_call(
        _rglru_kernel,
        grid=(bsz, n),
        in_specs=[fwd, bwd, _const_spec(perm.shape), _const_spec(perm_t.shape), _const_spec(cw.shape),
                  _const_spec(cb.shape), _const_spec(wg.shape), _const_spec(bg.shape), _const_spec(nsp.shape)],
        out_specs=[fwd, bwd],
        out_shape=[out, out],
        scratch_shapes=[pltpu.VMEM((2, (RNN_CONV_K - 1) * SUBLANES, dr), F32), pltpu.VMEM((2, 1, dr), F32)],
        compiler_params=_params(("arbitrary", "arbitrary")),
        name="rglru",
    )(rx, rx, perm, perm_t, cw, cb, wg, bg, nsp)


def _route(h, rw_ref, rb_ref, tri_ref):
    tm = h.shape[0]
    hi = h.astype(BF16)
    lo = (h - hi.astype(F32)).astype(BF16)
    d = h.shape[1]
    parts = _dot(hi, rw_ref[0:d, :]) + _dot(lo, rw_ref[d:2 * d, :])
    parts_t = parts.T
    logits = parts_t[0:N_EXPERTS] + parts_t[N_EXPERTS:2 * N_EXPERTS]
    scores = _sigmoid(logits)
    sel = scores + rb_ref[...]
    row = lambda t, e: t[e:e + 1, :]
    epg = EXPERTS_PER_GROUP
    gscore = []
    for g in range(N_GROUPS):
        best = None
        for a in range(epg):
            for b in range(a + 1, epg):
                pair = row(sel, g * epg + a) + row(sel, g * epg + b)
                best = pair if best is None else jnp.maximum(best, pair)
        gscore.append(best)
    gbest = gscore[0]
    gidx = jnp.zeros_like(gbest, dtype=jnp.int32)
    for g in range(1, N_GROUPS):
        better = gscore[g] > gbest
        gidx = jnp.where(better, g, gidx)
        gbest = jnp.where(better, gscore[g], gbest)
    picked = []
    flags = []
    for e in range(N_EXPERTS):
        g = e // epg
        rank = jnp.zeros_like(gidx)
        for o in range(g * epg, (g + 1) * epg):
            if o == e:
                continue
            ahead = (row(sel, o) > row(sel, e)) if o > e else (row(sel, o) >= row(sel, e))
            rank = rank + ahead.astype(jnp.int32)
        chosen = jnp.logical_and(gidx == g, rank < 2)
        picked.append(jnp.where(chosen, row(scores, e), 0.0))
        flags.append(jnp.where(chosen, 1.0, 0.0))
    den = picked[0]
    for e in range(1, N_EXPERTS):
        den = den + picked[e]
    gate = jnp.concatenate(picked, axis=0) / den
    flag = jnp.concatenate(flags, axis=0)

    cnt = jnp.sum(flag, axis=1, keepdims=True).astype(jnp.int32)
    pc = jnp.left_shift(jnp.right_shift(cnt + (ROW_ALIGN - 1), ROW_SHIFT), ROW_SHIFT)
    starts = [jnp.zeros((1, 1), jnp.int32)]
    for e in range(1, N_EXPERTS):
        starts.append(starts[-1] + pc[e - 1:e, :])
    seg0 = jnp.concatenate(starts, axis=0).astype(F32)
    pos = seg0 + _dot(flag.astype(BF16), tri_ref[...])
    used = flag > 0.5
    pos_lo = jnp.min(jnp.where(used, pos, 1e9), axis=0, keepdims=True)
    pos_hi = jnp.max(jnp.where(used, pos, -1.0), axis=0, keepdims=True)
    w_lo = jnp.sum(jnp.where(pos == pos_lo, gate, 0.0), axis=0, keepdims=True)
    w_hi = jnp.sum(jnp.where(pos == pos_hi, gate, 0.0), axis=0, keepdims=True)
    info = jnp.concatenate([pos_lo, pos_hi, w_lo, w_hi, jnp.zeros((4, tm), F32)], axis=0)
    return hi, info, jnp.broadcast_to(cnt, (N_EXPERTS, LANES))


MERGE_TILES = 2


def _merge_kernel(n_src, n_ctx_tiles, tiles_per_batch, *refs):
    n_in = n_src + 9
    shared = refs[MERGE_TILES * n_in:MERGE_TILES * n_in + 8]
    n2_ref, pw_ref, no_ref, ro_ref, ow_ref, rw_ref, rb_ref, tri_ref = shared
    o_ref, h2_ref, info_ref, cnt_ref = refs[MERGE_TILES * n_in + 8:]
    d = o_ref.shape[1]
    tm = o_ref.shape[0] // MERGE_TILES
    chains = []
    for h in range(MERGE_TILES):
        tile_refs = refs[h * n_in:(h + 1) * n_in]
        chains.append((h, tile_refs[:n_src]) + tuple(tile_refs[n_src:]))
    branch = []
    for h, x_refs, g1_ref, sh2_ref, sc2_ref, cv_ref, at_ref, hf_ref, hb_ref, rg_ref, gt_ref in chains:
        hsum = hf_ref[0].astype(F32) + hb_ref[0].astype(F32)
        branch.append((_dot(cv_ref[0], pw_ref[0]), _dot(at_ref[0], no_ref[0]),
                       _dot((rg_ref[0].astype(F32) * hsum).astype(BF16), ro_ref[0])))
    ys = []
    for (conv, na, rnn), chain in zip(branch, chains):
        gt_ref = chain[-1]
        m = (gt_ref[0, :, 0:d].astype(F32) * conv + gt_ref[0, :, d:2 * d].astype(F32) * na
             + gt_ref[0, :, 2 * d:3 * d].astype(F32) * rnn)
        ys.append(_dot(m.astype(BF16), ow_ref[0]))
    hms = []
    for y, (h, x_refs, g1_ref, sh2_ref, sc2_ref, *_) in zip(ys, chains):
        if n_src == 1:
            x_in = x_refs[0][0]
        else:
            tile = lax.rem(pl.program_id(0) * MERGE_TILES + h, tiles_per_batch)
            x_in = jnp.where(tile < n_ctx_tiles, x_refs[0][0], x_refs[1][0])
        x = x_in + g1_ref[0] * y
        o_ref[h * tm:(h + 1) * tm, :] = x
        hm = x * lax.rsqrt(jnp.mean(x * x, axis=-1, keepdims=True) + EPS) * n2_ref[...]
        hms.append(hm * (1.0 + sc2_ref[0]) + sh2_ref[0])
    for h, hm in enumerate(hms):
        h2, info, cnt = _route(hm, rw_ref, rb_ref, tri_ref)
        h2_ref[h * tm:(h + 1) * tm, :] = h2
        info_ref[:, h * tm:(h + 1) * tm] = info
        cnt_ref[h] = cnt


def _merge(src, modtab, n2, cv, at, hf, hb, rg, gt, pw, no, ro, ow, layer, rw, rb, tri):
    bsz, l, _ = cv.shape
    d = ow.shape[2]
    tm = TOKEN_TILE
    nt = l // tm
    assert (bsz * nt) % MERGE_TILES == 0
    is_pair = isinstance(src, tuple)
    nc = src[0].shape[1] // tm if is_pair else 0
    in_specs, args = [], []
    for h in range(MERGE_TILES):
        def where(k, h=h):
            t = k * MERGE_TILES + h
            return t // nt, lax.rem(t, nt), t

        def tile(n, where=where):
            return pl.BlockSpec((1, tm, n), lambda k: where(k)[:2] + (0,))

        def mod(which, where=where):
            def index(k):
                b, i, _ = where(k)
                return ((2 * b + jnp.minimum(i, 1)) * N_MOD + which, 0, 0)
            return pl.BlockSpec((1, 1, d), index)

        if is_pair:
            in_specs += [pl.BlockSpec((1, tm, d), lambda k, where=where: (where(k)[0], jnp.minimum(where(k)[1], nc - 1), 0)),
                         pl.BlockSpec((1, tm, d), lambda k, where=where: (where(k)[0], jnp.maximum(where(k)[1] - nc, 0), 0))]
            args += list(src)
        else:
            in_specs.append(tile(d))
            args.append(src)
        in_specs += [mod(2), mod(3), mod(4), tile(cv.shape[2]), tile(at.shape[2]), tile(hf.shape[2]),
                     tile(hb.shape[2]), tile(rg.shape[2]), tile(3 * d)]
        args += [modtab, modtab, modtab, cv, at, hf, hb, rg, gt]
    in_specs += [_const_spec((1, d)), _layer_spec(pw, layer), _layer_spec(no, layer), _layer_spec(ro, layer),
                 _layer_spec(ow, layer), _const_spec(rw.shape), _const_spec(rb.shape), _const_spec(tri.shape)]
    args += [n2, pw, no, ro, ow, rw, rb, tri]
    rows = MERGE_TILES * tm
    xa, h2, info, cnt = pl.pallas_call(
        functools.partial(_merge_kernel, 2 if is_pair else 1, nc, nt),
        grid=(bsz * nt // MERGE_TILES,),
        in_specs=in_specs,
        out_specs=[pl.BlockSpec((rows, d), lambda k: (k, 0)), pl.BlockSpec((rows, d), lambda k: (k, 0)),
                   pl.BlockSpec((SUBLANES, rows), lambda k: (0, k)),
                   pl.BlockSpec((MERGE_TILES, N_EXPERTS, LANES), lambda k: (k, 0, 0))],
        out_shape=[jax.ShapeDtypeStruct((bsz * l, d), F32), jax.ShapeDtypeStruct((bsz * l, d), BF16),
                   jax.ShapeDtypeStruct((SUBLANES, bsz * l), F32),
                   jax.ShapeDtypeStruct((bsz * nt, N_EXPERTS, LANES), jnp.int32)],
        compiler_params=_params(("arbitrary",)),
        name="merge_route",
    )(*args)
    return xa.reshape(bsz, l, d), h2.reshape(bsz, l, d), info, cnt


def _route_plan(cnt, n_tiles):
    pc = (cnt + (ROW_ALIGN - 1)) // ROW_ALIGN * ROW_ALIGN
    ls = jnp.cumsum(pc, axis=1) - pc
    seg_tot = jnp.sum(pc, axis=0)
    reg = (seg_tot + SEG_CAP + (EXPERT_TILE - 1)) // EXPERT_TILE * EXPERT_TILE
    reg_end = jnp.cumsum(reg)
    base = reg_end - reg
    gs = base[None, :] + jnp.cumsum(pc, axis=0) - pc
    n_used = reg_end[-1] // EXPERT_TILE
    tiles = jnp.minimum(jnp.arange(n_tiles, dtype=jnp.int32), n_used - 1)
    tile_expert = jnp.minimum(jnp.sum(tiles[:, None] >= (reg_end // EXPERT_TILE)[None, :], axis=1),
                              N_EXPERTS - 1).astype(jnp.int32)
    i32 = lambda a: a.astype(jnp.int32).reshape(-1)
    return dict(pc=i32(pc), ls=i32(ls), gs=i32(gs), tot=i32(jnp.sum(pc, axis=1)),
                sent=i32(jnp.sum(jnp.maximum(pc, SEG_CAP), axis=1)),
                tail_start=i32(base + seg_tot), tail_len=i32(reg - seg_tot),
                tile_expert=tile_expert, n_used=i32(n_used))


def _for_each_piece(n, sizes, fn):
    off = jnp.int32(0)
    for size in sizes:
        hit = (n & size) != 0

        @pl.when(hit)
        def _(off=off, size=size):
            fn(pl.multiple_of(off, ROW_ALIGN), size)

        off = off + jnp.where(hit, size, 0)


def _window_copies(pc_ref, ls_ref, gs_ref, win, staged_at, sorted_at, to_sorted, sem, fn):
    for e in range(N_EXPERTS):
        n = pc_ref[win * N_EXPERTS + e]
        src0 = ls_ref[win * N_EXPERTS + e]
        dst0 = gs_ref[win * N_EXPERTS + e]

        def piece(off, size, src0=src0, dst0=dst0):
            a = staged_at(pl.multiple_of(src0 + off, ROW_ALIGN), size)
            b = sorted_at(pl.multiple_of(dst0 + off, ROW_ALIGN), size)
            fn(pltpu.make_async_copy(a, b, sem) if to_sorted else pltpu.make_async_copy(b, a, sem))

        _for_each_piece(n, SEG_SIZES, piece)


def _wait_rows(n, desc):
    big = WAIT_SIZES[0]
    lax.fori_loop(0, lax.shift_right_logical(n, big.bit_length() - 1),
                  lambda k, c: (desc(big).wait(), c)[1], 0)
    for size in WAIT_SIZES[1:]:
        @pl.when((n & size) != 0)
        def _(size=size):
            desc(size).wait()


def _one_hot_rows(info_ref, rows, lo_val, hi_val):
    tm = info_ref.shape[1]
    rid = lax.broadcasted_iota(jnp.int32, (rows, tm), 0).astype(F32)
    return jnp.where(rid == info_ref[0:1, :], lo_val, jnp.where(rid == info_ref[1:2, :], hi_val, 0.0))


def _dispatch_kernel(pc_ref, ls_ref, gs_ref, sent_ref, ts_ref, tl_ref, nu_ref, h_ref, info_ref, xs_ref,
                     buf_ref, zero_ref, sem_ref, zsem_ref):
    w = pl.program_id(0)
    nw = pl.num_programs(0)
    slot = w % 2
    rows = STAGE_ROWS

    @pl.when(w == 0)
    def _():
        buf_ref[:, rows:, :] = jnp.zeros((2, buf_ref.shape[1] - rows, buf_ref.shape[2]), BF16)

    p = _one_hot_rows(info_ref, rows, 1.0, 1.0).astype(BF16)
    buf_ref[slot, 0:rows, :] = _dot(p, h_ref[...]).astype(BF16)

    def start_copies(win, sl):
        for e in range(N_EXPERTS):
            n = pc_ref[win * N_EXPERTS + e]
            src0 = pl.multiple_of(ls_ref[win * N_EXPERTS + e], ROW_ALIGN)
            dst0 = pl.multiple_of(gs_ref[win * N_EXPERTS + e], ROW_ALIGN)

            def piece(off, size, src0=src0, dst0=dst0):
                pltpu.make_async_copy(buf_ref.at[sl, pl.ds(pl.multiple_of(src0 + off, ROW_ALIGN), size)],
                                      xs_ref.at[pl.ds(pl.multiple_of(dst0 + off, ROW_ALIGN), size)],
                                      sem_ref.at[sl]).start()

            piece(0, SEG_CAP)

            @pl.when(n > SEG_CAP)
            def _(n=n, piece=piece):
                _for_each_piece(n - SEG_CAP, OVER_SIZES, lambda off, size: piece(off + SEG_CAP, size))

    def wait_copies(win, sl):
        _wait_rows(sent_ref[win], lambda n: pltpu.make_async_copy(
            buf_ref.at[sl, pl.ds(0, n)], xs_ref.at[pl.ds(0, n)], sem_ref.at[sl]))

    @pl.when(w > 0)
    def _():
        wait_copies(w - 1, 1 - slot)

    start_copies(w, slot)

    @pl.when(w == nw - 1)
    def _():
        wait_copies(w, slot)
        zero_ref[...] = jnp.zeros_like(zero_ref)

        def tails(fn):
            for e in range(N_EXPERTS):
                start = ts_ref[e]

                def piece(off, size, start=start):
                    fn(pltpu.make_async_copy(zero_ref.at[pl.ds(0, size)],
                                             xs_ref.at[pl.ds(pl.multiple_of(start + off, ROW_ALIGN), size)],
                                             zsem_ref.at[0]))

                _for_each_piece(tl_ref[e], TAIL_SIZES, piece)

        tails(lambda c: c.start())
        tails(lambda c: c.wait())

        fill = zero_ref.shape[0]
        used = nu_ref[0] * EXPERT_TILE

        def spare(k):
            return pltpu.make_async_copy(
                zero_ref, xs_ref.at[pl.ds(pl.multiple_of(used + k * fill, ROW_ALIGN), fill)], zsem_ref.at[0])

        n_fill = (xs_ref.shape[0] - used) // fill
        lax.fori_loop(0, n_fill, lambda k, c: (spare(k).start(), c)[1], 0)
        lax.fori_loop(0, n_fill, lambda k, c: (spare(k).wait(), c)[1], 0)


def _dispatch(h2, info, plan, n_rows):
    t, d = h2.shape
    tm = TOKEN_TILE
    rows = STAGE_ROWS
    grid_spec = pltpu.PrefetchScalarGridSpec(
        num_scalar_prefetch=7,
        grid=(t // tm,),
        in_specs=[pl.BlockSpec((tm, d), lambda w, *_: (w, 0)),
                  pl.BlockSpec((SUBLANES, tm), lambda w, *_: (0, w))],
        out_specs=pl.BlockSpec(memory_space=pl.ANY),
        scratch_shapes=[pltpu.VMEM((2, rows + SEG_CAP, d), BF16), pltpu.VMEM((TAIL_SIZES[0], d), BF16),
                        pltpu.SemaphoreType.DMA((2,)), pltpu.SemaphoreType.DMA((1,))])
    return pl.pallas_call(
        _dispatch_kernel,
        grid_spec=grid_spec,
        out_shape=jax.ShapeDtypeStruct((n_rows, d), BF16),
        compiler_params=_params(("arbitrary",)),
        name="moe_dispatch",
    )(plan["pc"], plan["ls"], plan["gs"], plan["sent"], plan["tail_start"], plan["tail_len"], plan["n_used"],
      h2, info)


def _experts_kernel(te_ref, nu_ref, xs_ref, w1_ref, w3_ref, w2_ref, ys_ref, w13_ref, w2b_ref, z_ref):
    i = pl.program_id(0)
    de = w1_ref.shape[3]
    prev = te_ref[jnp.maximum(i - 1, 0)]

    @pl.when(jnp.logical_or(i == 0, te_ref[i] != prev))
    def _():
        w13_ref[:, 0:de] = w1_ref[0, 0].astype(BF16)
        w13_ref[:, de:2 * de] = w3_ref[0, 0].astype(BF16)
        w2b_ref[...] = w2_ref[0, 0].astype(BF16)

    @pl.when(i >= nu_ref[0])
    def _():
        ys_ref[...] = jnp.zeros_like(ys_ref)

    @pl.when(i < nu_ref[0])
    def _():
        for blk, r0 in enumerate(range(0, xs_ref.shape[0], EXPERT_SUB)):
            z_ref[blk] = _dot(xs_ref[r0:r0 + EXPERT_SUB, :], w13_ref[...])
        for blk, r0 in enumerate(range(0, xs_ref.shape[0], EXPERT_SUB)):
            he = _silu(z_ref[blk, :, 0:de]) * z_ref[blk, :, de:2 * de]
            ys_ref[r0:r0 + EXPERT_SUB, :] = _dot(he.astype(BF16), w2b_ref[...]).astype(BF16)


def _experts(xs, plan, w1, w3, w2, layer):
    n_rows, d = xs.shape
    de = w1.shape[3]
    te, nu = plan["tile_expert"], plan["n_used"]
    tm = EXPERT_TILE
    grid_spec = pltpu.PrefetchScalarGridSpec(
        num_scalar_prefetch=2,
        grid=(n_rows // tm,),
        in_specs=[pl.BlockSpec((tm, d), lambda i, te, nu: (jnp.minimum(i, nu[0] - 1), 0)),
                  pl.BlockSpec((1, 1, d, de), lambda i, te, nu: (layer, te[i], 0, 0)),
                  pl.BlockSpec((1, 1, d, de), lambda i, te, nu: (layer, te[i], 0, 0)),
                  pl.BlockSpec((1, 1, de, d), lambda i, te, nu: (layer, te[i], 0, 0))],
        out_specs=pl.BlockSpec((tm, d), lambda i, te, nu: (i, 0)),
        scratch_shapes=[pltpu.VMEM((d, 2 * de), BF16), pltpu.VMEM((de, d), BF16),
                        pltpu.VMEM((tm // EXPERT_SUB, EXPERT_SUB, 2 * de), F32)])
    return pl.pallas_call(
        _experts_kernel,
        grid_spec=grid_spec,
        out_shape=jax.ShapeDtypeStruct((n_rows, d), BF16),
        compiler_params=_params(("arbitrary",)),
        name="moe_experts",
    )(te, nu, xs, w1, w3, w2)


def _combine_kernel(pc_ref, ls_ref, gs_ref, tot_ref, x_ref, g2_ref, info_ref, ys_ref, o_ref, buf_ref, sem_ref):
    nt = pl.num_programs(1)
    w = pl.program_id(0) * nt + pl.program_id(1)
    nw = pl.num_programs(0) * nt
    slot = w % 2
    rows = buf_ref.shape[1]

    def start_copies(win, sl):
        _window_copies(pc_ref, ls_ref, gs_ref, win,
                       lambda r, n: buf_ref.at[sl, pl.ds(r, n)], lambda r, n: ys_ref.at[pl.ds(r, n)],
                       False, sem_ref.at[sl], lambda c: c.start())

    @pl.when(w == 0)
    def _():
        buf_ref[...] = jnp.zeros_like(buf_ref)
        start_copies(w, slot)

    @pl.when(w + 1 < nw)
    def _():
        start_copies(w + 1, 1 - slot)

    _wait_rows(tot_ref[w], lambda n: pltpu.make_async_copy(
        ys_ref.at[pl.ds(0, n)], buf_ref.at[slot, pl.ds(0, n)], sem_ref.at[slot]))
    pw =_one_hot_rows(info_ref, rows, info_ref[2:3, :], info_ref[3:4, :]).astype(BF16)
    moe = lax.dot_general(pw, buf_ref[slot], (((0,), (0,)), ((), ())), preferred_element_type=F32)
    o_ref[0] = x_ref[0] + g2_ref[0] * moe


def _combine(xa, modtab, info, ys, plan, drop):
    bsz, l, d = xa.shape
    tm = TOKEN_TILE
    nt = l // tm
    grid_spec = pltpu.PrefetchScalarGridSpec(
        num_scalar_prefetch=4,
        grid=(bsz, nt),
        in_specs=[pl.BlockSpec((1, tm, d), lambda b, i, *_: (b, i, 0)),
                  _mod_spec(d, 5),
                  pl.BlockSpec((SUBLANES, tm), lambda b, i, *_: (0, b * nt + i)),
                  pl.BlockSpec(memory_space=pl.ANY)],
        out_specs=pl.BlockSpec((1, tm, d), lambda b, i, *_: (b, jnp.maximum(i - drop, 0), 0)),
        scratch_shapes=[pltpu.VMEM((2, STAGE_ROWS, d), BF16), pltpu.SemaphoreType.DMA((2,))])
    return pl.pallas_call(
        _combine_kernel,
        grid_spec=grid_spec,
        out_shape=jax.ShapeDtypeStruct((bsz, l - drop * tm, d), F32),
        compiler_params=_params(("arbitrary", "arbitrary")),
        name="moe_combine",
    )(plan["pc"], plan["ls"], plan["gs"], plan["tot"], xa, modtab, info, ys)


def _moe_sparse(xa, modtab, h2, info, cnt, w1, w3, w2, layer, drop):
    bsz, l, d = xa.shape
    t = bsz * l
    n_win = t // TOKEN_TILE
    bound = (TOP_K * t + n_win * N_EXPERTS * (ROW_ALIGN - 1)
             + N_EXPERTS * (SEG_CAP + EXPERT_TILE - ROW_ALIGN))
    n_tiles = -(-bound // EXPERT_TILE)
    plan = _route_plan(cnt[:, :, 0], n_tiles)
    xs = _dispatch(h2.reshape(t, d), info, plan, n_tiles * EXPERT_TILE)
    ys = _experts(xs, plan, w1, w3, w2, layer)
    return _combine(xa, modtab, info, ys, plan, drop)


def _block_diag(w):
    n, k, _ = w.shape
    eye = jnp.eye(n, dtype=w.dtype)
    return jnp.einsum("nij,nm->nimj", w, eye).reshape(n * k, n * k)


def kernel(x, c, ctx, c_ctx, router_w, router_bias, mod_w, mod_b, norm1_g, norm2_g, in_w, in_b,
           conv_dw_w, conv_dw_b, conv_ln_g, conv_ln_b, conv_pw_w, na_q_g, na_k_g, na_rpb, na_out_w,
           rnn_conv_w, rnn_conv_b, rnn_wa, rnn_ba, rnn_wx, rnn_bx, rnn_lam, rnn_out_w, out_w,
           exp_w1, exp_w3, exp_w2):
    bsz, s, d = x.shape
    n_ctx = ctx.shape[1]
    depth = mod_w.shape[0]
    dc = conv_pw_w.shape[1]
    dn = na_out_w.shape[1]
    dr = rnn_out_w.shape[1]
    sizes = (dc, dn, dr, d)
    assert n_ctx % TOKEN_TILE == 0 and s % TOKEN_TILE == 0 and s % GRID_W == 0
    assert dn == N_HEADS * HEAD_DIM and s // GRID_W >= NA_ROWS

    xa = (ctx, x)
    seq = n_ctx + s

    c_rows = -(-(bsz + 1) // SUBLANES) * SUBLANES
    c_all = jnp.zeros((c_rows, d), F32).at[:bsz].set(c).at[bsz].set(c_ctx)
    mods = _mod_vectors(c_all, mod_w, mod_b)

    head_mean = _block_diag(jnp.full((N_HEADS, HEAD_DIM, HEAD_DIM), 1.0 / HEAD_DIM, F32)).astype(BF16)
    rw_hi = router_w.astype(BF16)
    rw_lo = (router_w - rw_hi.astype(F32)).astype(BF16)
    rw = jnp.zeros((2 * d, LANES), BF16)
    rw = rw.at[:d, 0:N_EXPERTS].set(rw_hi).at[:d, N_EXPERTS:2 * N_EXPERTS].set(rw_lo)
    rw = rw.at[d:, 0:N_EXPERTS].set(rw_hi)
    rb = router_bias.reshape(N_EXPERTS, 1)
    tri = jnp.asarray(np.triu(np.ones((TOKEN_TILE, TOKEN_TILE), np.float32), k=1), BF16)
    shift_a, shift_b = _conv_shift_matrices(TOKEN_TILE + 2 * HALO)
    perm, perm_t = _scan_permutation(TOKEN_TILE)

    in_w_b = in_w.astype(BF16)
    pw_b, no_b, ro_b, ow_b = (a.astype(BF16) for a in (conv_pw_w, na_out_w, rnn_out_w, out_w))

    for l in range(depth):
        ml = mods[l].reshape(c_rows, N_MOD, d)
        modtab = jnp.stack([jnp.broadcast_to(ml[bsz], (bsz, N_MOD, d)), ml[:bsz]], axis=1)
        modtab = modtab.reshape(bsz * 2 * N_MOD, 1, d)

        qg = (jnp.tile(na_q_g[l], N_HEADS) * HEAD_DIM ** -0.5).reshape(1, dn)
        kg = jnp.tile(na_k_g[l], N_HEADS).reshape(1, dn)
        u, q, k, v, rx, rg, gt = _inproj(xa, bsz, seq, modtab, norm1_g[l].reshape(1, d), in_w_b, l,
                                         in_b[l].reshape(1, -1), qg, kg, head_mean, sizes)

        cw = jnp.zeros((CONV_K + 1, dc), F32).at[:CONV_K].set(conv_dw_w[l])
        cv = _conv_branch(u, shift_a, shift_b, cw, conv_dw_b[l].reshape(1, dc),
                          conv_ln_g[l].reshape(1, dc), conv_ln_b[l].reshape(1, dc))

        at = _attention(q, k, v, _bias_strips(na_rpb[l]), n_ctx)

        wg = jnp.stack([jnp.concatenate([_block_diag(rnn_wa[l, dn_]), _block_diag(rnn_wx[l, dn_])], axis=1)
                        for dn_ in range(2)]).astype(BF16)
        bg = jnp.concatenate([rnn_ba[l], rnn_bx[l]], axis=-1).reshape(2, 1, 2 * dr)
        nsp = (-LRU_C * jax.nn.softplus(-rnn_lam[l])).reshape(2, 1, dr)
        rcw = jnp.zeros((2, SUBLANES, dr), F32).at[:, :RNN_CONV_K].set(rnn_conv_w[l])
        hf, hb = _rglru(rx, perm, perm_t, rcw, rnn_conv_b[l].reshape(2, 1, dr), wg, bg, nsp)

        xa, h2, info, cnt = _merge(xa, modtab, norm2_g[l].reshape(1, d), cv, at, hf, hb, rg, gt,
                                   pw_b, no_b, ro_b, ow_b, l, rw, rb, tri)

        drop = n_ctx // TOKEN_TILE if l == depth - 1 else 0
        xa = _moe_sparse(xa, modtab, h2, info, cnt, exp_w1, exp_w3, exp_w2, l, drop)

    return xa
```

```python
import functools
import math

import jax
import jax.numpy as jnp
import numpy as np
from jax import lax
from jax.experimental import pallas as pl
from jax.experimental.pallas import tpu as pltpu

F32 = jnp.float32
BF16 = jnp.bfloat16

GRID_W = 64
CONV_K = 31
N_HEADS = 8
HEAD_DIM = 64
NA_ROWS = 8
NA_COLS = 16
RNN_BLOCKS = 8
RNN_CONV_K = 4
LRU_C = 8.0
N_EXPERTS = 16
N_GROUPS = 4
EXPERTS_PER_GROUP = N_EXPERTS // N_GROUPS
N_MOD = 6
EPS = 1e-6
NEG_INF = -1e30

SUBLANES = 8
LANES = 128
MXU_DIM = 256
VMEM_LIMIT = 56 * 1024 * 1024

TOP_K = 2
TOKEN_TILE = 256
HALO = 16
ROW_ALIGN = 16
ROW_SHIFT = 4
EXPERT_TILE = 512
EXPERT_SUB = 256
SEG_SIZES = tuple(TOKEN_TILE >> k for k in range(TOKEN_TILE.bit_length() - ROW_SHIFT))
MOE_SLOTS = 3
TAIL_SIZES = tuple((EXPERT_TILE // 2) >> k for k in range(EXPERT_TILE.bit_length() - 1 - ROW_SHIFT))
STAGE_ROWS = -(-(TOP_K * TOKEN_TILE + N_EXPERTS * (ROW_ALIGN - 1)) // MXU_DIM) * MXU_DIM
WAIT_SIZES = tuple(1 << k for k in range(STAGE_ROWS.bit_length() - 1, ROW_SHIFT - 1, -1))


def _params(sem, vmem=VMEM_LIMIT):
    return pltpu.CompilerParams(dimension_semantics=sem, vmem_limit_bytes=vmem)


def _sigmoid(x):
    return 0.5 * (1.0 + jnp.tanh(0.5 * x))


def _silu(x):
    return x * _sigmoid(x)


def _gelu_tanh(x):
    return x * (0.5 * (1.0 + jnp.tanh(math.sqrt(2.0 / math.pi) * (x + 0.044715 * (x * x * x)))))


def _dot(a, b):
    return jnp.dot(a, b, preferred_element_type=F32)


def _dot_nt(a, b):
    return lax.dot_general(a, b, (((1,), (1,)), ((), ())), preferred_element_type=F32)


def _const_spec(shape):
    return pl.BlockSpec(shape, lambda *_: (0,) * len(shape))


def _layer_spec(arr, layer):
    return pl.BlockSpec((1,) + arr.shape[1:], lambda *_: (layer,) + (0,) * (arr.ndim - 1))


def _mod_kernel(c_ref, w_ref, b_ref, o_ref):
    a = _silu(c_ref[...])
    o_ref[0] = jnp.dot(a, w_ref[0], preferred_element_type=F32,
                       precision=lax.Precision.HIGHEST) + b_ref[0]


def _mod_vectors(c_all, mod_w, mod_b):
    depth, d, n = mod_w.shape
    tn = 1536
    rows = c_all.shape[0]
    return pl.pallas_call(
        _mod_kernel,
        grid=(depth, n // tn),
        in_specs=[pl.BlockSpec((rows, d), lambda l, j: (0, 0)),
                  pl.BlockSpec((1, d, tn), lambda l, j: (l, 0, j)),
                  pl.BlockSpec((1, 1, tn), lambda l, j: (l, 0, j))],
        out_specs=pl.BlockSpec((1, rows, tn), lambda l, j: (l, 0, j)),
        out_shape=jax.ShapeDtypeStruct((depth, rows, n), F32),
        compiler_params=_params(("arbitrary", "arbitrary")),
        name="mod_vectors",
    )(c_all, mod_w, mod_b.reshape(depth, 1, n))


def _mod_spec(d, which):
    return pl.BlockSpec((1, 1, d),
                        lambda b, i, *_: ((2 * b + jnp.minimum(i, 1)) * N_MOD + which, 0, 0))


def _stream_specs(src, tm):
    if not isinstance(src, tuple):
        return [pl.BlockSpec((1, tm, src.shape[2]), lambda b, i, *_: (b, i, 0))], [src], 0
    ctx, lat = src
    nc = ctx.shape[1] // tm
    d = ctx.shape[2]
    return ([pl.BlockSpec((1, tm, d), lambda b, i, *_: (b, jnp.minimum(i, nc - 1), 0)),
             pl.BlockSpec((1, tm, d), lambda b, i, *_: (b, jnp.maximum(i - nc, 0), 0))], [ctx, lat], nc)


def _stream_tile(x_refs, n_ctx_tiles):
    if len(x_refs) == 1:
        return x_refs[0][0]
    return jnp.where(pl.program_id(1) < n_ctx_tiles, x_refs[0][0], x_refs[1][0])


def _inproj_kernel(sizes, n_src, n_ctx_tiles, *refs):
    x_refs = refs[:n_src]
    (sh_ref, sc_ref, g_ref, w_ref, b_ref, qg_ref, kg_ref, bd_ref,
     u_ref, q_ref, k_ref, v_ref, rx_ref, rg_ref, gt_ref, z_ref) = refs[n_src:]
    dc, dn, dr, d = sizes
    x = _stream_tile(x_refs, n_ctx_tiles)
    h = x * lax.rsqrt(jnp.mean(x * x, axis=-1, keepdims=True) + EPS) * g_ref[...]
    h = h * (1.0 + sc_ref[0]) + sh_ref[0]
    hb = h.astype(BF16)

    d_in = w_ref.shape[2]
    for c0 in range(0, d_in, dn):
        z_ref[:, c0:c0 + dn] = _dot(hb, w_ref[0, :, c0:c0 + dn]) + b_ref[:, c0:c0 + dn]

    def proj(off, n):
        return z_ref[:, off:off + n]

    def head_norm(t, gain_ref):
        ms = _dot((t * t).astype(BF16), bd_ref[...])
        return t * lax.rsqrt(ms + EPS) * gain_ref[...]

    off = 0
    a = proj(off, dc); off += dc
    ga = proj(off, dc); off += dc
    u_ref[0] = (a * _sigmoid(ga)).astype(BF16)
    q_ref[0] = head_norm(proj(off, dn), qg_ref).astype(BF16); off += dn
    k_ref[0] = head_norm(proj(off, dn), kg_ref).astype(BF16); off += dn
    v_ref[0] = proj(off, dn).astype(BF16); off += dn
    rx_ref[0] = proj(off, dr).astype(BF16); off += dr
    rg_ref[0] = _gelu_tanh(proj(off, dr)).astype(BF16); off += dr
    for j in range(3):
        gt_ref[0, :, j * d:(j + 1) * d] = _sigmoid(proj(off, d)).astype(BF16); off += d


def _inproj(src, bsz, l, modtab, g, w, layer, b, qg, kg, bd, sizes):
    dc, dn, dr, d = sizes
    d_in = w.shape[2]
    tm = TOKEN_TILE
    x_specs, x_args, nc = _stream_specs(src, tm)
    tile = lambda n: pl.BlockSpec((1, tm, n), lambda b, i: (b, i, 0))
    out = lambda n: jax.ShapeDtypeStruct((bsz, l, n), BF16)
    return pl.pallas_call(
        functools.partial(_inproj_kernel, sizes, len(x_args), nc),
        grid=(bsz, l // tm),
        in_specs=x_specs + [_mod_spec(d, 0), _mod_spec(d, 1), _const_spec((1, d)),
                            pl.BlockSpec((1, d, d_in), lambda b, i: (layer, 0, 0), pipeline_mode=pl.Buffered(1)),
                            _const_spec((1, d_in)), _const_spec((1, dn)), _const_spec((1, dn)),
                            _const_spec((dn, dn))],
        out_specs=[tile(dc), tile(dn), tile(dn), tile(dn), tile(dr), tile(dr), tile(3 * d)],
        out_shape=[out(dc), out(dn), out(dn), out(dn), out(dr), out(dr), out(3 * d)],
        scratch_shapes=[pltpu.VMEM((tm, d_in), F32)],
        compiler_params=_params(("arbitrary", "arbitrary")),
        name="inproj",
    )(*x_args, modtab, modtab, g, w, b, qg, kg, bd)


CONV_SPLIT = 128


def _conv_shift_matrices(rows):
    ka = CONV_SPLIT + HALO
    nb = rows - CONV_SPLIT
    sa = np.zeros(((SUBLANES - 1) * CONV_SPLIT, ka), np.float32)
    sb = np.zeros(((SUBLANES - 1) * nb, nb), np.float32)
    for s in range(1, SUBLANES):
        for r in range(CONV_SPLIT):
            sa[(s - 1) * CONV_SPLIT + r, r + s] = 1.0
        for r in range(nb - s):
            sb[(s - 1) * nb + r, r + s] = 1.0
    return jnp.asarray(sa, BF16), jnp.asarray(sb, BF16)


def _conv_kernel(u_ref, prev_ref, next_ref, sa_ref, sb_ref, w_ref, b_ref, lg_ref, lb_ref, o_ref,
                 sh_ref, y_ref):
    i = pl.program_id(1)
    n = pl.num_programs(1)
    tm = u_ref.shape[1]
    dc = u_ref.shape[2]
    rows = tm + 2 * HALO
    ka = sa_ref.shape[1]
    nb = rows - CONV_SPLIT
    prev_ok = i >= 2
    next_ok = jnp.logical_and(i >= 1, i < n - 1)
    zero = jnp.zeros((HALO, dc), BF16)
    win = jnp.concatenate([jnp.where(prev_ok, prev_ref[0], zero), u_ref[0],
                           jnp.where(next_ok, next_ref[0], zero)], axis=0)
    sh_ref[0] = win.astype(F32)
    rc = 64
    base = HALO - CONV_K // 2
    for c0 in range(0, dc, MXU_DIM):
        hs = slice(c0, c0 + MXU_DIM)
        sha = _dot(sa_ref[...], win[0:ka, hs])
        shb = _dot(sb_ref[...], win[CONV_SPLIT:rows, hs])
        for s in range(1, SUBLANES):
            sh_ref[s, 0:CONV_SPLIT, hs] = sha[(s - 1) * CONV_SPLIT:s * CONV_SPLIT]
            sh_ref[s, CONV_SPLIT:rows, hs] = shb[(s - 1) * nb:s * nb]
    for cb in range(dc // LANES):
        cs = slice(cb * LANES, (cb + 1) * LANES)
        for r0 in range(0, tm, rc):
            acc = jnp.zeros((rc, LANES), F32)
            for k in range(CONV_K):
                o = base + k
                s, a = o % SUBLANES, o // SUBLANES
                acc = acc + sh_ref[s, r0 + a * SUBLANES:r0 + a * SUBLANES + rc, cs] * w_ref[k:k + 1, cs]
            y_ref[r0:r0 + rc, cs] = acc
    y = y_ref[...] + b_ref[...]
    mu = jnp.mean(y, axis=-1, keepdims=True)
    yc = y - mu
    var = jnp.mean(yc * yc, axis=-1, keepdims=True)
    z = yc * lax.rsqrt(var + EPS) * lg_ref[...] + lb_ref[...]
    o_ref[0] = _silu(z).astype(BF16)


def _conv_branch(u, sa, sb, w, b, lg, lb):
    bsz, l, dc = u.shape
    tm = TOKEN_TILE
    hb = tm // HALO
    nh = l // HALO
    return pl.pallas_call(
        _conv_kernel,
        grid=(bsz, l // tm),
        in_specs=[pl.BlockSpec((1, tm, dc), lambda b, i: (b, i, 0)),
                  pl.BlockSpec((1, HALO, dc), lambda b, i: (b, jnp.maximum(i * hb - 1, 0), 0)),
                  pl.BlockSpec((1, HALO, dc), lambda b, i: (b, jnp.minimum((i + 1) * hb, nh - 1), 0)),
                  _const_spec(sa.shape), _const_spec(sb.shape),
                  _const_spec(w.shape), _const_spec((1, dc)), _const_spec((1, dc)), _const_spec((1, dc))],
        out_specs=pl.BlockSpec((1, tm, dc), lambda b, i: (b, i, 0)),
        out_shape=jax.ShapeDtypeStruct((bsz, l, dc), BF16),
        scratch_shapes=[pltpu.VMEM((SUBLANES, tm + 2 * HALO, dc), F32), pltpu.VMEM((tm, dc), F32)],
        compiler_params=_params(("arbitrary", "arbitrary")),
        name="conv_branch",
    )(u, u, u, sa, sb, w, b, lg, lb)


def _softmax_pv(parts):
    m = None
    for s, _ in parts:
        mi = jnp.max(s, axis=-1, keepdims=True)
        m = mi if m is None else jnp.maximum(m, mi)
    colsum = None
    acc = None
    for s, v in parts:
        p = jnp.exp((s - m).astype(BF16))
        for c0 in range(0, p.shape[1], LANES):
            blk = p[:, c0:c0 + LANES]
            colsum = blk if colsum is None else colsum + blk
        pv = _dot(p, v)
        acc = pv if acc is None else acc + pv
    den = jnp.sum(colsum.astype(F32), axis=-1, keepdims=True)
    return acc / den


def _attn_kernel(n_ctx, q_ref, k_ref, v_ref, strip_ref, o_ref, s_ref):
    j = pl.program_id(1)
    tq = q_ref.shape[1]
    rows_per_step = tq // GRID_W
    n_rows = (k_ref.shape[1] - n_ctx) // GRID_W
    n_loc = NA_ROWS * GRID_W
    pairs = N_HEADS // 2

    def split_heads(q2):
        lane = lax.broadcasted_iota(jnp.int32, q2.shape, 1)
        zero = jnp.zeros_like(q2)
        return jnp.concatenate([jnp.where(lane < HEAD_DIM, q2, zero),
                                jnp.where(lane >= HEAD_DIM, q2, zero)], axis=0)

    def merge_heads(o, m):
        lane = lax.broadcasted_iota(jnp.int32, (m, LANES), 1)
        return jnp.where(lane < HEAD_DIM, o[0:m], o[m:2 * m])

    @pl.when(j == 0)
    def _():
        for hp in range(pairs):
            cs = slice(hp * LANES, (hp + 1) * LANES)
            qs = split_heads(q_ref[0, :, cs])
            s = _dot_nt(qs, k_ref[0, 0:n_ctx, cs])
            o = _softmax_pv([(s, v_ref[0, 0:n_ctx, cs])])
            o_ref[0, :, cs] = merge_heads(o, tq).astype(BF16)

    @pl.when(j > 0)
    def _():
        rows2 = 2 * GRID_W
        starts = []
        for rr in range(rows_per_step):
            r = (j - 1) * rows_per_step + rr
            rs = jnp.clip(r - NA_ROWS // 2, 0, n_rows - NA_ROWS)
            starts.append((rs - r + (NA_ROWS - 1), pl.multiple_of(n_ctx + rs * GRID_W, GRID_W)))
        for hp in range(pairs):
            cs = slice(hp * LANES, (hp + 1) * LANES)
            qs_all = jnp.concatenate([split_heads(q_ref[0, rr * GRID_W:(rr + 1) * GRID_W, cs])
                                      for rr in range(rows_per_step)], axis=0)
            s_ctx = _dot_nt(qs_all, k_ref[0, 0:n_ctx, cs])
            for rr in range(rows_per_step):
                case, kstart = starts[rr]
                slot = rr * pairs + hp
                bias = jnp.concatenate([strip_ref[2 * hp, case], strip_ref[2 * hp + 1, case]], axis=0)
                s_ref[slot, :, 0:n_loc] = _dot_nt(qs_all[rr * rows2:(rr + 1) * rows2],
                                                  k_ref[0, pl.ds(kstart, n_loc), cs]) + bias
                s_ref[slot, :, n_loc:n_loc + n_ctx] = s_ctx[rr * rows2:(rr + 1) * rows2]
        for rr in range(rows_per_step):
            _, kstart = starts[rr]
            rsl = slice(rr * GRID_W, (rr + 1) * GRID_W)
            for hp in range(pairs):
                cs = slice(hp * LANES, (hp + 1) * LANES)
                slot = rr * pairs + hp
                o = _softmax_pv([(s_ref[slot, :, 0:n_loc], v_ref[0, pl.ds(kstart, n_loc), cs]),
                                 (s_ref[slot, :, n_loc:n_loc + n_ctx], v_ref[0, 0:n_ctx, cs])])
                o_ref[0, rsl, cs] = merge_heads(o, GRID_W).astype(BF16)


def _attention(q, k, v, strips, n_ctx):
    bsz, l, dn = q.shape
    tq = TOKEN_TILE
    whole = lambda: pl.BlockSpec((1, l, dn), lambda b, j: (b, 0, 0), pipeline_mode=pl.Buffered(1))
    return pl.pallas_call(
        functools.partial(_attn_kernel, n_ctx),
        grid=(bsz, l // tq),
        in_specs=[pl.BlockSpec((1, tq, dn), lambda b, j: (b, j, 0)), whole(), whole(),
                  pl.BlockSpec(strips.shape, lambda b, j: (0, 0, 0, 0), pipeline_mode=pl.Buffered(1))],
        out_specs=pl.BlockSpec((1, tq, dn), lambda b, j: (b, j, 0)),
        out_shape=jax.ShapeDtypeStruct((bsz, l, dn), BF16),
        scratch_shapes=[pltpu.VMEM(((tq // GRID_W) * (N_HEADS // 2), 2 * GRID_W, NA_ROWS * GRID_W + n_ctx), F32)],
        compiler_params=_params(("arbitrary", "arbitrary")),
        name="attention",
    )(q, k, v, strips)


def _bias_strips(rpb):
    col = np.arange(GRID_W)
    start = np.clip(col - NA_COLS // 2, 0, GRID_W - NA_COLS)
    mask = (col[None, :] >= start[:, None]) & (col[None, :] < start[:, None] + NA_COLS)
    dc_idx = np.clip(col[None, :] - col[:, None] + NA_COLS - 1, 0, 2 * NA_COLS - 2)
    pick = jnp.asarray(dc_idx[None] == np.arange(2 * NA_COLS - 1)[:, None, None], F32)
    bias = jnp.einsum("hrc,cqk->hrqk", rpb.astype(F32), pick, precision=lax.Precision.HIGHEST)
    bias = jnp.where(jnp.asarray(mask)[None, None], bias, NEG_INF)
    cases = [jnp.concatenate([bias[:, c + j] for j in range(NA_ROWS)], axis=-1) for c in range(NA_ROWS)]
    return jnp.stack(cases, axis=1)


def _scan_permutation(tc):
    p = np.arange(tc)
    t = (p % SUBLANES) * (tc // SUBLANES) + p // SUBLANES
    m = np.zeros((tc, tc), np.float32)
    m[p, t] = 1.0
    return jnp.asarray(m, BF16), jnp.asarray(m.T, BF16)


def _rglru_kernel(xf_ref, xb_ref, p_ref, pt_ref, cw_ref, cb_ref, wg_ref, bg_ref, nsp_ref, of_ref, ob_ref,
                  edge_ref, h_ref):
    j = pl.program_id(1)

    @pl.when(j == 0)
    def _():
        h_ref[...] = jnp.zeros_like(h_ref)

    @pl.when(j <= 1)
    def _():
        edge_ref[...] = jnp.zeros_like(edge_ref)

    chains = [_rglru_chunk(dirn == 1, x_ref[0], p_ref, pt_ref, cw_ref.at[dirn], cb_ref.at[dirn],
                           wg_ref.at[dirn], bg_ref.at[dirn], nsp_ref.at[dirn],
                           edge_ref.at[dirn], h_ref.at[dirn])
              for dirn, x_ref in enumerate((xf_ref, xb_ref))]
    outs = [None, None]
    for _ in range(RGLRU_PHASES):
        outs = [next(chain) for chain in chains]
    of_ref[0] = outs[0]
    ob_ref[0] = outs[1]


RGLRU_PHASES = 6


def _rglru_chunk(reverse, x, p_ref, pt_ref, cw_ref, cb_ref, wg_ref, bg_ref, nsp_ref, edge_ref, h_ref):
    tc, dr = x.shape
    sl = SUBLANES
    nv = tc // sl
    reach = (RNN_CONV_K - 1) * sl
    xp = _dot(p_ref[...], x)
    yield None
    sub = lax.broadcasted_iota(jnp.int32, (sl, dr), 0)
    prev = edge_ref[...]
    if not reverse:
        edge = xp[tc - reach:tc]
        fixed = jnp.concatenate(
            [jnp.where(sub == 0, pltpu.roll(prev[v * sl:(v + 1) * sl], 1, 0),
                       pltpu.roll(edge[v * sl:(v + 1) * sl], 1, 0)) for v in range(RNN_CONV_K - 1)], axis=0)
        shifted = [xp] + [jnp.concatenate([fixed[reach - d * sl:reach], xp[0:tc - d * sl]], axis=0)
                          for d in range(1, RNN_CONV_K)]
    else:
        edge = xp[0:reach]
        fixed = jnp.concatenate(
            [jnp.where(sub == sl - 1, pltpu.roll(prev[v * sl:(v + 1) * sl], sl - 1, 0),
                       pltpu.roll(edge[v * sl:(v + 1) * sl], sl - 1, 0)) for v in range(RNN_CONV_K - 1)], axis=0)
        shifted = [xp] + [jnp.concatenate([xp[d * sl:tc], fixed[0:d * sl]], axis=0)
                          for d in range(1, RNN_CONV_K)]
    edge_ref[...] = edge
    u = jnp.zeros((tc, dr), F32) + cb_ref[...]
    for kk in range(RNN_CONV_K):
        u = u + shifted[RNN_CONV_K - 1 - kk] * cw_ref[kk:kk + 1, :]
    yield None

    z = _dot(u.astype(BF16), wg_ref[...]) + bg_ref[...]
    yield None
    r = _sigmoid(z[:, 0:dr])
    ig = _sigmoid(z[:, dr:2 * dr])
    a = jnp.exp(r * nsp_ref[...])
    bx = jnp.sqrt(1.0 - a * a) * (ig * u)
    yield None

    order = list(range(nv)) if not reverse else list(range(nv - 1, -1, -1))
    hs, prods = {}, {}
    h = None
    prod = None
    for v in order:
        av = a[v * sl:(v + 1) * sl]
        bv = bx[v * sl:(v + 1) * sl]
        h = bv if h is None else av * h + bv
        prod = av if prod is None else av * prod
        hs[v], prods[v] = h, prod
    h_end, p_end = hs[order[-1]], prods[order[-1]]
    c = h_ref[...]
    carry = {}
    for s in (range(sl) if not reverse else range(sl - 1, -1, -1)):
        carry[s] = c
        c = p_end[s:s + 1] * c + h_end[s:s + 1]
    h_ref[...] = c
    cin = jnp.concatenate([carry[s] for s in range(sl)], axis=0)
    out = jnp.concatenate([hs[v] + prods[v] * cin for v in range(nv)], axis=0)
    yield None
    yield _dot(pt_ref[...], out.astype(BF16)).astype(BF16)


def _rglru(rx, perm, perm_t, cw, cb, wg, bg, nsp):
    bsz, l, dr = rx.shape
    tc = TOKEN_TILE
    n = l // tc
    fwd = pl.BlockSpec((1, tc, dr), lambda b, j: (b, j, 0))
    bwd = pl.BlockSpec((1, tc, dr), lambda b, j: (b, jnp.where(j == 0, 0, n - j), 0))
    out = jax.ShapeDtypeStruct((bsz, l, dr), BF16)
    return pl.pallas_call(
        _rglru_kernel,
        grid=(bsz, n),
        in_specs=[fwd, bwd, _const_spec(perm.shape), _const_spec(perm_t.shape), _const_spec(cw.shape),
                  _const_spec(cb.shape), _const_spec(wg.shape), _const_spec(bg.shape), _const_spec(nsp.shape)],
        out_specs=[fwd, bwd],
        out_shape=[out, out],
        scratch_shapes=[pltpu.VMEM((2, (RNN_CONV_K - 1) * SUBLANES, dr), F32), pltpu.VMEM((2, 1, dr), F32)],
        compiler_params=_params(("arbitrary", "arbitrary")),
        name="rglru",
    )(rx, rx, perm, perm_t, cw, cb, wg, bg, nsp)


def _route(h, rw_ref, rb_ref, tri_ref):
    tm = h.shape[0]
    hi = h.astype(BF16)
    lo = (h - hi.astype(F32)).astype(BF16)
    d = h.shape[1]
    parts = _dot(hi, rw_ref[0:d, :]) + _dot(lo, rw_ref[d:2 * d, :])
    parts_t = parts.T
    logits = parts_t[0:N_EXPERTS] + parts_t[N_EXPERTS:2 * N_EXPERTS]
    scores = _sigmoid(logits)
    sel = scores + rb_ref[...]
    row = lambda t, e: t[e:e + 1, :]
    epg = EXPERTS_PER_GROUP
    gscore = []
    for g in range(N_GROUPS):
        best = None
        for a in range(epg):
            for b in range(a + 1, epg):
                pair = row(sel, g * epg + a) + row(sel, g * epg + b)
                best = pair if best is None else jnp.maximum(best, pair)
        gscore.append(best)
    gbest = gscore[0]
    gidx = jnp.zeros_like(gbest, dtype=jnp.int32)
    for g in range(1, N_GROUPS):
        better = gscore[g] > gbest
        gidx = jnp.where(better, g, gidx)
        gbest = jnp.where(better, gscore[g], gbest)
    picked = []
    flags = []
    for e in range(N_EXPERTS):
        g = e // epg
        rank = jnp.zeros_like(gidx)
        for o in range(g * epg, (g + 1) * epg):
            if o == e:
                continue
            ahead = (row(sel, o) > row(sel, e)) if o > e else (row(sel, o) >= row(sel, e))
            rank = rank + ahead.astype(jnp.int32)
        chosen = jnp.logical_and(gidx == g, rank < 2)
        picked.append(jnp.where(chosen, row(scores, e), 0.0))
        flags.append(jnp.where(chosen, 1.0, 0.0))
    den = picked[0]
    for e in range(1, N_EXPERTS):
        den = den + picked[e]
    gate = jnp.concatenate(picked, axis=0) / den
    flag = jnp.concatenate(flags, axis=0)

    cnt = jnp.sum(flag, axis=1, keepdims=True).astype(jnp.int32)
    pc = jnp.left_shift(jnp.right_shift(cnt + (ROW_ALIGN - 1), ROW_SHIFT), ROW_SHIFT)
    starts = [jnp.zeros((1, 1), jnp.int32)]
    for e in range(1, N_EXPERTS):
        starts.append(starts[-1] + pc[e - 1:e, :])
    seg0 = jnp.concatenate(starts, axis=0).astype(F32)
    pos = seg0 + _dot(flag.astype(BF16), tri_ref[...])
    used = flag > 0.5
    pos_lo = jnp.min(jnp.where(used, pos, 1e9), axis=0, keepdims=True)
    pos_hi = jnp.max(jnp.where(used, pos, -1.0), axis=0, keepdims=True)
    w_lo = jnp.sum(jnp.where(pos == pos_lo, gate, 0.0), axis=0, keepdims=True)
    w_hi = jnp.sum(jnp.where(pos == pos_hi, gate, 0.0), axis=0, keepdims=True)
    info = jnp.concatenate([pos_lo, pos_hi, w_lo, w_hi, jnp.zeros((4, tm), F32)], axis=0)
    return hi, info, jnp.broadcast_to(cnt, (N_EXPERTS, LANES))


MERGE_TILES = 2


def _merge_kernel(n_src, n_ctx_tiles, tiles_per_batch, *refs):
    n_in = n_src + 9
    shared = refs[MERGE_TILES * n_in:MERGE_TILES * n_in + 8]
    n2_ref, pw_ref, no_ref, ro_ref, ow_ref, rw_ref, rb_ref, tri_ref = shared
    o_ref, h2_ref, info_ref, cnt_ref = refs[MERGE_TILES * n_in + 8:]
    d = o_ref.shape[1]
    tm = o_ref.shape[0] // MERGE_TILES
    chains = []
    for h in range(MERGE_TILES):
        tile_refs = refs[h * n_in:(h + 1) * n_in]
        chains.append((h, tile_refs[:n_src]) + tuple(tile_refs[n_src:]))
    branch = []
    for h, x_refs, g1_ref, sh2_ref, sc2_ref, cv_ref, at_ref, hf_ref, hb_ref, rg_ref, gt_ref in chains:
        hsum = hf_ref[0].astype(F32) + hb_ref[0].astype(F32)
        branch.append((_dot(cv_ref[0], pw_ref[0]), _dot(at_ref[0], no_ref[0]),
                       _dot((rg_ref[0].astype(F32) * hsum).astype(BF16), ro_ref[0])))
    ys = []
    for (conv, na, rnn), chain in zip(branch, chains):
        gt_ref = chain[-1]
        m = (gt_ref[0, :, 0:d].astype(F32) * conv + gt_ref[0, :, d:2 * d].astype(F32) * na
             + gt_ref[0, :, 2 * d:3 * d].astype(F32) * rnn)
        ys.append(_dot(m.astype(BF16), ow_ref[0]))
    hms = []
    for y, (h, x_refs, g1_ref, sh2_ref, sc2_ref, *_) in zip(ys, chains):
        if n_src == 1:
            x_in = x_refs[0][0]
        else:
            tile = lax.rem(pl.program_id(0) * MERGE_TILES + h, tiles_per_batch)
            x_in = jnp.where(tile < n_ctx_tiles, x_refs[0][0], x_refs[1][0])
        x = x_in + g1_ref[0] * y
        o_ref[h * tm:(h + 1) * tm, :] = x
        hm = x * lax.rsqrt(jnp.mean(x * x, axis=-1, keepdims=True) + EPS) * n2_ref[...]
        hms.append(hm * (1.0 + sc2_ref[0]) + sh2_ref[0])
    for h, hm in enumerate(hms):
        h2, info, cnt = _route(hm, rw_ref, rb_ref, tri_ref)
        h2_ref[h * tm:(h + 1) * tm, :] = h2
        info_ref[:, h * tm:(h + 1) * tm] = info
        cnt_ref[h] = cnt


def _merge(src, modtab, n2, cv, at, hf, hb, rg, gt, pw, no, ro, ow, layer, rw, rb, tri):
    bsz, l, _ = cv.shape
    d = ow.shape[2]
    tm = TOKEN_TILE
    nt = l // tm
    assert (bsz * nt) % MERGE_TILES == 0
    is_pair = isinstance(src, tuple)
    nc = src[0].shape[1] // tm if is_pair else 0
    in_specs, args = [], []
    for h in range(MERGE_TILES):
        def where(k, h=h):
            t = k * MERGE_TILES + h
            return t // nt, lax.rem(t, nt), t

        def tile(n, where=where):
            return pl.BlockSpec((1, tm, n), lambda k: where(k)[:2] + (0,))

        def mod(which, where=where):
            def index(k):
                b, i, _ = where(k)
                return ((2 * b + jnp.minimum(i, 1)) * N_MOD + which, 0, 0)
            return pl.BlockSpec((1, 1, d), index)

        if is_pair:
            in_specs += [pl.BlockSpec((1, tm, d), lambda k, where=where: (where(k)[0], jnp.minimum(where(k)[1], nc - 1), 0)),
                         pl.BlockSpec((1, tm, d), lambda k, where=where: (where(k)[0], jnp.maximum(where(k)[1] - nc, 0), 0))]
            args += list(src)
        else:
            in_specs.append(tile(d))
            args.append(src)
        in_specs += [mod(2), mod(3), mod(4), tile(cv.shape[2]), tile(at.shape[2]), tile(hf.shape[2]),
                     tile(hb.shape[2]), tile(rg.shape[2]), tile(3 * d)]
        args += [modtab, modtab, modtab, cv, at, hf, hb, rg, gt]
    in_specs += [_const_spec((1, d)), _layer_spec(pw, layer), _layer_spec(no, layer), _layer_spec(ro, layer),
                 _layer_spec(ow, layer), _const_spec(rw.shape), _const_spec(rb.shape), _const_spec(tri.shape)]
    args += [n2, pw, no, ro, ow, rw, rb, tri]
    rows = MERGE_TILES * tm
    xa, h2, info, cnt = pl.pallas_call(
        functools.partial(_merge_kernel, 2 if is_pair else 1, nc, nt),
        grid=(bsz * nt // MERGE_TILES,),
        in_specs=in_specs,
        out_specs=[pl.BlockSpec((rows, d), lambda k: (k, 0)), pl.BlockSpec((rows, d), lambda k: (k, 0)),
                   pl.BlockSpec((SUBLANES, rows), lambda k: (0, k)),
                   pl.BlockSpec((MERGE_TILES, N_EXPERTS, LANES), lambda k: (k, 0, 0))],
        out_shape=[jax.ShapeDtypeStruct((bsz * l, d), F32), jax.ShapeDtypeStruct((bsz * l, d), BF16),
                   jax.ShapeDtypeStruct((SUBLANES, bsz * l), F32),
                   jax.ShapeDtypeStruct((bsz * nt, N_EXPERTS, LANES), jnp.int32)],
        compiler_params=_params(("arbitrary",)),
        name="merge_route",
    )(*args)
    return xa.reshape(bsz, l, d), h2.reshape(bsz, l, d), info, cnt


def _route_plan(cnt, n_tiles):
    pc = (cnt + (ROW_ALIGN - 1)) // ROW_ALIGN * ROW_ALIGN
    ls = jnp.cumsum(pc, axis=1) - pc
    seg_tot = jnp.sum(pc, axis=0)
    reg = (seg_tot + (EXPERT_TILE - 1)) // EXPERT_TILE * EXPERT_TILE
    reg_end = jnp.cumsum(reg)
    base = reg_end - reg
    gs = base[None, :] + jnp.cumsum(pc, axis=0) - pc
    n_used = reg_end[-1] // EXPERT_TILE
    tiles = jnp.minimum(jnp.arange(n_tiles, dtype=jnp.int32), n_used - 1)
    tile_expert = jnp.minimum(jnp.sum(tiles[:, None] >= (reg_end // EXPERT_TILE)[None, :], axis=1),
                              N_EXPERTS - 1).astype(jnp.int32)
    i32 = lambda a: a.astype(jnp.int32).reshape(-1)
    return dict(pc=i32(pc), ls=i32(ls), gs=i32(gs), tot=i32(jnp.sum(pc, axis=1)),
                tail_start=i32(base + seg_tot), tail_len=i32(reg - seg_tot),
                tile_expert=tile_expert, n_used=i32(n_used))


def _for_each_piece(n, sizes, fn):
    off = jnp.int32(0)
    for size in sizes:
        hit = (n & size) != 0

        @pl.when(hit)
        def _(off=off, size=size):
            fn(pl.multiple_of(off, ROW_ALIGN), size)

        off = off + jnp.where(hit, size, 0)


def _window_copies(pc_ref, ls_ref, gs_ref, win, staged_at, sorted_at, to_sorted, sem, fn):
    for e in range(N_EXPERTS):
        n = pc_ref[win * N_EXPERTS + e]
        src0 = ls_ref[win * N_EXPERTS + e]
        dst0 = gs_ref[win * N_EXPERTS + e]

        def piece(off, size, src0=src0, dst0=dst0):
            a = staged_at(pl.multiple_of(src0 + off, ROW_ALIGN), size)
            b = sorted_at(pl.multiple_of(dst0 + off, ROW_ALIGN), size)
            fn(pltpu.make_async_copy(a, b, sem) if to_sorted else pltpu.make_async_copy(b, a, sem))

        _for_each_piece(n, SEG_SIZES, piece)


def _wait_rows(n, desc):
    for size in WAIT_SIZES:
        @pl.when((n & size) != 0)
        def _(size=size):
            desc(size).wait()


def _one_hot_rows(info_ref, rows, lo_val, hi_val):
    tm = info_ref.shape[1]
    rid = lax.broadcasted_iota(jnp.int32, (rows, tm), 0).astype(F32)
    return jnp.where(rid == info_ref[0:1, :], lo_val, jnp.where(rid == info_ref[1:2, :], hi_val, 0.0))


def _dispatch_kernel(pc_ref, ls_ref, gs_ref, tot_ref, ts_ref, tl_ref, nu_ref, h_ref, info_ref, xs_ref,
                     buf_ref, zero_ref, sem_ref, zsem_ref):
    w = pl.program_id(0)
    nw = pl.num_programs(0)
    n_slots = buf_ref.shape[0]
    slot_of = lambda win: lax.rem(win, n_slots)
    slot = slot_of(w)
    rows = buf_ref.shape[1]
    p = _one_hot_rows(info_ref, rows, 1.0, 1.0).astype(BF16)
    buf_ref[slot] = _dot(p, h_ref[...]).astype(BF16)

    def start_copies(win):
        sl = slot_of(win)
        _window_copies(pc_ref, ls_ref, gs_ref, win,
                       lambda r, n: buf_ref.at[sl, pl.ds(r, n)], lambda r, n: xs_ref.at[pl.ds(r, n)],
                       True, sem_ref.at[sl], lambda c: c.start())

    def wait_copies(win):
        sl = slot_of(win)
        _wait_rows(tot_ref[win], lambda n: pltpu.make_async_copy(
            buf_ref.at[sl, pl.ds(0, n)], xs_ref.at[pl.ds(0, n)], sem_ref.at[sl]))

    lag = n_slots - 1

    @pl.when(w >= lag)
    def _():
        wait_copies(w - lag)

    start_copies(w)

    @pl.when(w == nw - 1)
    def _():
        for back in range(lag - 1, -1, -1):
            @pl.when(w >= back)
            def _(back=back):
                wait_copies(w - back)
        zero_ref[...] = jnp.zeros_like(zero_ref)

        def tails(fn):
            for e in range(N_EXPERTS):
                start = ts_ref[e]

                def piece(off, size, start=start):
                    fn(pltpu.make_async_copy(zero_ref.at[pl.ds(0, size)],
                                             xs_ref.at[pl.ds(pl.multiple_of(start + off, ROW_ALIGN), size)],
                                             zsem_ref.at[0]))

                _for_each_piece(tl_ref[e], TAIL_SIZES, piece)

        tails(lambda c: c.start())
        tails(lambda c: c.wait())

        fill = zero_ref.shape[0]
        used = nu_ref[0] * EXPERT_TILE

        def spare(k):
            return pltpu.make_async_copy(
                zero_ref, xs_ref.at[pl.ds(pl.multiple_of(used + k * fill, ROW_ALIGN), fill)], zsem_ref.at[0])

        n_fill = (xs_ref.shape[0] - used) // fill
        lax.fori_loop(0, n_fill, lambda k, c: (spare(k).start(), c)[1], 0)
        lax.fori_loop(0, n_fill, lambda k, c: (spare(k).wait(), c)[1], 0)


def _dispatch(h2, info, plan, n_rows):
    t, d = h2.shape
    tm = TOKEN_TILE
    rows = STAGE_ROWS
    grid_spec = pltpu.PrefetchScalarGridSpec(
        num_scalar_prefetch=7,
        grid=(t // tm,),
        in_specs=[pl.BlockSpec((tm, d), lambda w, *_: (w, 0)),
                  pl.BlockSpec((SUBLANES, tm), lambda w, *_: (0, w))],
        out_specs=pl.BlockSpec(memory_space=pl.ANY),
        scratch_shapes=[pltpu.VMEM((MOE_SLOTS, rows, d), BF16), pltpu.VMEM((TAIL_SIZES[0], d), BF16),
                        pltpu.SemaphoreType.DMA((MOE_SLOTS,)), pltpu.SemaphoreType.DMA((1,))])
    return pl.pallas_call(
        _dispatch_kernel,
        grid_spec=grid_spec,
        out_shape=jax.ShapeDtypeStruct((n_rows, d), BF16),
        compiler_params=_params(("arbitrary",)),
        name="moe_dispatch",
    )(plan["pc"], plan["ls"], plan["gs"], plan["tot"], plan["tail_start"], plan["tail_len"], plan["n_used"],
      h2, info)


def _experts_kernel(te_ref, nu_ref, xs_ref, w1_ref, w3_ref, w2_ref, ys_ref, w13_ref, w2b_ref, z_ref):
    i = pl.program_id(0)
    de = w1_ref.shape[3]
    prev = te_ref[jnp.maximum(i - 1, 0)]

    @pl.when(jnp.logical_or(i == 0, te_ref[i] != prev))
    def _():
        w13_ref[:, 0:de] = w1_ref[0, 0].astype(BF16)
        w13_ref[:, de:2 * de] = w3_ref[0, 0].astype(BF16)
        w2b_ref[...] = w2_ref[0, 0].astype(BF16)

    @pl.when(i >= nu_ref[0])
    def _():
        ys_ref[...] = jnp.zeros_like(ys_ref)

    @pl.when(i < nu_ref[0])
    def _():
        for blk, r0 in enumerate(range(0, xs_ref.shape[0], EXPERT_SUB)):
            z_ref[blk] = _dot(xs_ref[r0:r0 + EXPERT_SUB, :], w13_ref[...])
        for blk, r0 in enumerate(range(0, xs_ref.shape[0], EXPERT_SUB)):
            he = _silu(z_ref[blk, :, 0:de]) * z_ref[blk, :, de:2 * de]
            ys_ref[r0:r0 + EXPERT_SUB, :] = _dot(he.astype(BF16), w2b_ref[...]).astype(BF16)


def _experts(xs, plan, w1, w3, w2, layer):
    n_rows, d = xs.shape
    de = w1.shape[3]
    te, nu = plan["tile_expert"], plan["n_used"]
    tm = EXPERT_TILE
    grid_spec = pltpu.PrefetchScalarGridSpec(
        num_scalar_prefetch=2,
        grid=(n_rows // tm,),
        in_specs=[pl.BlockSpec((tm, d), lambda i, te, nu: (jnp.minimum(i, nu[0] - 1), 0)),
                  pl.BlockSpec((1, 1, d, de), lambda i, te, nu: (layer, te[i], 0, 0)),
                  pl.BlockSpec((1, 1, d, de), lambda i, te, nu: (layer, te[i], 0, 0)),
                  pl.BlockSpec((1, 1, de, d), lambda i, te, nu: (layer, te[i], 0, 0))],
        out_specs=pl.BlockSpec((tm, d), lambda i, te, nu: (i, 0)),
        scratch_shapes=[pltpu.VMEM((d, 2 * de), BF16), pltpu.VMEM((de, d), BF16),
                        pltpu.VMEM((tm // EXPERT_SUB, EXPERT_SUB, 2 * de), F32)])
    return pl.pallas_call(
        _experts_kernel,
        grid_spec=grid_spec,
        out_shape=jax.ShapeDtypeStruct((n_rows, d), BF16),
        compiler_params=_params(("arbitrary",)),
        name="moe_experts",
    )(te, nu, xs, w1, w3, w2)


def _combine_kernel(pc_ref, ls_ref, gs_ref, tot_ref, x_ref, g2_ref, info_ref, ys_ref, o_ref, buf_ref, sem_ref):
    nt = pl.num_programs(1)
    w = pl.program_id(0) * nt + pl.program_id(1)
    nw = pl.num_programs(0) * nt
    n_slots = buf_ref.shape[0]
    slot = lax.rem(w, n_slots)
    rows = buf_ref.shape[1]
    ahead = n_slots - 1

    def start_copies(win):
        sl = lax.rem(win, n_slots)
        _window_copies(pc_ref, ls_ref, gs_ref, win,
                       lambda r, n: buf_ref.at[sl, pl.ds(r, n)], lambda r, n: ys_ref.at[pl.ds(r, n)],
                       False, sem_ref.at[sl], lambda c: c.start())

    @pl.when(w == 0)
    def _():
        buf_ref[...] = jnp.zeros_like(buf_ref)
        for first in range(ahead):
            @pl.when(first < nw)
            def _(first=first):
                start_copies(first)

    @pl.when(w + ahead < nw)
    def _():
        start_copies(w + ahead)

    _wait_rows(tot_ref[w], lambda n: pltpu.make_async_copy(
        ys_ref.at[pl.ds(0, n)], buf_ref.at[slot, pl.ds(0, n)], sem_ref.at[slot]))
    pw =_one_hot_rows(info_ref, rows, info_ref[2:3, :], info_ref[3:4, :]).astype(BF16)
    moe = lax.dot_general(pw, buf_ref[slot], (((0,), (0,)), ((), ())), preferred_element_type=F32)
    o_ref[0] = x_ref[0] + g2_ref[0] * moe


def _combine(xa, modtab, info, ys, plan, drop):
    bsz, l, d = xa.shape
    tm = TOKEN_TILE
    nt = l // tm
    grid_spec = pltpu.PrefetchScalarGridSpec(
        num_scalar_prefetch=4,
        grid=(bsz, nt),
        in_specs=[pl.BlockSpec((1, tm, d), lambda b, i, *_: (b, i, 0)),
                  _mod_spec(d, 5),
                  pl.BlockSpec((SUBLANES, tm), lambda b, i, *_: (0, b * nt + i)),
                  pl.BlockSpec(memory_space=pl.ANY)],
        out_specs=pl.BlockSpec((1, tm, d), lambda b, i, *_: (b, jnp.maximum(i - drop, 0), 0)),
        scratch_shapes=[pltpu.VMEM((MOE_SLOTS, STAGE_ROWS, d), BF16), pltpu.SemaphoreType.DMA((MOE_SLOTS,))])
    return pl.pallas_call(
        _combine_kernel,
        grid_spec=grid_spec,
        out_shape=jax.ShapeDtypeStruct((bsz, l - drop * tm, d), F32),
        compiler_params=_params(("arbitrary", "arbitrary")),
        name="moe_combine",
    )(plan["pc"], plan["ls"], plan["gs"], plan["tot"], xa, modtab, info, ys)


def _moe_sparse(xa, modtab, h2, info, cnt, w1, w3, w2, layer, drop):
    bsz, l, d = xa.shape
    t = bsz * l
    n_win = t // TOKEN_TILE
    bound = TOP_K * t + n_win * N_EXPERTS * (ROW_ALIGN - 1) + N_EXPERTS * (EXPERT_TILE - ROW_ALIGN)
    n_tiles = -(-bound // EXPERT_TILE)
    plan = _route_plan(cnt[:, :, 0], n_tiles)
    xs = _dispatch(h2.reshape(t, d), info, plan, n_tiles * EXPERT_TILE)
    ys = _experts(xs, plan, w1, w3, w2, layer)
    return _combine(xa, modtab, info, ys, plan, drop)


def _block_diag(w):
    n, k, _ = w.shape
    eye = jnp.eye(n, dtype=w.dtype)
    return jnp.einsum("nij,nm->nimj", w, eye).reshape(n * k, n * k)


def kernel(x, c, ctx, c_ctx, router_w, router_bias, mod_w, mod_b, norm1_g, norm2_g, in_w, in_b,
           conv_dw_w, conv_dw_b, conv_ln_g, conv_ln_b, conv_pw_w, na_q_g, na_k_g, na_rpb, na_out_w,
           rnn_conv_w, rnn_conv_b, rnn_wa, rnn_ba, rnn_wx, rnn_bx, rnn_lam, rnn_out_w, out_w,
           exp_w1, exp_w3, exp_w2):
    bsz, s, d = x.shape
    n_ctx = ctx.shape[1]
    depth = mod_w.shape[0]
    dc = conv_pw_w.shape[1]
    dn = na_out_w.shape[1]
    dr = rnn_out_w.shape[1]
    sizes = (dc, dn, dr, d)
    assert n_ctx % TOKEN_TILE == 0 and s % TOKEN_TILE == 0 and s % GRID_W == 0
    assert dn == N_HEADS * HEAD_DIM and s // GRID_W >= NA_ROWS

    xa = (ctx, x)
    seq = n_ctx + s

    c_rows = -(-(bsz + 1) // SUBLANES) * SUBLANES
    c_all = jnp.zeros((c_rows, d), F32).at[:bsz].set(c).at[bsz].set(c_ctx)
    mods = _mod_vectors(c_all, mod_w, mod_b)

    head_mean = _block_diag(jnp.full((N_HEADS, HEAD_DIM, HEAD_DIM), 1.0 / HEAD_DIM, F32)).astype(BF16)
    rw_hi = router_w.astype(BF16)
    rw_lo = (router_w - rw_hi.astype(F32)).astype(BF16)
    rw = jnp.zeros((2 * d, LANES), BF16)
    rw = rw.at[:d, 0:N_EXPERTS].set(rw_hi).at[:d, N_EXPERTS:2 * N_EXPERTS].set(rw_lo)
    rw = rw.at[d:, 0:N_EXPERTS].set(rw_hi)
    rb = router_bias.reshape(N_EXPERTS, 1)
    tri = jnp.asarray(np.triu(np.ones((TOKEN_TILE, TOKEN_TILE), np.float32), k=1), BF16)
    shift_a, shift_b = _conv_shift_matrices(TOKEN_TILE + 2 * HALO)
    perm, perm_t = _scan_permutation(TOKEN_TILE)

    in_w_b = in_w.astype(BF16)
    pw_b, no_b, ro_b, ow_b = (a.astype(BF16) for a in (conv_pw_w, na_out_w, rnn_out_w, out_w))

    for l in range(depth):
        ml = mods[l].reshape(c_rows, N_MOD, d)
        modtab = jnp.stack([jnp.broadcast_to(ml[bsz], (bsz, N_MOD, d)), ml[:bsz]], axis=1)
        modtab = modtab.reshape(bsz * 2 * N_MOD, 1, d)

        qg = (jnp.tile(na_q_g[l], N_HEADS) * HEAD_DIM ** -0.5).reshape(1, dn)
        kg = jnp.tile(na_k_g[l], N_HEADS).reshape(1, dn)
        u, q, k, v, rx, rg, gt = _inproj(xa, bsz, seq, modtab, norm1_g[l].reshape(1, d), in_w_b, l,
                                         in_b[l].reshape(1, -1), qg, kg, head_mean, sizes)

        cw = jnp.zeros((CONV_K + 1, dc), F32).at[:CONV_K].set(conv_dw_w[l])
        cv = _conv_branch(u, shift_a, shift_b, cw, conv_dw_b[l].reshape(1, dc),
                          conv_ln_g[l].reshape(1, dc), conv_ln_b[l].reshape(1, dc))

        at = _attention(q, k, v, _bias_strips(na_rpb[l]), n_ctx)

        wg = jnp.stack([jnp.concatenate([_block_diag(rnn_wa[l, dn_]), _block_diag(rnn_wx[l, dn_])], axis=1)
                        for dn_ in range(2)]).astype(BF16)
        bg = jnp.concatenate([rnn_ba[l], rnn_bx[l]], axis=-1).reshape(2, 1, 2 * dr)
        nsp = (-LRU_C * jax.nn.softplus(-rnn_lam[l])).reshape(2, 1, dr)
        rcw = jnp.zeros((2, SUBLANES, dr), F32).at[:, :RNN_CONV_K].set(rnn_conv_w[l])
        hf, hb = _rglru(rx, perm, perm_t, rcw, rnn_conv_b[l].reshape(2, 1, dr), wg, bg, nsp)

        xa, h2, info, cnt = _merge(xa, modtab, norm2_g[l].reshape(1, d), cv, at, hf, hb, rg, gt,
                                   pw_b, no_b, ro_b, ow_b, l, rw, rb, tri)

        drop = n_ctx // TOKEN_TILE if l == depth - 1 else 0
        xa = _moe_sparse(xa, modtab, h2, info, cnt, exp_w1, exp_w3, exp_w2, l, drop)

    return xa
```

```python
import functools
import math

import jax
import jax.numpy as jnp
import numpy as np
from jax import lax
from jax.experimental import pallas as pl
from jax.experimental.pallas import tpu as pltpu

F32 = jnp.float32
BF16 = jnp.bfloat16

GRID_W = 64
CONV_K = 31
N_HEADS = 8
HEAD_DIM = 64
NA_ROWS = 8
NA_COLS = 16
RNN_BLOCKS = 8
RNN_CONV_K = 4
LRU_C = 8.0
N_EXPERTS = 16
N_GROUPS = 4
EXPERTS_PER_GROUP = N_EXPERTS // N_GROUPS
N_MOD = 6
EPS = 1e-6
NEG_INF = -1e30

SUBLANES = 8
LANES = 128
MXU_DIM = 256
VMEM_LIMIT = 56 * 1024 * 1024

TOP_K = 2
TOKEN_TILE = 256
HALO = 16
ROW_ALIGN = 16
ROW_SHIFT = 4
EXPERT_TILE = 512
EXPERT_SUB = 256
SEG_SIZES = tuple(TOKEN_TILE >> k for k in range(TOKEN_TILE.bit_length() - ROW_SHIFT))
SEG_SPLIT = 128
MOE_SLOTS = 4
TAIL_SIZES = tuple((EXPERT_TILE // 2) >> k for k in range(EXPERT_TILE.bit_length() - 1 - ROW_SHIFT))
STAGE_ROWS = -(-(TOP_K * TOKEN_TILE + N_EXPERTS * (ROW_ALIGN - 1)) // MXU_DIM) * MXU_DIM
WAIT_SIZES = tuple(1 << k for k in range(STAGE_ROWS.bit_length() - 1, ROW_SHIFT - 1, -1))


def _params(sem, vmem=VMEM_LIMIT):
    return pltpu.CompilerParams(dimension_semantics=sem, vmem_limit_bytes=vmem)


def _sigmoid(x):
    return 0.5 * (1.0 + jnp.tanh(0.5 * x))


def _silu(x):
    return x * _sigmoid(x)


def _gelu_tanh(x):
    return x * (0.5 * (1.0 + jnp.tanh(math.sqrt(2.0 / math.pi) * (x + 0.044715 * (x * x * x)))))


def _dot(a, b):
    return jnp.dot(a, b, preferred_element_type=F32)


def _dot_nt(a, b):
    return lax.dot_general(a, b, (((1,), (1,)), ((), ())), preferred_element_type=F32)


def _const_spec(shape):
    return pl.BlockSpec(shape, lambda *_: (0,) * len(shape))


def _layer_spec(arr, layer):
    return pl.BlockSpec((1,) + arr.shape[1:], lambda *_: (layer,) + (0,) * (arr.ndim - 1))


def _mod_kernel(c_ref, w_ref, b_ref, o_ref):
    a = _silu(c_ref[...])
    o_ref[0] = jnp.dot(a, w_ref[0], preferred_element_type=F32,
                       precision=lax.Precision.HIGHEST) + b_ref[0]


def _mod_vectors(c_all, mod_w, mod_b):
    depth, d, n = mod_w.shape
    tn = 1536
    rows = c_all.shape[0]
    return pl.pallas_call(
        _mod_kernel,
        grid=(depth, n // tn),
        in_specs=[pl.BlockSpec((rows, d), lambda l, j: (0, 0)),
                  pl.BlockSpec((1, d, tn), lambda l, j: (l, 0, j)),
                  pl.BlockSpec((1, 1, tn), lambda l, j: (l, 0, j))],
        out_specs=pl.BlockSpec((1, rows, tn), lambda l, j: (l, 0, j)),
        out_shape=jax.ShapeDtypeStruct((depth, rows, n), F32),
        compiler_params=_params(("arbitrary", "arbitrary")),
        name="mod_vectors",
    )(c_all, mod_w, mod_b.reshape(depth, 1, n))


def _mod_spec(d, which):
    return pl.BlockSpec((1, 1, d),
                        lambda b, i, *_: ((2 * b + jnp.minimum(i, 1)) * N_MOD + which, 0, 0))


def _stream_specs(src, tm):
    if not isinstance(src, tuple):
        return [pl.BlockSpec((1, tm, src.shape[2]), lambda b, i, *_: (b, i, 0))], [src], 0
    ctx, lat = src
    nc = ctx.shape[1] // tm
    d = ctx.shape[2]
    return ([pl.BlockSpec((1, tm, d), lambda b, i, *_: (b, jnp.minimum(i, nc - 1), 0)),
             pl.BlockSpec((1, tm, d), lambda b, i, *_: (b, jnp.maximum(i - nc, 0), 0))], [ctx, lat], nc)


def _stream_tile(x_refs, n_ctx_tiles):
    if len(x_refs) == 1:
        return x_refs[0][0]
    return jnp.where(pl.program_id(1) < n_ctx_tiles, x_refs[0][0], x_refs[1][0])


def _inproj_kernel(sizes, n_src, n_ctx_tiles, *refs):
    x_refs = refs[:n_src]
    (sh_ref, sc_ref, g_ref, w_ref, b_ref, qg_ref, kg_ref, bd_ref,
     u_ref, q_ref, k_ref, v_ref, rx_ref, rg_ref, gt_ref, z_ref) = refs[n_src:]
    dc, dn, dr, d = sizes
    x = _stream_tile(x_refs, n_ctx_tiles)
    h = x * lax.rsqrt(jnp.mean(x * x, axis=-1, keepdims=True) + EPS) * g_ref[...]
    h = h * (1.0 + sc_ref[0]) + sh_ref[0]
    hb = h.astype(BF16)

    d_in = w_ref.shape[2]
    for c0 in range(0, d_in, dn):
        z_ref[:, c0:c0 + dn] = _dot(hb, w_ref[0, :, c0:c0 + dn]) + b_ref[:, c0:c0 + dn]

    def proj(off, n):
        return z_ref[:, off:off + n]

    def head_norm(t, gain_ref):
        ms = _dot((t * t).astype(BF16), bd_ref[...])
        return t * lax.rsqrt(ms + EPS) * gain_ref[...]

    off = 0
    a = proj(off, dc); off += dc
    ga = proj(off, dc); off += dc
    u_ref[0] = (a * _sigmoid(ga)).astype(BF16)
    q_ref[0] = head_norm(proj(off, dn), qg_ref).astype(BF16); off += dn
    k_ref[0] = head_norm(proj(off, dn), kg_ref).astype(BF16); off += dn
    v_ref[0] = proj(off, dn).astype(BF16); off += dn
    rx_ref[0] = proj(off, dr).astype(BF16); off += dr
    rg_ref[0] = _gelu_tanh(proj(off, dr)).astype(BF16); off += dr
    for j in range(3):
        gt_ref[0, :, j * d:(j + 1) * d] = _sigmoid(proj(off, d)).astype(BF16); off += d


def _inproj(src, bsz, l, modtab, g, w, layer, b, qg, kg, bd, sizes):
    dc, dn, dr, d = sizes
    d_in = w.shape[2]
    tm = TOKEN_TILE
    x_specs, x_args, nc = _stream_specs(src, tm)
    tile = lambda n: pl.BlockSpec((1, tm, n), lambda b, i: (b, i, 0))
    out = lambda n: jax.ShapeDtypeStruct((bsz, l, n), BF16)
    return pl.pallas_call(
        functools.partial(_inproj_kernel, sizes, len(x_args), nc),
        grid=(bsz, l // tm),
        in_specs=x_specs + [_mod_spec(d, 0), _mod_spec(d, 1), _const_spec((1, d)),
                            pl.BlockSpec((1, d, d_in), lambda b, i: (layer, 0, 0), pipeline_mode=pl.Buffered(1)),
                            _const_spec((1, d_in)), _const_spec((1, dn)), _const_spec((1, dn)),
                            _const_spec((dn, dn))],
        out_specs=[tile(dc), tile(dn), tile(dn), tile(dn), tile(dr), tile(dr), tile(3 * d)],
        out_shape=[out(dc), out(dn), out(dn), out(dn), out(dr), out(dr), out(3 * d)],
        scratch_shapes=[pltpu.VMEM((tm, d_in), F32)],
        compiler_params=_params(("arbitrary", "arbitrary")),
        name="inproj",
    )(*x_args, modtab, modtab, g, w, b, qg, kg, bd)


CONV_SPLIT = 128


def _conv_shift_matrices(rows):
    ka = CONV_SPLIT + HALO
    nb = rows - CONV_SPLIT
    sa = np.zeros(((SUBLANES - 1) * CONV_SPLIT, ka), np.float32)
    sb = np.zeros(((SUBLANES - 1) * nb, nb), np.float32)
    for s in range(1, SUBLANES):
        for r in range(CONV_SPLIT):
            sa[(s - 1) * CONV_SPLIT + r, r + s] = 1.0
        for r in range(nb - s):
            sb[(s - 1) * nb + r, r + s] = 1.0
    return jnp.asarray(sa, BF16), jnp.asarray(sb, BF16)


def _conv_kernel(u_ref, prev_ref, next_ref, sa_ref, sb_ref, w_ref, b_ref, lg_ref, lb_ref, o_ref,
                 sh_ref, y_ref):
    i = pl.program_id(1)
    n = pl.num_programs(1)
    tm = u_ref.shape[1]
    dc = u_ref.shape[2]
    rows = tm + 2 * HALO
    ka = sa_ref.shape[1]
    nb = rows - CONV_SPLIT
    prev_ok = i >= 2
    next_ok = jnp.logical_and(i >= 1, i < n - 1)
    zero = jnp.zeros((HALO, dc), BF16)
    win = jnp.concatenate([jnp.where(prev_ok, prev_ref[0], zero), u_ref[0],
                           jnp.where(next_ok, next_ref[0], zero)], axis=0)
    sh_ref[0] = win.astype(F32)
    rc = 64
    base = HALO - CONV_K // 2
    for c0 in range(0, dc, MXU_DIM):
        hs = slice(c0, c0 + MXU_DIM)
        sha = _dot(sa_ref[...], win[0:ka, hs])
        shb = _dot(sb_ref[...], win[CONV_SPLIT:rows, hs])
        for s in range(1, SUBLANES):
            sh_ref[s, 0:CONV_SPLIT, hs] = sha[(s - 1) * CONV_SPLIT:s * CONV_SPLIT]
            sh_ref[s, CONV_SPLIT:rows, hs] = shb[(s - 1) * nb:s * nb]
    for cb in range(dc // LANES):
        cs = slice(cb * LANES, (cb + 1) * LANES)
        for r0 in range(0, tm, rc):
            acc = jnp.zeros((rc, LANES), F32)
            for k in range(CONV_K):
                o = base + k
                s, a = o % SUBLANES, o // SUBLANES
                acc = acc + sh_ref[s, r0 + a * SUBLANES:r0 + a * SUBLANES + rc, cs] * w_ref[k:k + 1, cs]
            y_ref[r0:r0 + rc, cs] = acc
    y = y_ref[...] + b_ref[...]
    mu = jnp.mean(y, axis=-1, keepdims=True)
    yc = y - mu
    var = jnp.mean(yc * yc, axis=-1, keepdims=True)
    z = yc * lax.rsqrt(var + EPS) * lg_ref[...] + lb_ref[...]
    o_ref[0] = _silu(z).astype(BF16)


def _conv_branch(u, sa, sb, w, b, lg, lb):
    bsz, l, dc = u.shape
    tm = TOKEN_TILE
    hb = tm // HALO
    nh = l // HALO
    return pl.pallas_call(
        _conv_kernel,
        grid=(bsz, l // tm),
        in_specs=[pl.BlockSpec((1, tm, dc), lambda b, i: (b, i, 0)),
                  pl.BlockSpec((1, HALO, dc), lambda b, i: (b, jnp.maximum(i * hb - 1, 0), 0)),
                  pl.BlockSpec((1, HALO, dc), lambda b, i: (b, jnp.minimum((i + 1) * hb, nh - 1), 0)),
                  _const_spec(sa.shape), _const_spec(sb.shape),
                  _const_spec(w.shape), _const_spec((1, dc)), _const_spec((1, dc)), _const_spec((1, dc))],
        out_specs=pl.BlockSpec((1, tm, dc), lambda b, i: (b, i, 0)),
        out_shape=jax.ShapeDtypeStruct((bsz, l, dc), BF16),
        scratch_shapes=[pltpu.VMEM((SUBLANES, tm + 2 * HALO, dc), F32), pltpu.VMEM((tm, dc), F32)],
        compiler_params=_params(("arbitrary", "arbitrary")),
        name="conv_branch",
    )(u, u, u, sa, sb, w, b, lg, lb)


def _softmax_pv(parts):
    m = None
    for s, _ in parts:
        mi = jnp.max(s, axis=-1, keepdims=True)
        m = mi if m is None else jnp.maximum(m, mi)
    colsum = None
    acc = None
    for s, v in parts:
        p = jnp.exp((s - m).astype(BF16))
        for c0 in range(0, p.shape[1], LANES):
            blk = p[:, c0:c0 + LANES]
            colsum = blk if colsum is None else colsum + blk
        pv = _dot(p, v)
        acc = pv if acc is None else acc + pv
    den = jnp.sum(colsum.astype(F32), axis=-1, keepdims=True)
    return acc / den


def _attn_kernel(n_ctx, q_ref, k_ref, v_ref, strip_ref, o_ref, s_ref):
    j = pl.program_id(1)
    tq = q_ref.shape[1]
    rows_per_step = tq // GRID_W
    n_rows = (k_ref.shape[1] - n_ctx) // GRID_W
    n_loc = NA_ROWS * GRID_W
    pairs = N_HEADS // 2

    def split_heads(q2):
        lane = lax.broadcasted_iota(jnp.int32, q2.shape, 1)
        zero = jnp.zeros_like(q2)
        return jnp.concatenate([jnp.where(lane < HEAD_DIM, q2, zero),
                                jnp.where(lane >= HEAD_DIM, q2, zero)], axis=0)

    def merge_heads(o, m):
        lane = lax.broadcasted_iota(jnp.int32, (m, LANES), 1)
        return jnp.where(lane < HEAD_DIM, o[0:m], o[m:2 * m])

    @pl.when(j == 0)
    def _():
        for hp in range(pairs):
            cs = slice(hp * LANES, (hp + 1) * LANES)
            qs = split_heads(q_ref[0, :, cs])
            s = _dot_nt(qs, k_ref[0, 0:n_ctx, cs])
            o = _softmax_pv([(s, v_ref[0, 0:n_ctx, cs])])
            o_ref[0, :, cs] = merge_heads(o, tq).astype(BF16)

    @pl.when(j > 0)
    def _():
        rows2 = 2 * GRID_W
        starts = []
        for rr in range(rows_per_step):
            r = (j - 1) * rows_per_step + rr
            rs = jnp.clip(r - NA_ROWS // 2, 0, n_rows - NA_ROWS)
            starts.append((rs - r + (NA_ROWS - 1), pl.multiple_of(n_ctx + rs * GRID_W, GRID_W)))
        for hp in range(pairs):
            cs = slice(hp * LANES, (hp + 1) * LANES)
            qs_all = jnp.concatenate([split_heads(q_ref[0, rr * GRID_W:(rr + 1) * GRID_W, cs])
                                      for rr in range(rows_per_step)], axis=0)
            s_ctx = _dot_nt(qs_all, k_ref[0, 0:n_ctx, cs])
            for rr in range(rows_per_step):
                case, kstart = starts[rr]
                slot = rr * pairs + hp
                bias = jnp.concatenate([strip_ref[2 * hp, case], strip_ref[2 * hp + 1, case]], axis=0)
                s_ref[slot, :, 0:n_loc] = _dot_nt(qs_all[rr * rows2:(rr + 1) * rows2],
                                                  k_ref[0, pl.ds(kstart, n_loc), cs]) + bias
                s_ref[slot, :, n_loc:n_loc + n_ctx] = s_ctx[rr * rows2:(rr + 1) * rows2]
        for rr in range(rows_per_step):
            _, kstart = starts[rr]
            rsl = slice(rr * GRID_W, (rr + 1) * GRID_W)
            for hp in range(pairs):
                cs = slice(hp * LANES, (hp + 1) * LANES)
                slot = rr * pairs + hp
                o = _softmax_pv([(s_ref[slot, :, 0:n_loc], v_ref[0, pl.ds(kstart, n_loc), cs]),
                                 (s_ref[slot, :, n_loc:n_loc + n_ctx], v_ref[0, 0:n_ctx, cs])])
                o_ref[0, rsl, cs] = merge_heads(o, GRID_W).astype(BF16)


def _attention(q, k, v, strips, n_ctx):
    bsz, l, dn = q.shape
    tq = TOKEN_TILE
    whole = lambda: pl.BlockSpec((1, l, dn), lambda b, j: (b, 0, 0), pipeline_mode=pl.Buffered(1))
    return pl.pallas_call(
        functools.partial(_attn_kernel, n_ctx),
        grid=(bsz, l // tq),
        in_specs=[pl.BlockSpec((1, tq, dn), lambda b, j: (b, j, 0)), whole(), whole(),
                  pl.BlockSpec(strips.shape, lambda b, j: (0, 0, 0, 0), pipeline_mode=pl.Buffered(1))],
        out_specs=pl.BlockSpec((1, tq, dn), lambda b, j: (b, j, 0)),
        out_shape=jax.ShapeDtypeStruct((bsz, l, dn), BF16),
        scratch_shapes=[pltpu.VMEM(((tq // GRID_W) * (N_HEADS // 2), 2 * GRID_W, NA_ROWS * GRID_W + n_ctx), F32)],
        compiler_params=_params(("arbitrary", "arbitrary")),
        name="attention",
    )(q, k, v, strips)


def _bias_strips(rpb):
    col = np.arange(GRID_W)
    start = np.clip(col - NA_COLS // 2, 0, GRID_W - NA_COLS)
    mask = (col[None, :] >= start[:, None]) & (col[None, :] < start[:, None] + NA_COLS)
    dc_idx = np.clip(col[None, :] - col[:, None] + NA_COLS - 1, 0, 2 * NA_COLS - 2)
    pick = jnp.asarray(dc_idx[None] == np.arange(2 * NA_COLS - 1)[:, None, None], F32)
    bias = jnp.einsum("hrc,cqk->hrqk", rpb.astype(F32), pick, precision=lax.Precision.HIGHEST)
    bias = jnp.where(jnp.asarray(mask)[None, None], bias, NEG_INF)
    cases = [jnp.concatenate([bias[:, c + j] for j in range(NA_ROWS)], axis=-1) for c in range(NA_ROWS)]
    return jnp.stack(cases, axis=1)


def _scan_permutation(tc):
    p = np.arange(tc)
    t = (p % SUBLANES) * (tc // SUBLANES) + p // SUBLANES
    m = np.zeros((tc, tc), np.float32)
    m[p, t] = 1.0
    return jnp.asarray(m, BF16), jnp.asarray(m.T, BF16)


def _rglru_kernel(xf_ref, xb_ref, p_ref, pt_ref, cw_ref, cb_ref, wg_ref, bg_ref, nsp_ref, of_ref, ob_ref,
                  edge_ref, h_ref):
    j = pl.program_id(1)

    @pl.when(j == 0)
    def _():
        h_ref[...] = jnp.zeros_like(h_ref)

    @pl.when(j <= 1)
    def _():
        edge_ref[...] = jnp.zeros_like(edge_ref)

    chains = [_rglru_chunk(dirn == 1, x_ref[0], p_ref, pt_ref, cw_ref.at[dirn], cb_ref.at[dirn],
                           wg_ref.at[dirn], bg_ref.at[dirn], nsp_ref.at[dirn],
                           edge_ref.at[dirn], h_ref.at[dirn])
              for dirn, x_ref in enumerate((xf_ref, xb_ref))]
    outs = [None, None]
    for _ in range(RGLRU_PHASES):
        outs = [next(chain) for chain in chains]
    of_ref[0] = outs[0]
    ob_ref[0] = outs[1]


RGLRU_PHASES = 6


def _rglru_chunk(reverse, x, p_ref, pt_ref, cw_ref, cb_ref, wg_ref, bg_ref, nsp_ref, edge_ref, h_ref):
    tc, dr = x.shape
    sl = SUBLANES
    nv = tc // sl
    reach = (RNN_CONV_K - 1) * sl
    xp = _dot(p_ref[...], x)
    yield None
    sub = lax.broadcasted_iota(jnp.int32, (sl, dr), 0)
    prev = edge_ref[...]
    if not reverse:
        edge = xp[tc - reach:tc]
        fixed = jnp.concatenate(
            [jnp.where(sub == 0, pltpu.roll(prev[v * sl:(v + 1) * sl], 1, 0),
                       pltpu.roll(edge[v * sl:(v + 1) * sl], 1, 0)) for v in range(RNN_CONV_K - 1)], axis=0)
        shifted = [xp] + [jnp.concatenate([fixed[reach - d * sl:reach], xp[0:tc - d * sl]], axis=0)
                          for d in range(1, RNN_CONV_K)]
    else:
        edge = xp[0:reach]
        fixed = jnp.concatenate(
            [jnp.where(sub == sl - 1, pltpu.roll(prev[v * sl:(v + 1) * sl], sl - 1, 0),
                       pltpu.roll(edge[v * sl:(v + 1) * sl], sl - 1, 0)) for v in range(RNN_CONV_K - 1)], axis=0)
        shifted = [xp] + [jnp.concatenate([xp[d * sl:tc], fixed[0:d * sl]], axis=0)
                          for d in range(1, RNN_CONV_K)]
    edge_ref[...] = edge
    u = jnp.zeros((tc, dr), F32) + cb_ref[...]
    for kk in range(RNN_CONV_K):
        u = u + shifted[RNN_CONV_K - 1 - kk] * cw_ref[kk:kk + 1, :]
    yield None

    z = _dot(u.astype(BF16), wg_ref[...]) + bg_ref[...]
    yield None
    r = _sigmoid(z[:, 0:dr])
    ig = _sigmoid(z[:, dr:2 * dr])
    a = jnp.exp(r * nsp_ref[...])
    bx = jnp.sqrt(1.0 - a * a) * (ig * u)
    yield None

    order = list(range(nv)) if not reverse else list(range(nv - 1, -1, -1))
    hs, prods = {}, {}
    h = None
    prod = None
    for v in order:
        av = a[v * sl:(v + 1) * sl]
        bv = bx[v * sl:(v + 1) * sl]
        h = bv if h is None else av * h + bv
        prod = av if prod is None else av * prod
        hs[v], prods[v] = h, prod
    h_end, p_end = hs[order[-1]], prods[order[-1]]
    c = h_ref[...]
    carry = {}
    for s in (range(sl) if not reverse else range(sl - 1, -1, -1)):
        carry[s] = c
        c = p_end[s:s + 1] * c + h_end[s:s + 1]
    h_ref[...] = c
    cin = jnp.concatenate([carry[s] for s in range(sl)], axis=0)
    out = jnp.concatenate([hs[v] + prods[v] * cin for v in range(nv)], axis=0)
    yield None
    yield _dot(pt_ref[...], out.astype(BF16)).astype(BF16)


def _rglru(rx, perm, perm_t, cw, cb, wg, bg, nsp):
    bsz, l, dr = rx.shape
    tc = TOKEN_TILE
    n = l // tc
    fwd = pl.BlockSpec((1, tc, dr), lambda b, j: (b, j, 0))
    bwd = pl.BlockSpec((1, tc, dr), lambda b, j: (b, jnp.where(j == 0, 0, n - j), 0))
    out = jax.ShapeDtypeStruct((bsz, l, dr), BF16)
    return pl.pallas_call(
        _rglru_kernel,
        grid=(bsz, n),
        in_specs=[fwd, bwd, _const_spec(perm.shape), _const_spec(perm_t.shape), _const_spec(cw.shape),
                  _const_spec(cb.shape), _const_spec(wg.shape), _const_spec(bg.shape), _const_spec(nsp.shape)],
        out_specs=[fwd, bwd],
        out_shape=[out, out],
        scratch_shapes=[pltpu.VMEM((2, (RNN_CONV_K - 1) * SUBLANES, dr), F32), pltpu.VMEM((2, 1, dr), F32)],
        compiler_params=_params(("arbitrary", "arbitrary")),
        name="rglru",
    )(rx, rx, perm, perm_t, cw, cb, wg, bg, nsp)


def _route(h, rw_ref, rb_ref, tri_ref):
    tm = h.shape[0]
    hi = h.astype(BF16)
    lo = (h - hi.astype(F32)).astype(BF16)
    d = h.shape[1]
    parts = _dot(hi, rw_ref[0:d, :]) + _dot(lo, rw_ref[d:2 * d, :])
    parts_t = parts.T
    logits = parts_t[0:N_EXPERTS] + parts_t[N_EXPERTS:2 * N_EXPERTS]
    scores = _sigmoid(logits)
    sel = scores + rb_ref[...]
    row = lambda t, e: t[e:e + 1, :]
    epg = EXPERTS_PER_GROUP
    gscore = []
    for g in range(N_GROUPS):
        best = None
        for a in range(epg):
            for b in range(a + 1, epg):
                pair = row(sel, g * epg + a) + row(sel, g * epg + b)
                best = pair if best is None else jnp.maximum(best, pair)
        gscore.append(best)
    gbest = gscore[0]
    gidx = jnp.zeros_like(gbest, dtype=jnp.int32)
    for g in range(1, N_GROUPS):
        better = gscore[g] > gbest
        gidx = jnp.where(better, g, gidx)
        gbest = jnp.where(better, gscore[g], gbest)
    picked = []
    flags = []
    for e in range(N_EXPERTS):
        g = e // epg
        rank = jnp.zeros_like(gidx)
        for o in range(g * epg, (g + 1) * epg):
            if o == e:
                continue
            ahead = (row(sel, o) > row(sel, e)) if o > e else (row(sel, o) >= row(sel, e))
            rank = rank + ahead.astype(jnp.int32)
        chosen = jnp.logical_and(gidx == g, rank < 2)
        picked.append(jnp.where(chosen, row(scores, e), 0.0))
        flags.append(jnp.where(chosen, 1.0, 0.0))
    den = picked[0]
    for e in range(1, N_EXPERTS):
        den = den + picked[e]
    gate = jnp.concatenate(picked, axis=0) / den
    flag = jnp.concatenate(flags, axis=0)

    cnt = jnp.sum(flag, axis=1, keepdims=True).astype(jnp.int32)
    pc = jnp.left_shift(jnp.right_shift(cnt + (ROW_ALIGN - 1), ROW_SHIFT), ROW_SHIFT)
    starts = [jnp.zeros((1, 1), jnp.int32)]
    for e in range(1, N_EXPERTS):
        starts.append(starts[-1] + pc[e - 1:e, :])
    seg0 = jnp.concatenate(starts, axis=0).astype(F32)
    pos = seg0 + _dot(flag.astype(BF16), tri_ref[...])
    used = flag > 0.5
    pos_lo = jnp.min(jnp.where(used, pos, 1e9), axis=0, keepdims=True)
    pos_hi = jnp.max(jnp.where(used, pos, -1.0), axis=0, keepdims=True)
    w_lo = jnp.sum(jnp.where(pos == pos_lo, gate, 0.0), axis=0, keepdims=True)
    w_hi = jnp.sum(jnp.where(pos == pos_hi, gate, 0.0), axis=0, keepdims=True)
    info = jnp.concatenate([pos_lo, pos_hi, w_lo, w_hi, jnp.zeros((4, tm), F32)], axis=0)
    return hi, info, jnp.broadcast_to(cnt, (N_EXPERTS, LANES))


MERGE_TILES = 2


def _merge_kernel(n_src, n_ctx_tiles, tiles_per_batch, *refs):
    n_in = n_src + 9
    shared = refs[MERGE_TILES * n_in:MERGE_TILES * n_in + 8]
    n2_ref, pw_ref, no_ref, ro_ref, ow_ref, rw_ref, rb_ref, tri_ref = shared
    o_ref, h2_ref, info_ref, cnt_ref = refs[MERGE_TILES * n_in + 8:]
    d = o_ref.shape[1]
    tm = o_ref.shape[0] // MERGE_TILES
    chains = []
    for h in range(MERGE_TILES):
        tile_refs = refs[h * n_in:(h + 1) * n_in]
        chains.append((h, tile_refs[:n_src]) + tuple(tile_refs[n_src:]))
    branch = []
    for h, x_refs, g1_ref, sh2_ref, sc2_ref, cv_ref, at_ref, hf_ref, hb_ref, rg_ref, gt_ref in chains:
        hsum = hf_ref[0].astype(F32) + hb_ref[0].astype(F32)
        branch.append((_dot(cv_ref[0], pw_ref[0]), _dot(at_ref[0], no_ref[0]),
                       _dot((rg_ref[0].astype(F32) * hsum).astype(BF16), ro_ref[0])))
    ys = []
    for (conv, na, rnn), chain in zip(branch, chains):
        gt_ref = chain[-1]
        m = (gt_ref[0, :, 0:d].astype(F32) * conv + gt_ref[0, :, d:2 * d].astype(F32) * na
             + gt_ref[0, :, 2 * d:3 * d].astype(F32) * rnn)
        ys.append(_dot(m.astype(BF16), ow_ref[0]))
    hms = []
    for y, (h, x_refs, g1_ref, sh2_ref, sc2_ref, *_) in zip(ys, chains):
        if n_src == 1:
            x_in = x_refs[0][0]
        else:
            tile = lax.rem(pl.program_id(0) * MERGE_TILES + h, tiles_per_batch)
            x_in = jnp.where(tile < n_ctx_tiles, x_refs[0][0], x_refs[1][0])
        x = x_in + g1_ref[0] * y
        o_ref[h * tm:(h + 1) * tm, :] = x
        hm = x * lax.rsqrt(jnp.mean(x * x, axis=-1, keepdims=True) + EPS) * n2_ref[...]
        hms.append(hm * (1.0 + sc2_ref[0]) + sh2_ref[0])
    for h, hm in enumerate(hms):
        h2, info, cnt = _route(hm, rw_ref, rb_ref, tri_ref)
        h2_ref[h * tm:(h + 1) * tm, :] = h2
        info_ref[:, h * tm:(h + 1) * tm] = info
        cnt_ref[h] = cnt


def _merge(src, modtab, n2, cv, at, hf, hb, rg, gt, pw, no, ro, ow, layer, rw, rb, tri):
    bsz, l, _ = cv.shape
    d = ow.shape[2]
    tm = TOKEN_TILE
    nt = l // tm
    assert (bsz * nt) % MERGE_TILES == 0
    is_pair = isinstance(src, tuple)
    nc = src[0].shape[1] // tm if is_pair else 0
    in_specs, args = [], []
    for h in range(MERGE_TILES):
        def where(k, h=h):
            t = k * MERGE_TILES + h
            return t // nt, lax.rem(t, nt), t

        def tile(n, where=where):
            return pl.BlockSpec((1, tm, n), lambda k: where(k)[:2] + (0,))

        def mod(which, where=where):
            def index(k):
                b, i, _ = where(k)
                return ((2 * b + jnp.minimum(i, 1)) * N_MOD + which, 0, 0)
            return pl.BlockSpec((1, 1, d), index)

        if is_pair:
            in_specs += [pl.BlockSpec((1, tm, d), lambda k, where=where: (where(k)[0], jnp.minimum(where(k)[1], nc - 1), 0)),
                         pl.BlockSpec((1, tm, d), lambda k, where=where: (where(k)[0], jnp.maximum(where(k)[1] - nc, 0), 0))]
            args += list(src)
        else:
            in_specs.append(tile(d))
            args.append(src)
        in_specs += [mod(2), mod(3), mod(4), tile(cv.shape[2]), tile(at.shape[2]), tile(hf.shape[2]),
                     tile(hb.shape[2]), tile(rg.shape[2]), tile(3 * d)]
        args += [modtab, modtab, modtab, cv, at, hf, hb, rg, gt]
    in_specs += [_const_spec((1, d)), _layer_spec(pw, layer), _layer_spec(no, layer), _layer_spec(ro, layer),
                 _layer_spec(ow, layer), _const_spec(rw.shape), _const_spec(rb.shape), _const_spec(tri.shape)]
    args += [n2, pw, no, ro, ow, rw, rb, tri]
    rows = MERGE_TILES * tm
    xa, h2, info, cnt = pl.pallas_call(
        functools.partial(_merge_kernel, 2 if is_pair else 1, nc, nt),
        grid=(bsz * nt // MERGE_TILES,),
        in_specs=in_specs,
        out_specs=[pl.BlockSpec((rows, d), lambda k: (k, 0)), pl.BlockSpec((rows, d), lambda k: (k, 0)),
                   pl.BlockSpec((SUBLANES, rows), lambda k: (0, k)),
                   pl.BlockSpec((MERGE_TILES, N_EXPERTS, LANES), lambda k: (k, 0, 0))],
        out_shape=[jax.ShapeDtypeStruct((bsz * l, d), F32), jax.ShapeDtypeStruct((bsz * l, d), BF16),
                   jax.ShapeDtypeStruct((SUBLANES, bsz * l), F32),
                   jax.ShapeDtypeStruct((bsz * nt, N_EXPERTS, LANES), jnp.int32)],
        compiler_params=_params(("arbitrary",)),
        name="merge_route",
    )(*args)
    return xa.reshape(bsz, l, d), h2.reshape(bsz, l, d), info, cnt


def _route_plan(cnt, n_tiles):
    pc = (cnt + (ROW_ALIGN - 1)) // ROW_ALIGN * ROW_ALIGN
    ls = jnp.cumsum(pc, axis=1) - pc
    seg_tot = jnp.sum(pc, axis=0)
    reg = (seg_tot + (EXPERT_TILE - 1)) // EXPERT_TILE * EXPERT_TILE
    reg_end = jnp.cumsum(reg)
    base = reg_end - reg
    gs = base[None, :] + jnp.cumsum(pc, axis=0) - pc
    n_used = reg_end[-1] // EXPERT_TILE
    tiles = jnp.minimum(jnp.arange(n_tiles, dtype=jnp.int32), n_used - 1)
    tile_expert = jnp.minimum(jnp.sum(tiles[:, None] >= (reg_end // EXPERT_TILE)[None, :], axis=1),
                              N_EXPERTS - 1).astype(jnp.int32)
    i32 = lambda a: a.astype(jnp.int32).reshape(-1)
    return dict(pc=i32(pc), ls=i32(ls), gs=i32(gs), tot=i32(jnp.sum(pc, axis=1)),
                tail_start=i32(base + seg_tot), tail_len=i32(reg - seg_tot),
                tile_expert=tile_expert, n_used=i32(n_used))


def _for_each_piece(n, sizes, fn):
    off = jnp.int32(0)
    for size in sizes:
        hit = (n & size) != 0

        @pl.when(hit)
        def _(off=off, size=size):
            fn(pl.multiple_of(off, ROW_ALIGN), size)

        off = off + jnp.where(hit, size, 0)


def _window_copies(pc_ref, ls_ref, gs_ref, win, staged_at, sorted_at, to_sorted, sem, fn):
    for e in range(N_EXPERTS):
        n = pc_ref[win * N_EXPERTS + e]
        src0 = ls_ref[win * N_EXPERTS + e]
        dst0 = gs_ref[win * N_EXPERTS + e]

        def piece(off, size, src0=src0, dst0=dst0):
            a = staged_at(pl.multiple_of(src0 + off, ROW_ALIGN), size)
            b = sorted_at(pl.multiple_of(dst0 + off, ROW_ALIGN), size)
            fn(pltpu.make_async_copy(a, b, sem) if to_sorted else pltpu.make_async_copy(b, a, sem))

        small = n & (SEG_SPLIT - 1)
        _for_each_piece(small, tuple(s for s in SEG_SIZES if s < SEG_SPLIT), piece)

        @pl.when(n >= SEG_SPLIT)
        def _(n=n, small=small, piece=piece):
            _for_each_piece(n - small, tuple(s for s in SEG_SIZES if s >= SEG_SPLIT),
                            lambda off, size: piece(off + small, size))


def _wait_rows(n, desc):
    for size in WAIT_SIZES:
        @pl.when((n & size) != 0)
        def _(size=size):
            desc(size).wait()


def _one_hot_rows(info_ref, rows, lo_val, hi_val):
    tm = info_ref.shape[1]
    rid = lax.broadcasted_iota(jnp.int32, (rows, tm), 0).astype(F32)
    return jnp.where(rid == info_ref[0:1, :], lo_val, jnp.where(rid == info_ref[1:2, :], hi_val, 0.0))


def _dispatch_kernel(pc_ref, ls_ref, gs_ref, tot_ref, ts_ref, tl_ref, nu_ref, h_ref, info_ref, xs_ref,
                     buf_ref, zero_ref, sem_ref, zsem_ref):
    w = pl.program_id(0)
    nw = pl.num_programs(0)
    n_slots = buf_ref.shape[0]
    slot_of = lambda win: lax.rem(win, n_slots)
    slot = slot_of(w)
    rows = buf_ref.shape[1]
    p = _one_hot_rows(info_ref, rows, 1.0, 1.0).astype(BF16)
    buf_ref[slot] = _dot(p, h_ref[...]).astype(BF16)

    def start_copies(win):
        sl = slot_of(win)
        _window_copies(pc_ref, ls_ref, gs_ref, win,
                       lambda r, n: buf_ref.at[sl, pl.ds(r, n)], lambda r, n: xs_ref.at[pl.ds(r, n)],
                       True, sem_ref.at[sl], lambda c: c.start())

    def wait_copies(win):
        sl = slot_of(win)
        _wait_rows(tot_ref[win], lambda n: pltpu.make_async_copy(
            buf_ref.at[sl, pl.ds(0, n)], xs_ref.at[pl.ds(0, n)], sem_ref.at[sl]))

    lag = n_slots - 1

    @pl.when(w >= lag)
    def _():
        wait_copies(w - lag)

    start_copies(w)

    @pl.when(w == nw - 1)
    def _():
        for back in range(lag - 1, -1, -1):
            @pl.when(w >= back)
            def _(back=back):
                wait_copies(w - back)
        zero_ref[...] = jnp.zeros_like(zero_ref)

        def tails(fn):
            for e in range(N_EXPERTS):
                start = ts_ref[e]

                def piece(off, size, start=start):
                    fn(pltpu.make_async_copy(zero_ref.at[pl.ds(0, size)],
                                             xs_ref.at[pl.ds(pl.multiple_of(start + off, ROW_ALIGN), size)],
                                             zsem_ref.at[0]))

                _for_each_piece(tl_ref[e], TAIL_SIZES, piece)

        tails(lambda c: c.start())
        tails(lambda c: c.wait())

        fill = zero_ref.shape[0]
        used = nu_ref[0] * EXPERT_TILE

        def spare(k):
            return pltpu.make_async_copy(
                zero_ref, xs_ref.at[pl.ds(pl.multiple_of(used + k * fill, ROW_ALIGN), fill)], zsem_ref.at[0])

        n_fill = (xs_ref.shape[0] - used) // fill
        lax.fori_loop(0, n_fill, lambda k, c: (spare(k).start(), c)[1], 0)
        lax.fori_loop(0, n_fill, lambda k, c: (spare(k).wait(), c)[1], 0)


def _dispatch(h2, info, plan, n_rows):
    t, d = h2.shape
    tm = TOKEN_TILE
    rows = STAGE_ROWS
    grid_spec = pltpu.PrefetchScalarGridSpec(
        num_scalar_prefetch=7,
        grid=(t // tm,),
        in_specs=[pl.BlockSpec((tm, d), lambda w, *_: (w, 0)),
                  pl.BlockSpec((SUBLANES, tm), lambda w, *_: (0, w))],
        out_specs=pl.BlockSpec(memory_space=pl.ANY),
        scratch_shapes=[pltpu.VMEM((MOE_SLOTS, rows, d), BF16), pltpu.VMEM((TAIL_SIZES[0], d), BF16),
                        pltpu.SemaphoreType.DMA((MOE_SLOTS,)), pltpu.SemaphoreType.DMA((1,))])
    return pl.pallas_call(
        _dispatch_kernel,
        grid_spec=grid_spec,
        out_shape=jax.ShapeDtypeStruct((n_rows, d), BF16),
        compiler_params=_params(("arbitrary",)),
        name="moe_dispatch",
    )(plan["pc"], plan["ls"], plan["gs"], plan["tot"], plan["tail_start"], plan["tail_len"], plan["n_used"],
      h2, info)


def _experts_kernel(te_ref, nu_ref, xs_ref, w1_ref, w3_ref, w2_ref, ys_ref, w13_ref, w2b_ref, z_ref):
    i = pl.program_id(0)
    de = w1_ref.shape[3]
    prev = te_ref[jnp.maximum(i - 1, 0)]

    @pl.when(jnp.logical_or(i == 0, te_ref[i] != prev))
    def _():
        w13_ref[:, 0:de] = w1_ref[0, 0].astype(BF16)
        w13_ref[:, de:2 * de] = w3_ref[0, 0].astype(BF16)
        w2b_ref[...] = w2_ref[0, 0].astype(BF16)

    @pl.when(i >= nu_ref[0])
    def _():
        ys_ref[...] = jnp.zeros_like(ys_ref)

    @pl.when(i < nu_ref[0])
    def _():
        for blk, r0 in enumerate(range(0, xs_ref.shape[0], EXPERT_SUB)):
            z_ref[blk] = _dot(xs_ref[r0:r0 + EXPERT_SUB, :], w13_ref[...])
        for blk, r0 in enumerate(range(0, xs_ref.shape[0], EXPERT_SUB)):
            he = _silu(z_ref[blk, :, 0:de]) * z_ref[blk, :, de:2 * de]
            ys_ref[r0:r0 + EXPERT_SUB, :] = _dot(he.astype(BF16), w2b_ref[...]).astype(BF16)


def _experts(xs, plan, w1, w3, w2, layer):
    n_rows, d = xs.shape
    de = w1.shape[3]
    te, nu = plan["tile_expert"], plan["n_used"]
    tm = EXPERT_TILE
    grid_spec = pltpu.PrefetchScalarGridSpec(
        num_scalar_prefetch=2,
        grid=(n_rows // tm,),
        in_specs=[pl.BlockSpec((tm, d), lambda i, te, nu: (jnp.minimum(i, nu[0] - 1), 0)),
                  pl.BlockSpec((1, 1, d, de), lambda i, te, nu: (layer, te[i], 0, 0)),
                  pl.BlockSpec((1, 1, d, de), lambda i, te, nu: (layer, te[i], 0, 0)),
                  pl.BlockSpec((1, 1, de, d), lambda i, te, nu: (layer, te[i], 0, 0))],
        out_specs=pl.BlockSpec((tm, d), lambda i, te, nu: (i, 0)),
        scratch_shapes=[pltpu.VMEM((d, 2 * de), BF16), pltpu.VMEM((de, d), BF16),
                        pltpu.VMEM((tm // EXPERT_SUB, EXPERT_SUB, 2 * de), F32)])
    return pl.pallas_call(
        _experts_kernel,
        grid_spec=grid_spec,
        out_shape=jax.ShapeDtypeStruct((n_rows, d), BF16),
        compiler_params=_params(("arbitrary",)),
        name="moe_experts",
    )(te, nu, xs, w1, w3, w2)


def _combine_kernel(pc_ref, ls_ref, gs_ref, tot_ref, x_ref, g2_ref, info_ref, ys_ref, o_ref, buf_ref, sem_ref):
    nt = pl.num_programs(1)
    w = pl.program_id(0) * nt + pl.program_id(1)
    nw = pl.num_programs(0) * nt
    n_slots = buf_ref.shape[0]
    slot = lax.rem(w, n_slots)
    rows = buf_ref.shape[1]
    ahead = n_slots - 1

    def start_copies(win):
        sl = lax.rem(win, n_slots)
        _window_copies(pc_ref, ls_ref, gs_ref, win,
                       lambda r, n: buf_ref.at[sl, pl.ds(r, n)], lambda r, n: ys_ref.at[pl.ds(r, n)],
                       False, sem_ref.at[sl], lambda c: c.start())

    @pl.when(w == 0)
    def _():
        buf_ref[...] = jnp.zeros_like(buf_ref)
        for first in range(ahead):
            @pl.when(first < nw)
            def _(first=first):
                start_copies(first)

    @pl.when(w + ahead < nw)
    def _():
        start_copies(w + ahead)

    _wait_rows(tot_ref[w], lambda n: pltpu.make_async_copy(
        ys_ref.at[pl.ds(0, n)], buf_ref.at[slot, pl.ds(0, n)], sem_ref.at[slot]))
    pw =_one_hot_rows(info_ref, rows, info_ref[2:3, :], info_ref[3:4, :]).astype(BF16)
    moe = lax.dot_general(pw, buf_ref[slot], (((0,), (0,)), ((), ())), preferred_element_type=F32)
    o_ref[0] = x_ref[0] + g2_ref[0] * moe


def _combine(xa, modtab, info, ys, plan, drop):
    bsz, l, d = xa.shape
    tm = TOKEN_TILE
    nt = l // tm
    grid_spec = pltpu.PrefetchScalarGridSpec(
        num_scalar_prefetch=4,
        grid=(bsz, nt),
        in_specs=[pl.BlockSpec((1, tm, d), lambda b, i, *_: (b, i, 0)),
                  _mod_spec(d, 5),
                  pl.BlockSpec((SUBLANES, tm), lambda b, i, *_: (0, b * nt + i)),
                  pl.BlockSpec(memory_space=pl.ANY)],
        out_specs=pl.BlockSpec((1, tm, d), lambda b, i, *_: (b, jnp.maximum(i - drop, 0), 0)),
        scratch_shapes=[pltpu.VMEM((MOE_SLOTS, STAGE_ROWS, d), BF16), pltpu.SemaphoreType.DMA((MOE_SLOTS,))])
    return pl.pallas_call(
        _combine_kernel,
        grid_spec=grid_spec,
        out_shape=jax.ShapeDtypeStruct((bsz, l - drop * tm, d), F32),
        compiler_params=_params(("arbitrary", "arbitrary")),
        name="moe_combine",
    )(plan["pc"], plan["ls"], plan["gs"], plan["tot"], xa, modtab, info, ys)


def _moe_sparse(xa, modtab, h2, info, cnt, w1, w3, w2, layer, drop):
    bsz, l, d = xa.shape
    t = bsz * l
    n_win = t // TOKEN_TILE
    bound = TOP_K * t + n_win * N_EXPERTS * (ROW_ALIGN - 1) + N_EXPERTS * (EXPERT_TILE - ROW_ALIGN)
    n_tiles = -(-bound // EXPERT_TILE)
    plan = _route_plan(cnt[:, :, 0], n_tiles)
    xs = _dispatch(h2.reshape(t, d), info, plan, n_tiles * EXPERT_TILE)
    ys = _experts(xs, plan, w1, w3, w2, layer)
    return _combine(xa, modtab, info, ys, plan, drop)


def _block_diag(w):
    n, k, _ = w.shape
    eye = jnp.eye(n, dtype=w.dtype)
    return jnp.einsum("nij,nm->nimj", w, eye).reshape(n * k, n * k)


def kernel(x, c, ctx, c_ctx, router_w, router_bias, mod_w, mod_b, norm1_g, norm2_g, in_w, in_b,
           conv_dw_w, conv_dw_b, conv_ln_g, conv_ln_b, conv_pw_w, na_q_g, na_k_g, na_rpb, na_out_w,
           rnn_conv_w, rnn_conv_b, rnn_wa, rnn_ba, rnn_wx, rnn_bx, rnn_lam, rnn_out_w, out_w,
           exp_w1, exp_w3, exp_w2):
    bsz, s, d = x.shape
    n_ctx = ctx.shape[1]
    depth = mod_w.shape[0]
    dc = conv_pw_w.shape[1]
    dn = na_out_w.shape[1]
    dr = rnn_out_w.shape[1]
    sizes = (dc, dn, dr, d)
    assert n_ctx % TOKEN_TILE == 0 and s % TOKEN_TILE == 0 and s % GRID_W == 0
    assert dn == N_HEADS * HEAD_DIM and s // GRID_W >= NA_ROWS

    xa = (ctx, x)
    seq = n_ctx + s

    c_rows = -(-(bsz + 1) // SUBLANES) * SUBLANES
    c_all = jnp.zeros((c_rows, d), F32).at[:bsz].set(c).at[bsz].set(c_ctx)
    mods = _mod_vectors(c_all, mod_w, mod_b)

    head_mean = _block_diag(jnp.full((N_HEADS, HEAD_DIM, HEAD_DIM), 1.0 / HEAD_DIM, F32)).astype(BF16)
    rw_hi = router_w.astype(BF16)
    rw_lo = (router_w - rw_hi.astype(F32)).astype(BF16)
    rw = jnp.zeros((2 * d, LANES), BF16)
    rw = rw.at[:d, 0:N_EXPERTS].set(rw_hi).at[:d, N_EXPERTS:2 * N_EXPERTS].set(rw_lo)
    rw = rw.at[d:, 0:N_EXPERTS].set(rw_hi)
    rb = router_bias.reshape(N_EXPERTS, 1)
    tri = jnp.asarray(np.triu(np.ones((TOKEN_TILE, TOKEN_TILE), np.float32), k=1), BF16)
    shift_a, shift_b = _conv_shift_matrices(TOKEN_TILE + 2 * HALO)
    perm, perm_t = _scan_permutation(TOKEN_TILE)

    in_w_b = in_w.astype(BF16)
    pw_b, no_b, ro_b, ow_b = (a.astype(BF16) for a in (conv_pw_w, na_out_w, rnn_out_w, out_w))

    for l in range(depth):
        ml = mods[l].reshape(c_rows, N_MOD, d)
        modtab = jnp.stack([jnp.broadcast_to(ml[bsz], (bsz, N_MOD, d)), ml[:bsz]], axis=1)
        modtab = modtab.reshape(bsz * 2 * N_MOD, 1, d)

        qg = (jnp.tile(na_q_g[l], N_HEADS) * HEAD_DIM ** -0.5).reshape(1, dn)
        kg = jnp.tile(na_k_g[l], N_HEADS).reshape(1, dn)
        u, q, k, v, rx, rg, gt = _inproj(xa, bsz, seq, modtab, norm1_g[l].reshape(1, d), in_w_b, l,
                                         in_b[l].reshape(1, -1), qg, kg, head_mean, sizes)

        cw = jnp.zeros((CONV_K + 1, dc), F32).at[:CONV_K].set(conv_dw_w[l])
        cv = _conv_branch(u, shift_a, shift_b, cw, conv_dw_b[l].reshape(1, dc),
                          conv_ln_g[l].reshape(1, dc), conv_ln_b[l].reshape(1, dc))

        at = _attention(q, k, v, _bias_strips(na_rpb[l]), n_ctx)

        wg = jnp.stack([jnp.concatenate([_block_diag(rnn_wa[l, dn_]), _block_diag(rnn_wx[l, dn_])], axis=1)
                        for dn_ in range(2)]).astype(BF16)
        bg = jnp.concatenate([rnn_ba[l], rnn_bx[l]], axis=-1).reshape(2, 1, 2 * dr)
        nsp = (-LRU_C * jax.nn.softplus(-rnn_lam[l])).reshape(2, 1, dr)
        rcw = jnp.zeros((2, SUBLANES, dr), F32).at[:, :RNN_CONV_K].set(rnn_conv_w[l])
        hf, hb = _rglru(rx, perm, perm_t, rcw, rnn_conv_b[l].reshape(2, 1, dr), wg, bg, nsp)

        xa, h2, info, cnt = _merge(xa, modtab, norm2_g[l].reshape(1, d), cv, at, hf, hb, rg, gt,
                                   pw_b, no_b, ro_b, ow_b, l, rw, rb, tri)

        drop = n_ctx // TOKEN_TILE if l == depth - 1 else 0
        xa = _moe_sparse(xa, modtab, h2, info, cnt, exp_w1, exp_w3, exp_w2, l, drop)

    return xa
```

```python
import functools
import math

import jax
import jax.numpy as jnp
import numpy as np
from jax import lax
from jax.experimental import pallas as pl
from jax.experimental.pallas import tpu as pltpu

F32 = jnp.float32
BF16 = jnp.bfloat16

GRID_W = 64
CONV_K = 31
N_HEADS = 8
HEAD_DIM = 64
NA_ROWS = 8
NA_COLS = 16
RNN_BLOCKS = 8
RNN_CONV_K = 4
LRU_C = 8.0
N_EXPERTS = 16
N_GROUPS = 4
EXPERTS_PER_GROUP = N_EXPERTS // N_GROUPS
N_MOD = 6
EPS = 1e-6
NEG_INF = -1e30

SUBLANES = 8
LANES = 128
MXU_DIM = 256
VMEM_LIMIT = 56 * 1024 * 1024

TOP_K = 2
TOKEN_TILE = 256
HALO = 16
ROW_ALIGN = 16
ROW_SHIFT = 4
EXPERT_TILE = 512
EXPERT_SUB = 256
SEG_SIZES = tuple(TOKEN_TILE >> k for k in range(TOKEN_TILE.bit_length() - ROW_SHIFT))
SEG_SPLIT = 128
MOE_SLOTS = 4
TAIL_SIZES = tuple((EXPERT_TILE // 2) >> k for k in range(EXPERT_TILE.bit_length() - 1 - ROW_SHIFT))
STAGE_ROWS = -(-(TOP_K * TOKEN_TILE + N_EXPERTS * (ROW_ALIGN - 1)) // MXU_DIM) * MXU_DIM
WAIT_SIZES = tuple(1 << k for k in range(STAGE_ROWS.bit_length() - 1, ROW_SHIFT - 1, -1))


def _params(sem, vmem=VMEM_LIMIT):
    return pltpu.CompilerParams(dimension_semantics=sem, vmem_limit_bytes=vmem)


def _sigmoid(x):
    return 0.5 * (1.0 + jnp.tanh(0.5 * x))


def _silu(x):
    return x * _sigmoid(x)


def _gelu_tanh(x):
    return x * (0.5 * (1.0 + jnp.tanh(math.sqrt(2.0 / math.pi) * (x + 0.044715 * (x * x * x)))))


def _dot(a, b):
    return jnp.dot(a, b, preferred_element_type=F32)


def _dot_nt(a, b):
    return lax.dot_general(a, b, (((1,), (1,)), ((), ())), preferred_element_type=F32)


def _const_spec(shape):
    return pl.BlockSpec(shape, lambda *_: (0,) * len(shape))


def _layer_spec(arr, layer):
    return pl.BlockSpec((1,) + arr.shape[1:], lambda *_: (layer,) + (0,) * (arr.ndim - 1))


def _mod_kernel(c_ref, w_ref, b_ref, o_ref):
    a = _silu(c_ref[...])
    o_ref[0] = jnp.dot(a, w_ref[0], preferred_element_type=F32,
                       precision=lax.Precision.HIGHEST) + b_ref[0]


def _mod_vectors(c_all, mod_w, mod_b):
    depth, d, n = mod_w.shape
    tn = 768
    rows = c_all.shape[0]
    return pl.pallas_call(
        _mod_kernel,
        grid=(depth, n // tn),
        in_specs=[pl.BlockSpec((rows, d), lambda l, j: (0, 0)),
                  pl.BlockSpec((1, d, tn), lambda l, j: (l, 0, j)),
                  pl.BlockSpec((1, 1, tn), lambda l, j: (l, 0, j))],
        out_specs=pl.BlockSpec((1, rows, tn), lambda l, j: (l, 0, j)),
        out_shape=jax.ShapeDtypeStruct((depth, rows, n), F32),
        compiler_params=_params(("arbitrary", "arbitrary")),
        name="mod_vectors",
    )(c_all, mod_w, mod_b.reshape(depth, 1, n))


def _mod_spec(d, which):
    return pl.BlockSpec((1, 1, d),
                        lambda b, i, *_: ((2 * b + jnp.minimum(i, 1)) * N_MOD + which, 0, 0))


def _stream_specs(src, tm):
    if not isinstance(src, tuple):
        return [pl.BlockSpec((1, tm, src.shape[2]), lambda b, i, *_: (b, i, 0))], [src], 0
    ctx, lat = src
    nc = ctx.shape[1] // tm
    d = ctx.shape[2]
    return ([pl.BlockSpec((1, tm, d), lambda b, i, *_: (b, jnp.minimum(i, nc - 1), 0)),
             pl.BlockSpec((1, tm, d), lambda b, i, *_: (b, jnp.maximum(i - nc, 0), 0))], [ctx, lat], nc)


def _stream_tile(x_refs, n_ctx_tiles):
    if len(x_refs) == 1:
        return x_refs[0][0]
    return jnp.where(pl.program_id(1) < n_ctx_tiles, x_refs[0][0], x_refs[1][0])


def _inproj_kernel(sizes, n_src, n_ctx_tiles, *refs):
    x_refs = refs[:n_src]
    (sh_ref, sc_ref, g_ref, w_ref, b_ref, qg_ref, kg_ref, bd_ref,
     u_ref, q_ref, k_ref, v_ref, rx_ref, rg_ref, gt_ref, z_ref) = refs[n_src:]
    dc, dn, dr, d = sizes
    x = _stream_tile(x_refs, n_ctx_tiles)
    h = x * lax.rsqrt(jnp.mean(x * x, axis=-1, keepdims=True) + EPS) * g_ref[...]
    h = h * (1.0 + sc_ref[0]) + sh_ref[0]
    hb = h.astype(BF16)

    d_in = w_ref.shape[2]
    for c0 in range(0, d_in, dn):
        z_ref[:, c0:c0 + dn] = _dot(hb, w_ref[0, :, c0:c0 + dn]) + b_ref[:, c0:c0 + dn]

    def proj(off, n):
        return z_ref[:, off:off + n]

    def head_norm(t, gain_ref):
        ms = _dot((t * t).astype(BF16), bd_ref[...])
        return t * lax.rsqrt(ms + EPS) * gain_ref[...]

    off = 0
    a = proj(off, dc); off += dc
    ga = proj(off, dc); off += dc
    u_ref[0] = (a * _sigmoid(ga)).astype(BF16)
    q_ref[0] = head_norm(proj(off, dn), qg_ref).astype(BF16); off += dn
    k_ref[0] = head_norm(proj(off, dn), kg_ref).astype(BF16); off += dn
    v_ref[0] = proj(off, dn).astype(BF16); off += dn
    rx_ref[0] = proj(off, dr).astype(BF16); off += dr
    rg_ref[0] = _gelu_tanh(proj(off, dr)).astype(BF16); off += dr
    for j in range(3):
        gt_ref[0, :, j * d:(j + 1) * d] = _sigmoid(proj(off, d)).astype(BF16); off += d


def _inproj(src, bsz, l, modtab, g, w, layer, b, qg, kg, bd, sizes):
    dc, dn, dr, d = sizes
    d_in = w.shape[2]
    tm = TOKEN_TILE
    x_specs, x_args, nc = _stream_specs(src, tm)
    tile = lambda n: pl.BlockSpec((1, tm, n), lambda b, i: (b, i, 0))
    out = lambda n: jax.ShapeDtypeStruct((bsz, l, n), BF16)
    return pl.pallas_call(
        functools.partial(_inproj_kernel, sizes, len(x_args), nc),
        grid=(bsz, l // tm),
        in_specs=x_specs + [_mod_spec(d, 0), _mod_spec(d, 1), _const_spec((1, d)),
                            pl.BlockSpec((1, d, d_in), lambda b, i: (layer, 0, 0), pipeline_mode=pl.Buffered(1)),
                            _const_spec((1, d_in)), _const_spec((1, dn)), _const_spec((1, dn)),
                            _const_spec((dn, dn))],
        out_specs=[tile(dc), tile(dn), tile(dn), tile(dn), tile(dr), tile(dr), tile(3 * d)],
        out_shape=[out(dc), out(dn), out(dn), out(dn), out(dr), out(dr), out(3 * d)],
        scratch_shapes=[pltpu.VMEM((tm, d_in), F32)],
        compiler_params=_params(("arbitrary", "arbitrary")),
        name="inproj",
    )(*x_args, modtab, modtab, g, w, b, qg, kg, bd)


CONV_SPLIT = 128


def _conv_shift_matrices(rows):
    ka = CONV_SPLIT + HALO
    nb = rows - CONV_SPLIT
    sa = np.zeros(((SUBLANES - 1) * CONV_SPLIT, ka), np.float32)
    sb = np.zeros(((SUBLANES - 1) * nb, nb), np.float32)
    for s in range(1, SUBLANES):
        for r in range(CONV_SPLIT):
            sa[(s - 1) * CONV_SPLIT + r, r + s] = 1.0
        for r in range(nb - s):
            sb[(s - 1) * nb + r, r + s] = 1.0
    return jnp.asarray(sa, BF16), jnp.asarray(sb, BF16)


def _conv_kernel(u_ref, prev_ref, next_ref, sa_ref, sb_ref, w_ref, b_ref, lg_ref, lb_ref, o_ref,
                 sh_ref, y_ref):
    i = pl.program_id(1)
    n = pl.num_programs(1)
    tm = u_ref.shape[1]
    dc = u_ref.shape[2]
    rows = tm + 2 * HALO
    ka = sa_ref.shape[1]
    nb = rows - CONV_SPLIT
    prev_ok = i >= 2
    next_ok = jnp.logical_and(i >= 1, i < n - 1)
    zero = jnp.zeros((HALO, dc), BF16)
    win = jnp.concatenate([jnp.where(prev_ok, prev_ref[0], zero), u_ref[0],
                           jnp.where(next_ok, next_ref[0], zero)], axis=0)
    sh_ref[0] = win.astype(F32)
    rc = 64
    base = HALO - CONV_K // 2
    for c0 in range(0, dc, MXU_DIM):
        hs = slice(c0, c0 + MXU_DIM)
        sha = _dot(sa_ref[...], win[0:ka, hs])
        shb = _dot(sb_ref[...], win[CONV_SPLIT:rows, hs])
        for s in range(1, SUBLANES):
            sh_ref[s, 0:CONV_SPLIT, hs] = sha[(s - 1) * CONV_SPLIT:s * CONV_SPLIT]
            sh_ref[s, CONV_SPLIT:rows, hs] = shb[(s - 1) * nb:s * nb]
    for cb in range(dc // LANES):
        cs = slice(cb * LANES, (cb + 1) * LANES)
        for r0 in range(0, tm, rc):
            acc = jnp.zeros((rc, LANES), F32)
            for k in range(CONV_K):
                o = base + k
                s, a = o % SUBLANES, o // SUBLANES
                acc = acc + sh_ref[s, r0 + a * SUBLANES:r0 + a * SUBLANES + rc, cs] * w_ref[k:k + 1, cs]
            y_ref[r0:r0 + rc, cs] = acc
    y = y_ref[...] + b_ref[...]
    mu = jnp.mean(y, axis=-1, keepdims=True)
    yc = y - mu
    var = jnp.mean(yc * yc, axis=-1, keepdims=True)
    z = yc * lax.rsqrt(var + EPS) * lg_ref[...] + lb_ref[...]
    o_ref[0] = _silu(z).astype(BF16)


def _conv_branch(u, sa, sb, w, b, lg, lb):
    bsz, l, dc = u.shape
    tm = TOKEN_TILE
    hb = tm // HALO
    nh = l // HALO
    return pl.pallas_call(
        _conv_kernel,
        grid=(bsz, l // tm),
        in_specs=[pl.BlockSpec((1, tm, dc), lambda b, i: (b, i, 0)),
                  pl.BlockSpec((1, HALO, dc), lambda b, i: (b, jnp.maximum(i * hb - 1, 0), 0)),
                  pl.BlockSpec((1, HALO, dc), lambda b, i: (b, jnp.minimum((i + 1) * hb, nh - 1), 0)),
                  _const_spec(sa.shape), _const_spec(sb.shape),
                  _const_spec(w.shape), _const_spec((1, dc)), _const_spec((1, dc)), _const_spec((1, dc))],
        out_specs=pl.BlockSpec((1, tm, dc), lambda b, i: (b, i, 0)),
        out_shape=jax.ShapeDtypeStruct((bsz, l, dc), BF16),
        scratch_shapes=[pltpu.VMEM((SUBLANES, tm + 2 * HALO, dc), F32), pltpu.VMEM((tm, dc), F32)],
        compiler_params=_params(("arbitrary", "arbitrary")),
        name="conv_branch",
    )(u, u, u, sa, sb, w, b, lg, lb)


def _softmax_pv(parts):
    m = None
    for s, _ in parts:
        mi = jnp.max(s, axis=-1, keepdims=True)
        m = mi if m is None else jnp.maximum(m, mi)
    colsum = None
    acc = None
    for s, v in parts:
        p = jnp.exp((s - m).astype(BF16))
        for c0 in range(0, p.shape[1], LANES):
            blk = p[:, c0:c0 + LANES]
            colsum = blk if colsum is None else colsum + blk
        pv = _dot(p, v)
        acc = pv if acc is None else acc + pv
    den = jnp.sum(colsum.astype(F32), axis=-1, keepdims=True)
    return acc / den


def _attn_kernel(n_ctx, q_ref, k_ref, v_ref, strip_ref, o_ref, s_ref):
    j = pl.program_id(1)
    tq = q_ref.shape[1]
    rows_per_step = tq // GRID_W
    n_rows = (k_ref.shape[1] - n_ctx) // GRID_W
    n_loc = NA_ROWS * GRID_W
    pairs = N_HEADS // 2

    def split_heads(q2):
        lane = lax.broadcasted_iota(jnp.int32, q2.shape, 1)
        zero = jnp.zeros_like(q2)
        return jnp.concatenate([jnp.where(lane < HEAD_DIM, q2, zero),
                                jnp.where(lane >= HEAD_DIM, q2, zero)], axis=0)

    def merge_heads(o, m):
        lane = lax.broadcasted_iota(jnp.int32, (m, LANES), 1)
        return jnp.where(lane < HEAD_DIM, o[0:m], o[m:2 * m])

    @pl.when(j == 0)
    def _():
        for hp in range(pairs):
            cs = slice(hp * LANES, (hp + 1) * LANES)
            qs = split_heads(q_ref[0, :, cs])
            s = _dot_nt(qs, k_ref[0, 0:n_ctx, cs])
            o = _softmax_pv([(s, v_ref[0, 0:n_ctx, cs])])
            o_ref[0, :, cs] = merge_heads(o, tq).astype(BF16)

    @pl.when(j > 0)
    def _():
        rows2 = 2 * GRID_W
        starts = []
        for rr in range(rows_per_step):
            r = (j - 1) * rows_per_step + rr
            rs = jnp.clip(r - NA_ROWS // 2, 0, n_rows - NA_ROWS)
            starts.append((rs - r + (NA_ROWS - 1), pl.multiple_of(n_ctx + rs * GRID_W, GRID_W)))
        for hp in range(pairs):
            cs = slice(hp * LANES, (hp + 1) * LANES)
            qs_all = jnp.concatenate([split_heads(q_ref[0, rr * GRID_W:(rr + 1) * GRID_W, cs])
                                      for rr in range(rows_per_step)], axis=0)
            s_ctx = _dot_nt(qs_all, k_ref[0, 0:n_ctx, cs])
            for rr in range(rows_per_step):
                case, kstart = starts[rr]
                slot = rr * pairs + hp
                bias = jnp.concatenate([strip_ref[2 * hp, case], strip_ref[2 * hp + 1, case]], axis=0)
                s_ref[slot, :, 0:n_loc] = _dot_nt(qs_all[rr * rows2:(rr + 1) * rows2],
                                                  k_ref[0, pl.ds(kstart, n_loc), cs]) + bias
                s_ref[slot, :, n_loc:n_loc + n_ctx] = s_ctx[rr * rows2:(rr + 1) * rows2]
        for rr in range(rows_per_step):
            _, kstart = starts[rr]
            rsl = slice(rr * GRID_W, (rr + 1) * GRID_W)
            for hp in range(pairs):
                cs = slice(hp * LANES, (hp + 1) * LANES)
                slot = rr * pairs + hp
                o = _softmax_pv([(s_ref[slot, :, 0:n_loc], v_ref[0, pl.ds(kstart, n_loc), cs]),
                                 (s_ref[slot, :, n_loc:n_loc + n_ctx], v_ref[0, 0:n_ctx, cs])])
                o_ref[0, rsl, cs] = merge_heads(o, GRID_W).astype(BF16)


def _attention(q, k, v, strips, n_ctx):
    bsz, l, dn = q.shape
    tq = TOKEN_TILE
    whole = lambda: pl.BlockSpec((1, l, dn), lambda b, j: (b, 0, 0), pipeline_mode=pl.Buffered(1))
    return pl.pallas_call(
        functools.partial(_attn_kernel, n_ctx),
        grid=(bsz, l // tq),
        in_specs=[pl.BlockSpec((1, tq, dn), lambda b, j: (b, j, 0)), whole(), whole(),
                  pl.BlockSpec(strips.shape, lambda b, j: (0, 0, 0, 0), pipeline_mode=pl.Buffered(1))],
        out_specs=pl.BlockSpec((1, tq, dn), lambda b, j: (b, j, 0)),
        out_shape=jax.ShapeDtypeStruct((bsz, l, dn), BF16),
        scratch_shapes=[pltpu.VMEM(((tq // GRID_W) * (N_HEADS // 2), 2 * GRID_W, NA_ROWS * GRID_W + n_ctx), F32)],
        compiler_params=_params(("arbitrary", "arbitrary")),
        name="attention",
    )(q, k, v, strips)


def _bias_strips(rpb):
    col = np.arange(GRID_W)
    start = np.clip(col - NA_COLS // 2, 0, GRID_W - NA_COLS)
    mask = (col[None, :] >= start[:, None]) & (col[None, :] < start[:, None] + NA_COLS)
    dc_idx = np.clip(col[None, :] - col[:, None] + NA_COLS - 1, 0, 2 * NA_COLS - 2)
    pick = jnp.asarray(dc_idx[None] == np.arange(2 * NA_COLS - 1)[:, None, None], F32)
    bias = jnp.einsum("hrc,cqk->hrqk", rpb.astype(F32), pick, precision=lax.Precision.HIGHEST)
    bias = jnp.where(jnp.asarray(mask)[None, None], bias, NEG_INF)
    windows = jnp.stack([bias[:, c:c + NA_ROWS] for c in range(NA_ROWS)], axis=1)
    strips = jnp.transpose(windows, (0, 1, 3, 2, 4))
    return strips.reshape(N_HEADS, NA_ROWS, GRID_W, NA_ROWS * GRID_W)


def _scan_permutation(tc):
    p = np.arange(tc)
    t = (p % SUBLANES) * (tc // SUBLANES) + p // SUBLANES
    m = np.zeros((tc, tc), np.float32)
    m[p, t] = 1.0
    return jnp.asarray(m, BF16), jnp.asarray(m.T, BF16)


def _rglru_kernel(xf_ref, xb_ref, p_ref, pt_ref, cw_ref, cb_ref, wg_ref, bg_ref, nsp_ref, of_ref, ob_ref,
                  edge_ref, h_ref):
    j = pl.program_id(1)

    @pl.when(j == 0)
    def _():
        h_ref[...] = jnp.zeros_like(h_ref)

    @pl.when(j <= 1)
    def _():
        edge_ref[...] = jnp.zeros_like(edge_ref)

    chains = [_rglru_chunk(dirn == 1, x_ref[0], p_ref, pt_ref, cw_ref.at[dirn], cb_ref.at[dirn],
                           wg_ref.at[dirn], bg_ref.at[dirn], nsp_ref.at[dirn],
                           edge_ref.at[dirn], h_ref.at[dirn])
              for dirn, x_ref in enumerate((xf_ref, xb_ref))]
    outs = [None, None]
    for _ in range(RGLRU_PHASES):
        outs = [next(chain) for chain in chains]
    of_ref[0] = outs[0]
    ob_ref[0] = outs[1]


RGLRU_PHASES = 6


def _rglru_chunk(reverse, x, p_ref, pt_ref, cw_ref, cb_ref, wg_ref, bg_ref, nsp_ref, edge_ref, h_ref):
    tc, dr = x.shape
    sl = SUBLANES
    nv = tc // sl
    reach = (RNN_CONV_K - 1) * sl
    xp = _dot(p_ref[...], x)
    yield None
    sub = lax.broadcasted_iota(jnp.int32, (sl, dr), 0)
    prev = edge_ref[...]
    if not reverse:
        edge = xp[tc - reach:tc]
        fixed = jnp.concatenate(
            [jnp.where(sub == 0, pltpu.roll(prev[v * sl:(v + 1) * sl], 1, 0),
                       pltpu.roll(edge[v * sl:(v + 1) * sl], 1, 0)) for v in range(RNN_CONV_K - 1)], axis=0)
        shifted = [xp] + [jnp.concatenate([fixed[reach - d * sl:reach], xp[0:tc - d * sl]], axis=0)
                          for d in range(1, RNN_CONV_K)]
    else:
        edge = xp[0:reach]
        fixed = jnp.concatenate(
            [jnp.where(sub == sl - 1, pltpu.roll(prev[v * sl:(v + 1) * sl], sl - 1, 0),
                       pltpu.roll(edge[v * sl:(v + 1) * sl], sl - 1, 0)) for v in range(RNN_CONV_K - 1)], axis=0)
        shifted = [xp] + [jnp.concatenate([xp[d * sl:tc], fixed[0:d * sl]], axis=0)
                          for d in range(1, RNN_CONV_K)]
    edge_ref[...] = edge
    u = jnp.zeros((tc, dr), F32) + cb_ref[...]
    for kk in range(RNN_CONV_K):
        u = u + shifted[RNN_CONV_K - 1 - kk] * cw_ref[kk:kk + 1, :]
    yield None

    z = _dot(u.astype(BF16), wg_ref[...]) + bg_ref[...]
    yield None
    r = _sigmoid(z[:, 0:dr])
    ig = _sigmoid(z[:, dr:2 * dr])
    a = jnp.exp(r * nsp_ref[...])
    bx = jnp.sqrt(1.0 - a * a) * (ig * u)
    yield None

    order = list(range(nv)) if not reverse else list(range(nv - 1, -1, -1))
    hs, prods = {}, {}
    h = None
    prod = None
    for v in order:
        av = a[v * sl:(v + 1) * sl]
        bv = bx[v * sl:(v + 1) * sl]
        h = bv if h is None else av * h + bv
        prod = av if prod is None else av * prod
        hs[v], prods[v] = h, prod
    h_end, p_end = hs[order[-1]], prods[order[-1]]
    c = h_ref[...]
    carry = {}
    for s in (range(sl) if not reverse else range(sl - 1, -1, -1)):
        carry[s] = c
        c = p_end[s:s + 1] * c + h_end[s:s + 1]
    h_ref[...] = c
    cin = jnp.concatenate([carry[s] for s in range(sl)], axis=0)
    out = jnp.concatenate([hs[v] + prods[v] * cin for v in range(nv)], axis=0)
    yield None
    yield _dot(pt_ref[...], out.astype(BF16)).astype(BF16)


def _rglru(rx, perm, perm_t, cw, cb, wg, bg, nsp):
    bsz, l, dr = rx.shape
    tc = TOKEN_TILE
    n = l // tc
    fwd = pl.BlockSpec((1, tc, dr), lambda b, j: (b, j, 0))
    bwd = pl.BlockSpec((1, tc, dr), lambda b, j: (b, jnp.where(j == 0, 0, n - j), 0))
    out = jax.ShapeDtypeStruct((bsz, l, dr), BF16)
    return pl.pallas_call(
        _rglru_kernel,
        grid=(bsz, n),
        in_specs=[fwd, bwd, _const_spec(perm.shape), _const_spec(perm_t.shape), _const_spec(cw.shape),
                  _const_spec(cb.shape), _const_spec(wg.shape), _const_spec(bg.shape), _const_spec(nsp.shape)],
        out_specs=[fwd, bwd],
        out_shape=[out, out],
        scratch_shapes=[pltpu.VMEM((2, (RNN_CONV_K - 1) * SUBLANES, dr), F32), pltpu.VMEM((2, 1, dr), F32)],
        compiler_params=_params(("arbitrary", "arbitrary")),
        name="rglru",
    )(rx, rx, perm, perm_t, cw, cb, wg, bg, nsp)


def _route(h, rw_ref, rb_ref, tri_ref):
    tm = h.shape[0]
    hi = h.astype(BF16)
    lo = (h - hi.astype(F32)).astype(BF16)
    d = h.shape[1]
    parts = _dot(hi, rw_ref[0:d, :]) + _dot(lo, rw_ref[d:2 * d, :])
    parts_t = parts.T
    logits = parts_t[0:N_EXPERTS] + parts_t[N_EXPERTS:2 * N_EXPERTS]
    scores = _sigmoid(logits)
    sel = scores + rb_ref[...]
    row = lambda t, e: t[e:e + 1, :]
    epg = EXPERTS_PER_GROUP
    gscore = []
    for g in range(N_GROUPS):
        best = None
        for a in range(epg):
            for b in range(a + 1, epg):
                pair = row(sel, g * epg + a) + row(sel, g * epg + b)
                best = pair if best is None else jnp.maximum(best, pair)
        gscore.append(best)
    gbest = gscore[0]
    gidx = jnp.zeros_like(gbest, dtype=jnp.int32)
    for g in range(1, N_GROUPS):
        better = gscore[g] > gbest
        gidx = jnp.where(better, g, gidx)
        gbest = jnp.where(better, gscore[g], gbest)
    picked = []
    flags = []
    for e in range(N_EXPERTS):
        g = e // epg
        rank = jnp.zeros_like(gidx)
        for o in range(g * epg, (g + 1) * epg):
            if o == e:
                continue
            ahead = (row(sel, o) > row(sel, e)) if o > e else (row(sel, o) >= row(sel, e))
            rank = rank + ahead.astype(jnp.int32)
        chosen = jnp.logical_and(gidx == g, rank < 2)
        picked.append(jnp.where(chosen, row(scores, e), 0.0))
        flags.append(jnp.where(chosen, 1.0, 0.0))
    den = picked[0]
    for e in range(1, N_EXPERTS):
        den = den + picked[e]
    gate = jnp.concatenate(picked, axis=0) / den
    flag = jnp.concatenate(flags, axis=0)

    cnt = jnp.sum(flag, axis=1, keepdims=True).astype(jnp.int32)
    pc = jnp.left_shift(jnp.right_shift(cnt + (ROW_ALIGN - 1), ROW_SHIFT), ROW_SHIFT)
    starts = [jnp.zeros((1, 1), jnp.int32)]
    for e in range(1, N_EXPERTS):
        starts.append(starts[-1] + pc[e - 1:e, :])
    seg0 = jnp.concatenate(starts, axis=0).astype(F32)
    pos = seg0 + _dot(flag.astype(BF16), tri_ref[...])
    used = flag > 0.5
    pos_lo = jnp.min(jnp.where(used, pos, 1e9), axis=0, keepdims=True)
    pos_hi = jnp.max(jnp.where(used, pos, -1.0), axis=0, keepdims=True)
    w_lo = jnp.sum(jnp.where(pos == pos_lo, gate, 0.0), axis=0, keepdims=True)
    w_hi = jnp.sum(jnp.where(pos == pos_hi, gate, 0.0), axis=0, keepdims=True)
    info = jnp.concatenate([pos_lo, pos_hi, w_lo, w_hi, jnp.zeros((4, tm), F32)], axis=0)
    return hi, info, jnp.broadcast_to(cnt, (N_EXPERTS, LANES))


MERGE_TILES = 2


def _merge_kernel(n_src, n_ctx_tiles, tiles_per_batch, *refs):
    n_in = n_src + 9
    shared = refs[MERGE_TILES * n_in:MERGE_TILES * n_in + 8]
    n2_ref, pw_ref, no_ref, ro_ref, ow_ref, rw_ref, rb_ref, tri_ref = shared
    o_ref, h2_ref, info_ref, cnt_ref = refs[MERGE_TILES * n_in + 8:]
    d = o_ref.shape[1]
    tm = o_ref.shape[0] // MERGE_TILES
    chains = []
    for h in range(MERGE_TILES):
        tile_refs = refs[h * n_in:(h + 1) * n_in]
        chains.append((h, tile_refs[:n_src]) + tuple(tile_refs[n_src:]))
    branch = []
    for h, x_refs, g1_ref, sh2_ref, sc2_ref, cv_ref, at_ref, hf_ref, hb_ref, rg_ref, gt_ref in chains:
        hsum = hf_ref[0].astype(F32) + hb_ref[0].astype(F32)
        branch.append((_dot(cv_ref[0], pw_ref[0]), _dot(at_ref[0], no_ref[0]),
                       _dot((rg_ref[0].astype(F32) * hsum).astype(BF16), ro_ref[0])))
    ys = []
    for (conv, na, rnn), chain in zip(branch, chains):
        gt_ref = chain[-1]
        m = (gt_ref[0, :, 0:d].astype(F32) * conv + gt_ref[0, :, d:2 * d].astype(F32) * na
             + gt_ref[0, :, 2 * d:3 * d].astype(F32) * rnn)
        ys.append(_dot(m.astype(BF16), ow_ref[0]))
    hms = []
    for y, (h, x_refs, g1_ref, sh2_ref, sc2_ref, *_) in zip(ys, chains):
        if n_src == 1:
            x_in = x_refs[0][0]
        else:
            tile = lax.rem(pl.program_id(0) * MERGE_TILES + h, tiles_per_batch)
            x_in = jnp.where(tile < n_ctx_tiles, x_refs[0][0], x_refs[1][0])
        x = x_in + g1_ref[0] * y
        o_ref[h * tm:(h + 1) * tm, :] = x
        hm = x * lax.rsqrt(jnp.mean(x * x, axis=-1, keepdims=True) + EPS) * n2_ref[...]
        hms.append(hm * (1.0 + sc2_ref[0]) + sh2_ref[0])
    for h, hm in enumerate(hms):
        h2, info, cnt = _route(hm, rw_ref, rb_ref, tri_ref)
        h2_ref[h * tm:(h + 1) * tm, :] = h2
        info_ref[:, h * tm:(h + 1) * tm] = info
        cnt_ref[h] = cnt


def _merge(src, modtab, n2, cv, at, hf, hb, rg, gt, pw, no, ro, ow, layer, rw, rb, tri):
    bsz, l, _ = cv.shape
    d = ow.shape[2]
    tm = TOKEN_TILE
    nt = l // tm
    assert (bsz * nt) % MERGE_TILES == 0
    is_pair = isinstance(src, tuple)
    nc = src[0].shape[1] // tm if is_pair else 0
    in_specs, args = [], []
    for h in range(MERGE_TILES):
        def where(k, h=h):
            t = k * MERGE_TILES + h
            return t // nt, lax.rem(t, nt), t

        def tile(n, where=where):
            return pl.BlockSpec((1, tm, n), lambda k: where(k)[:2] + (0,))

        def mod(which, where=where):
            def index(k):
                b, i, _ = where(k)
                return ((2 * b + jnp.minimum(i, 1)) * N_MOD + which, 0, 0)
            return pl.BlockSpec((1, 1, d), index)

        if is_pair:
            in_specs += [pl.BlockSpec((1, tm, d), lambda k, where=where: (where(k)[0], jnp.minimum(where(k)[1], nc - 1), 0)),
                         pl.BlockSpec((1, tm, d), lambda k, where=where: (where(k)[0], jnp.maximum(where(k)[1] - nc, 0), 0))]
            args += list(src)
        else:
            in_specs.append(tile(d))
            args.append(src)
        in_specs += [mod(2), mod(3), mod(4), tile(cv.shape[2]), tile(at.shape[2]), tile(hf.shape[2]),
                     tile(hb.shape[2]), tile(rg.shape[2]), tile(3 * d)]
        args += [modtab, modtab, modtab, cv, at, hf, hb, rg, gt]
    in_specs += [_const_spec((1, d)), _layer_spec(pw, layer), _layer_spec(no, layer), _layer_spec(ro, layer),
                 _layer_spec(ow, layer), _const_spec(rw.shape), _const_spec(rb.shape), _const_spec(tri.shape)]
    args += [n2, pw, no, ro, ow, rw, rb, tri]
    rows = MERGE_TILES * tm
    xa, h2, info, cnt = pl.pallas_call(
        functools.partial(_merge_kernel, 2 if is_pair else 1, nc, nt),
        grid=(bsz * nt // MERGE_TILES,),
        in_specs=in_specs,
        out_specs=[pl.BlockSpec((rows, d), lambda k: (k, 0)), pl.BlockSpec((rows, d), lambda k: (k, 0)),
                   pl.BlockSpec((SUBLANES, rows), lambda k: (0, k)),
                   pl.BlockSpec((MERGE_TILES, N_EXPERTS, LANES), lambda k: (k, 0, 0))],
        out_shape=[jax.ShapeDtypeStruct((bsz * l, d), F32), jax.ShapeDtypeStruct((bsz * l, d), BF16),
                   jax.ShapeDtypeStruct((SUBLANES, bsz * l), F32),
                   jax.ShapeDtypeStruct((bsz * nt, N_EXPERTS, LANES), jnp.int32)],
        compiler_params=_params(("arbitrary",)),
        name="merge_route",
    )(*args)
    return xa.reshape(bsz, l, d), h2.reshape(bsz, l, d), info, cnt


def _route_plan(cnt, n_tiles):
    pc = (cnt + (ROW_ALIGN - 1)) // ROW_ALIGN * ROW_ALIGN
    ls = jnp.cumsum(pc, axis=1) - pc
    seg_tot = jnp.sum(pc, axis=0)
    reg = (seg_tot + (EXPERT_TILE - 1)) // EXPERT_TILE * EXPERT_TILE
    reg_end = jnp.cumsum(reg)
    base = reg_end - reg
    gs = base[None, :] + jnp.cumsum(pc, axis=0) - pc
    n_used = reg_end[-1] // EXPERT_TILE
    tiles = jnp.minimum(jnp.arange(n_tiles, dtype=jnp.int32), n_used - 1)
    tile_expert = jnp.minimum(jnp.sum(tiles[:, None] >= (reg_end // EXPERT_TILE)[None, :], axis=1),
                              N_EXPERTS - 1).astype(jnp.int32)
    i32 = lambda a: a.astype(jnp.int32).reshape(-1)
    return dict(pc=i32(pc), ls=i32(ls), gs=i32(gs), tot=i32(jnp.sum(pc, axis=1)),
                tail_start=i32(base + seg_tot), tail_len=i32(reg - seg_tot),
                tile_expert=tile_expert, n_used=i32(n_used))


def _for_each_piece(n, sizes, fn):
    off = jnp.int32(0)
    for size in sizes:
        hit = (n & size) != 0

        @pl.when(hit)
        def _(off=off, size=size):
            fn(pl.multiple_of(off, ROW_ALIGN), size)

        off = off + jnp.where(hit, size, 0)


def _window_copies(pc_ref, ls_ref, gs_ref, win, staged_at, sorted_at, to_sorted, sem, fn):
    for e in range(N_EXPERTS):
        n = pc_ref[win * N_EXPERTS + e]
        src0 = ls_ref[win * N_EXPERTS + e]
        dst0 = gs_ref[win * N_EXPERTS + e]

        def piece(off, size, src0=src0, dst0=dst0):
            a = staged_at(pl.multiple_of(src0 + off, ROW_ALIGN), size)
            b = sorted_at(pl.multiple_of(dst0 + off, ROW_ALIGN), size)
            fn(pltpu.make_async_copy(a, b, sem) if to_sorted else pltpu.make_async_copy(b, a, sem))

        small = n & (SEG_SPLIT - 1)
        _for_each_piece(small, tuple(s for s in SEG_SIZES if s < SEG_SPLIT), piece)

        @pl.when(n >= SEG_SPLIT)
        def _(n=n, small=small, piece=piece):
            _for_each_piece(n - small, tuple(s for s in SEG_SIZES if s >= SEG_SPLIT),
                            lambda off, size: piece(off + small, size))


def _wait_rows(n, desc):
    for size in WAIT_SIZES:
        @pl.when((n & size) != 0)
        def _(size=size):
            desc(size).wait()


def _one_hot_rows(info_ref, rows, lo_val, hi_val):
    tm = info_ref.shape[1]
    rid = lax.broadcasted_iota(jnp.int32, (rows, tm), 0).astype(F32)
    return jnp.where(rid == info_ref[0:1, :], lo_val, jnp.where(rid == info_ref[1:2, :], hi_val, 0.0))


def _dispatch_kernel(pc_ref, ls_ref, gs_ref, tot_ref, ts_ref, tl_ref, nu_ref, h_ref, info_ref, xs_ref,
                     buf_ref, zero_ref, sem_ref, zsem_ref):
    w = pl.program_id(0)
    nw = pl.num_programs(0)
    n_slots = buf_ref.shape[0]
    slot_of = lambda win: lax.rem(win, n_slots)
    slot = slot_of(w)
    rows = buf_ref.shape[1]
    p = _one_hot_rows(info_ref, rows, 1.0, 1.0).astype(BF16)
    buf_ref[slot] = _dot(p, h_ref[...]).astype(BF16)

    def start_copies(win):
        sl = slot_of(win)
        _window_copies(pc_ref, ls_ref, gs_ref, win,
                       lambda r, n: buf_ref.at[sl, pl.ds(r, n)], lambda r, n: xs_ref.at[pl.ds(r, n)],
                       True, sem_ref.at[sl], lambda c: c.start())

    def wait_copies(win):
        sl = slot_of(win)
        _wait_rows(tot_ref[win], lambda n: pltpu.make_async_copy(
            buf_ref.at[sl, pl.ds(0, n)], xs_ref.at[pl.ds(0, n)], sem_ref.at[sl]))

    lag = n_slots - 1

    @pl.when(w >= lag)
    def _():
        wait_copies(w - lag)

    start_copies(w)

    @pl.when(w == nw - 1)
    def _():
        for back in range(lag - 1, -1, -1):
            @pl.when(w >= back)
            def _(back=back):
                wait_copies(w - back)
        zero_ref[...] = jnp.zeros_like(zero_ref)

        def tails(fn):
            for e in range(N_EXPERTS):
                start = ts_ref[e]

                def piece(off, size, start=start):
                    fn(pltpu.make_async_copy(zero_ref.at[pl.ds(0, size)],
                                             xs_ref.at[pl.ds(pl.multiple_of(start + off, ROW_ALIGN), size)],
                                             zsem_ref.at[0]))

                _for_each_piece(tl_ref[e], TAIL_SIZES, piece)

        tails(lambda c: c.start())
        tails(lambda c: c.wait())

        fill = zero_ref.shape[0]
        used = nu_ref[0] * EXPERT_TILE

        def spare(k):
            return pltpu.make_async_copy(
                zero_ref, xs_ref.at[pl.ds(pl.multiple_of(used + k * fill, ROW_ALIGN), fill)], zsem_ref.at[0])

        n_fill = (xs_ref.shape[0] - used) // fill
        lax.fori_loop(0, n_fill, lambda k, c: (spare(k).start(), c)[1], 0)
        lax.fori_loop(0, n_fill, lambda k, c: (spare(k).wait(), c)[1], 0)


def _dispatch(h2, info, plan, n_rows):
    t, d = h2.shape
    tm = TOKEN_TILE
    rows = STAGE_ROWS
    grid_spec = pltpu.PrefetchScalarGridSpec(
        num_scalar_prefetch=7,
        grid=(t // tm,),
        in_specs=[pl.BlockSpec((tm, d), lambda w, *_: (w, 0)),
                  pl.BlockSpec((SUBLANES, tm), lambda w, *_: (0, w))],
        out_specs=pl.BlockSpec(memory_space=pl.ANY),
        scratch_shapes=[pltpu.VMEM((MOE_SLOTS, rows, d), BF16), pltpu.VMEM((TAIL_SIZES[0], d), BF16),
                        pltpu.SemaphoreType.DMA((MOE_SLOTS,)), pltpu.SemaphoreType.DMA((1,))])
    return pl.pallas_call(
        _dispatch_kernel,
        grid_spec=grid_spec,
        out_shape=jax.ShapeDtypeStruct((n_rows, d), BF16),
        compiler_params=_params(("arbitrary",)),
        name="moe_dispatch",
    )(plan["pc"], plan["ls"], plan["gs"], plan["tot"], plan["tail_start"], plan["tail_len"], plan["n_used"],
      h2, info)


def _experts_kernel(te_ref, nu_ref, xs_ref, w1_ref, w3_ref, w2_ref, ys_ref, w13_ref, w2b_ref, z_ref):
    i = pl.program_id(0)
    de = w1_ref.shape[3]
    prev = te_ref[jnp.maximum(i - 1, 0)]

    @pl.when(jnp.logical_or(i == 0, te_ref[i] != prev))
    def _():
        w13_ref[:, 0:de] = w1_ref[0, 0].astype(BF16)
        w13_ref[:, de:2 * de] = w3_ref[0, 0].astype(BF16)
        w2b_ref[...] = w2_ref[0, 0].astype(BF16)

    @pl.when(i >= nu_ref[0])
    def _():
        ys_ref[...] = jnp.zeros_like(ys_ref)

    @pl.when(i < nu_ref[0])
    def _():
        for blk, r0 in enumerate(range(0, xs_ref.shape[0], EXPERT_SUB)):
            z_ref[blk] = _dot(xs_ref[r0:r0 + EXPERT_SUB, :], w13_ref[...])
        for blk, r0 in enumerate(range(0, xs_ref.shape[0], EXPERT_SUB)):
            he = _silu(z_ref[blk, :, 0:de]) * z_ref[blk, :, de:2 * de]
            ys_ref[r0:r0 + EXPERT_SUB, :] = _dot(he.astype(BF16), w2b_ref[...]).astype(BF16)


def _experts(xs, plan, w1, w3, w2, layer):
    n_rows, d = xs.shape
    de = w1.shape[3]
    te, nu = plan["tile_expert"], plan["n_used"]
    tm = EXPERT_TILE
    grid_spec = pltpu.PrefetchScalarGridSpec(
        num_scalar_prefetch=2,
        grid=(n_rows // tm,),
        in_specs=[pl.BlockSpec((tm, d), lambda i, te, nu: (jnp.minimum(i, nu[0] - 1), 0)),
                  pl.BlockSpec((1, 1, d, de), lambda i, te, nu: (layer, te[i], 0, 0)),
                  pl.BlockSpec((1, 1, d, de), lambda i, te, nu: (layer, te[i], 0, 0)),
                  pl.BlockSpec((1, 1, de, d), lambda i, te, nu: (layer, te[i], 0, 0))],
        out_specs=pl.BlockSpec((tm, d), lambda i, te, nu: (i, 0)),
        scratch_shapes=[pltpu.VMEM((d, 2 * de), BF16), pltpu.VMEM((de, d), BF16),
                        pltpu.VMEM((tm // EXPERT_SUB, EXPERT_SUB, 2 * de), F32)])
    return pl.pallas_call(
        _experts_kernel,
        grid_spec=grid_spec,
        out_shape=jax.ShapeDtypeStruct((n_rows, d), BF16),
        compiler_params=_params(("arbitrary",)),
        name="moe_experts",
    )(te, nu, xs, w1, w3, w2)


def _combine_kernel(pc_ref, ls_ref, gs_ref, tot_ref, x_ref, g2_ref, info_ref, ys_ref, o_ref, buf_ref, sem_ref):
    nt = pl.num_programs(1)
    w = pl.program_id(0) * nt + pl.program_id(1)
    nw = pl.num_programs(0) * nt
    n_slots = buf_ref.shape[0]
    slot = lax.rem(w, n_slots)
    rows = buf_ref.shape[1]
    ahead = n_slots - 1

    def start_copies(win):
        sl = lax.rem(win, n_slots)
        _window_copies(pc_ref, ls_ref, gs_ref, win,
                       lambda r, n: buf_ref.at[sl, pl.ds(r, n)], lambda r, n: ys_ref.at[pl.ds(r, n)],
                       False, sem_ref.at[sl], lambda c: c.start())

    @pl.when(w == 0)
    def _():
        buf_ref[...] = jnp.zeros_like(buf_ref)
        for first in range(ahead):
            @pl.when(first < nw)
            def _(first=first):
                start_copies(first)

    @pl.when(w + ahead < nw)
    def _():
        start_copies(w + ahead)

    _wait_rows(tot_ref[w], lambda n: pltpu.make_async_copy(
        ys_ref.at[pl.ds(0, n)], buf_ref.at[slot, pl.ds(0, n)], sem_ref.at[slot]))
    pw =_one_hot_rows(info_ref, rows, info_ref[2:3, :], info_ref[3:4, :]).astype(BF16)
    moe = lax.dot_general(pw, buf_ref[slot], (((0,), (0,)), ((), ())), preferred_element_type=F32)
    o_ref[0] = x_ref[0] + g2_ref[0] * moe


def _combine(xa, modtab, info, ys, plan, drop):
    bsz, l, d = xa.shape
    tm = TOKEN_TILE
    nt = l // tm
    grid_spec = pltpu.PrefetchScalarGridSpec(
        num_scalar_prefetch=4,
        grid=(bsz, nt),
        in_specs=[pl.BlockSpec((1, tm, d), lambda b, i, *_: (b, i, 0)),
                  _mod_spec(d, 5),
                  pl.BlockSpec((SUBLANES, tm), lambda b, i, *_: (0, b * nt + i)),
                  pl.BlockSpec(memory_space=pl.ANY)],
        out_specs=pl.BlockSpec((1, tm, d), lambda b, i, *_: (b, jnp.maximum(i - drop, 0), 0)),
        scratch_shapes=[pltpu.VMEM((MOE_SLOTS, STAGE_ROWS, d), BF16), pltpu.SemaphoreType.DMA((MOE_SLOTS,))])
    return pl.pallas_call(
        _combine_kernel,
        grid_spec=grid_spec,
        out_shape=jax.ShapeDtypeStruct((bsz, l - drop * tm, d), F32),
        compiler_params=_params(("arbitrary", "arbitrary")),
        name="moe_combine",
    )(plan["pc"], plan["ls"], plan["gs"], plan["tot"], xa, modtab, info, ys)


def _moe_sparse(xa, modtab, h2, info, cnt, w1, w3, w2, layer, drop):
    bsz, l, d = xa.shape
    t = bsz * l
    n_win = t // TOKEN_TILE
    bound = TOP_K * t + n_win * N_EXPERTS * (ROW_ALIGN - 1) + N_EXPERTS * (EXPERT_TILE - ROW_ALIGN)
    n_tiles = -(-bound // EXPERT_TILE)
    plan = _route_plan(cnt[:, :, 0], n_tiles)
    xs = _dispatch(h2.reshape(t, d), info, plan, n_tiles * EXPERT_TILE)
    ys = _experts(xs, plan, w1, w3, w2, layer)
    return _combine(xa, modtab, info, ys, plan, drop)


def _block_diag(w):
    n, k, _ = w.shape
    eye = jnp.eye(n, dtype=w.dtype)
    return jnp.einsum("nij,nm->nimj", w, eye).reshape(n * k, n * k)


def kernel(x, c, ctx, c_ctx, router_w, router_bias, mod_w, mod_b, norm1_g, norm2_g, in_w, in_b,
           conv_dw_w, conv_dw_b, conv_ln_g, conv_ln_b, conv_pw_w, na_q_g, na_k_g, na_rpb, na_out_w,
           rnn_conv_w, rnn_conv_b, rnn_wa, rnn_ba, rnn_wx, rnn_bx, rnn_lam, rnn_out_w, out_w,
           exp_w1, exp_w3, exp_w2):
    bsz, s, d = x.shape
    n_ctx = ctx.shape[1]
    depth = mod_w.shape[0]
    dc = conv_pw_w.shape[1]
    dn = na_out_w.shape[1]
    dr = rnn_out_w.shape[1]
    sizes = (dc, dn, dr, d)
    assert n_ctx % TOKEN_TILE == 0 and s % TOKEN_TILE == 0 and s % GRID_W == 0
    assert dn == N_HEADS * HEAD_DIM and s // GRID_W >= NA_ROWS

    xa = (ctx, x)
    seq = n_ctx + s

    c_rows = -(-(bsz + 1) // SUBLANES) * SUBLANES
    c_all = jnp.zeros((c_rows, d), F32).at[:bsz].set(c).at[bsz].set(c_ctx)
    mods = _mod_vectors(c_all, mod_w, mod_b)

    head_mean = _block_diag(jnp.full((N_HEADS, HEAD_DIM, HEAD_DIM), 1.0 / HEAD_DIM, F32)).astype(BF16)
    rw_hi = router_w.astype(BF16)
    rw_lo = (router_w - rw_hi.astype(F32)).astype(BF16)
    rw = jnp.zeros((2 * d, LANES), BF16)
    rw = rw.at[:d, 0:N_EXPERTS].set(rw_hi).at[:d, N_EXPERTS:2 * N_EXPERTS].set(rw_lo)
    rw = rw.at[d:, 0:N_EXPERTS].set(rw_hi)
    rb = router_bias.reshape(N_EXPERTS, 1)
    tri = jnp.asarray(np.triu(np.ones((TOKEN_TILE, TOKEN_TILE), np.float32), k=1), BF16)
    shift_a, shift_b = _conv_shift_matrices(TOKEN_TILE + 2 * HALO)
    perm, perm_t = _scan_permutation(TOKEN_TILE)

    in_w_b = in_w.astype(BF16)
    pw_b, no_b, ro_b, ow_b = (a.astype(BF16) for a in (conv_pw_w, na_out_w, rnn_out_w, out_w))

    for l in range(depth):
        ml = mods[l].reshape(c_rows, N_MOD, d)
        modtab = jnp.stack([jnp.broadcast_to(ml[bsz], (bsz, N_MOD, d)), ml[:bsz]], axis=1)
        modtab = modtab.reshape(bsz * 2 * N_MOD, 1, d)

        qg = (jnp.tile(na_q_g[l], N_HEADS) * HEAD_DIM ** -0.5).reshape(1, dn)
        kg = jnp.tile(na_k_g[l], N_HEADS).reshape(1, dn)
        u, q, k, v, rx, rg, gt = _inproj(xa, bsz, seq, modtab, norm1_g[l].reshape(1, d), in_w_b, l,
                                         in_b[l].reshape(1, -1), qg, kg, head_mean, sizes)

        cw = jnp.zeros((CONV_K + 1, dc), F32).at[:CONV_K].set(conv_dw_w[l])
        cv = _conv_branch(u, shift_a, shift_b, cw, conv_dw_b[l].reshape(1, dc),
                          conv_ln_g[l].reshape(1, dc), conv_ln_b[l].reshape(1, dc))

        at = _attention(q, k, v, _bias_strips(na_rpb[l]), n_ctx)

        wg = jnp.stack([jnp.concatenate([_block_diag(rnn_wa[l, dn_]), _block_diag(rnn_wx[l, dn_])], axis=1)
                        for dn_ in range(2)]).astype(BF16)
        bg = jnp.concatenate([rnn_ba[l], rnn_bx[l]], axis=-1).reshape(2, 1, 2 * dr)
        nsp = (-LRU_C * jax.nn.softplus(-rnn_lam[l])).reshape(2, 1, dr)
        rcw = jnp.zeros((2, SUBLANES, dr), F32).at[:, :RNN_CONV_K].set(rnn_conv_w[l])
        hf, hb = _rglru(rx, perm, perm_t, rcw, rnn_conv_b[l].reshape(2, 1, dr), wg, bg, nsp)

        xa, h2, info, cnt = _merge(xa, modtab, norm2_g[l].reshape(1, d), cv, at, hf, hb, rg, gt,
                                   pw_b, no_b, ro_b, ow_b, l, rw, rb, tri)

        drop = n_ctx // TOKEN_TILE if l == depth - 1 else 0
        xa = _moe_sparse(xa, modtab, h2, info, cnt, exp_w1, exp_w3, exp_w2, l, drop)

    return xa
```
